```python
import jax, jax.numpy as jnp
from jax import lax
import numpy as np

D_MODEL = 2048
BATCH = 4
SEQ = 4096
DEPTH = 4
DEC_BATCH = 8
DEC_SEQ = 32
PAST_LEN = 1024

CHUNK = 64
Q_BLOCK = 128
ROPE_THETA = 10000.0
EPS = 1e-6
N_MIXERS = 3
HEAD_DIM = 128
MIX_WIDTH = D_MODEL
A_HEADS = MIX_WIDTH // HEAD_DIM
A_KV_HEADS = 4
IDX_HEADS = 16
IDX_DIM = 64
TOPK_MAX = 256
B_HEADS = MIX_WIDTH // (2 * HEAD_DIM)
B_QK_DIM = HEAD_DIM
B_V_DIM = 2 * HEAD_DIM
C_HEADS = MIX_WIDTH // HEAD_DIM

A_SIZES = (A_HEADS * HEAD_DIM, A_KV_HEADS * HEAD_DIM, A_KV_HEADS * HEAD_DIM,
           IDX_HEADS * IDX_DIM, IDX_DIM, IDX_HEADS, MIX_WIDTH)
B_SIZES = (2 * B_HEADS * B_QK_DIM, 2 * B_HEADS * B_QK_DIM, B_HEADS * B_V_DIM, MIX_WIDTH)
C_SIZES = (C_HEADS * HEAD_DIM, C_HEADS * HEAD_DIM, C_HEADS * HEAD_DIM, C_HEADS, MIX_WIDTH)

kernel_name = "hybrid_dsa_diff_fox_stream_step"


def rms_norm(x, g):
    xf = x.astype(jnp.float32)
    y = xf * lax.rsqrt(jnp.mean(xf * xf, axis=-1, keepdims=True) + EPS)
    return (y * g.astype(jnp.float32)).astype(x.dtype)


def rope(x, pos):
    d = x.shape[-1]
    half = d // 2
    inv = ROPE_THETA ** (-2.0 * jnp.arange(half, dtype=jnp.float32) / d)
    ang = pos.astype(jnp.float32)[:, None] * inv[None, :]
    cos = jnp.cos(ang)[None, :, None, :]
    sin = jnp.sin(ang)[None, :, None, :]
    xf = x.astype(jnp.float32)
    x1, x2 = xf[..., :half], xf[..., half:]
    return jnp.concatenate([x1 * cos - x2 * sin, x1 * sin + x2 * cos], axis=-1).astype(x.dtype)


def chunk_mask(q_pos, k_pos):
    return (k_pos[None, :] // CHUNK) <= (q_pos[:, None] // CHUNK)


def split_cols(p, sizes):
    return jnp.split(p, np.cumsum(sizes)[:-1].tolist(), axis=-1)


def sweep_blocks(fn, qs, pos):
    T = qs[0].shape[1]
    nb = T // Q_BLOCK
    def split(a):
        return jnp.swapaxes(a.reshape(a.shape[0], nb, Q_BLOCK, *a.shape[2:]), 0, 1)
    out = lax.map(lambda a: fn(*a[0], a[1]), (tuple(split(a) for a in qs), pos.reshape(nb, Q_BLOCK)))
    out = jnp.swapaxes(out, 0, 1)
    return out.reshape(out.shape[0], T, *out.shape[3:])


def gated_out(o, g, w_out):
    return (o * jax.nn.silu(g)) @ w_out


def a_project(h, pos, w_in, qn, kn, ikn):
    B, T, _ = h.shape
    q, k, v, qi, ki, wi, g = split_cols(h @ w_in, A_SIZES)
    q = rope(rms_norm(q.reshape(B, T, A_HEADS, HEAD_DIM), qn), pos)
    k = rope(rms_norm(k.reshape(B, T, A_KV_HEADS, HEAD_DIM), kn), pos)
    v = v.reshape(B, T, A_KV_HEADS, HEAD_DIM)
    qi = rope(qi.reshape(B, T, IDX_HEADS, IDX_DIM), pos)
    ki = rope(rms_norm(ki, ikn)[:, :, None, :], pos)[:, :, 0, :]
    return q, k, v, qi, ki, wi, g


def a_attend(q, qi, wi, q_pos, k, v, ki, k_pos, topk):
    B, Tq = q.shape[:2]
    sc = jnp.einsum('bthd,bsd->bths', qi, ki, preferred_element_type=jnp.float32) * (IDX_DIM ** -0.5)
    score = jnp.einsum('bths,bth->bts', jax.nn.relu(sc), wi.astype(jnp.float32)) * (IDX_HEADS ** -0.5)
    score = jnp.where(chunk_mask(q_pos, k_pos)[None], score, -jnp.inf)
    top_val, top_idx = lax.top_k(score, topk)
    valid = jnp.isfinite(top_val)
    gather = jax.vmap(lambda rows, ids: rows[ids])
    kg = gather(k, top_idx)
    vg = gather(v, top_idx)
    qg = q.reshape(B, Tq, A_KV_HEADS, A_HEADS // A_KV_HEADS, HEAD_DIM)
    logits = jnp.einsum('btgrd,btkgd->btgrk', qg, kg, preferred_element_type=jnp.float32) * (HEAD_DIM ** -0.5)
    logits = jnp.where(valid[:, :, None, None, :], logits, -jnp.inf)
    p = jax.nn.softmax(logits, axis=-1).astype(v.dtype)
    o = jnp.einsum('btgrk,btkgd->btgrd', p, vg)
    return o.reshape(B, Tq, A_HEADS * HEAD_DIM)


def mixer_a(layer, xp, xs, ck, cv, cki, norm, w_in, w_out, qn, kn, ikn):
    T = xp.shape[1]
    pos_p = jnp.arange(T)
    q, k, v, qi, ki, wi, g = a_project(rms_norm(xp, norm), pos_p, w_in, qn, kn, ikn)
    topk_p = min(TOPK_MAX, T // 4)
    o = sweep_blocks(lambda qb, qib, wib, pb: a_attend(qb, qib, wib, pb, k, v, ki, pos_p, topk_p),
                     (q, qi, wi), pos_p)
    yp = xp + gated_out(o, g, w_out)

    S = xs.shape[1]
    L = PAST_LEN + S
    pos_s = PAST_LEN + jnp.arange(S)
    q2, k2, v2, qi2, ki2, wi2, g2 = a_project(rms_norm(xs, norm), pos_s, w_in, qn, kn, ikn)
    o2 = a_attend(q2, qi2, wi2, pos_s,
                  jnp.concatenate([ck, k2], axis=1), jnp.concatenate([cv, v2], axis=1),
                  jnp.concatenate([cki, ki2], axis=1), jnp.arange(L), min(TOPK_MAX, L // 4))
    ys = xs + gated_out(o2, g2, w_out)
    return yp, ys, (k, v, ki, k2, v2, ki2)


def b_project(h, pos, w_in, qn, kn):
    B, T, _ = h.shape
    q, k, v, g = split_cols(h @ w_in, B_SIZES)
    q = rope(rms_norm(q.reshape(B, T, 2 * B_HEADS, B_QK_DIM), qn), pos)
    k = rope(rms_norm(k.reshape(B, T, 2 * B_HEADS, B_QK_DIM), kn), pos)
    v = v.reshape(B, T, B_HEADS, B_V_DIM)
    return q, k, v, g


def b_attend(q, k, v, q_pos, k_pos, lam):
    B, Tq = q.shape[:2]
    S = k.shape[1]
    logits = jnp.einsum('bthd,bshd->bhts', q, k, preferred_element_type=jnp.float32) * (B_QK_DIM ** -0.5)
    logits = jnp.where(chunk_mask(q_pos, k_pos)[None, None], logits, -jnp.inf)
    p = jax.nn.softmax(logits, axis=-1).reshape(B, B_HEADS, 2, Tq, S)
    a = (p[:, :, 0] - lam * p[:, :, 1]).astype(v.dtype)
    return jnp.einsum('bhts,bshe->bthe', a, v)


def mixer_b(layer, xp, xs, ck, cv, norm, w_in, w_out, qn, kn, lq1, lk1, lq2, lk2, subln):
    lam_init = 0.8 - 0.6 * float(np.exp(-0.3 * layer))
    f32 = jnp.float32
    lam = (jnp.exp(jnp.sum(lq1.astype(f32) * lk1.astype(f32)))
           - jnp.exp(jnp.sum(lq2.astype(f32) * lk2.astype(f32))) + lam_init)

    def post(o):
        o = rms_norm(o, subln) * (1.0 - lam_init)
        return o.reshape(o.shape[0], o.shape[1], MIX_WIDTH)

    T = xp.shape[1]
    pos_p = jnp.arange(T)
    q, k, v, g = b_project(rms_norm(xp, norm), pos_p, w_in, qn, kn)
    o = sweep_blocks(lambda qb, pb: b_attend(qb, k, v, pb, pos_p, lam), (q,), pos_p)
    yp = xp + gated_out(post(o), g, w_out)

    S = xs.shape[1]
    pos_s = PAST_LEN + jnp.arange(S)
    q2, k2, v2, g2 = b_project(rms_norm(xs, norm), pos_s, w_in, qn, kn)
    o2 = b_attend(q2, jnp.concatenate([ck, k2], axis=1), jnp.concatenate([cv, v2], axis=1),
                  pos_s, jnp.arange(PAST_LEN + S), lam)
    ys = xs + gated_out(post(o2), g2, w_out)
    return yp, ys, (k, v, k2, v2)


def c_project(h, w_in, qn, kn, fb):
    B, T, _ = h.shape
    q, k, v, fl, g = split_cols(h @ w_in, C_SIZES)
    q = rms_norm(q.reshape(B, T, C_HEADS, HEAD_DIM), qn)
    k = rms_norm(k.reshape(B, T, C_HEADS, HEAD_DIM), kn)
    v = v.reshape(B, T, C_HEADS, HEAD_DIM)
    logf = jax.nn.log_sigmoid(fl.astype(jnp.float32) + fb.astype(jnp.float32))
    return q, k, v, logf, g


def c_attend(q, k, v, cq, ck, q_pos, k_pos):
    B, Tq = q.shape[:2]
    logits = jnp.einsum('bthd,bshd->bhts', q, k, preferred_element_type=jnp.float32) * (HEAD_DIM ** -0.5)
    logits = logits + (jnp.swapaxes(cq, 1, 2)[..., :, None] - jnp.swapaxes(ck, 1, 2)[..., None, :])
    logits = jnp.where((k_pos[None, :] <= q_pos[:, None])[None, None], logits, -jnp.inf)
    p = jax.nn.softmax(logits, axis=-1).astype(v.dtype)
    o = jnp.einsum('bhts,bshd->bthd', p, v)
    return o.reshape(B, Tq, C_HEADS * HEAD_DIM)


def mixer_c(layer, xp, xs, ck, cv, clogf, norm, w_in, w_out, qn, kn, fb):
    T = xp.shape[1]
    pos_p = jnp.arange(T)
    q, k, v, logf, g = c_project(rms_norm(xp, norm), w_in, qn, kn, fb)
    c = jnp.cumsum(logf, axis=1)
    o = sweep_blocks(lambda qb, cqb, pb: c_attend(qb, k, v, cqb, c, pb, pos_p), (q, c), pos_p)
    yp = xp + gated_out(o, g, w_out)

    S = xs.shape[1]
    pos_s = PAST_LEN + jnp.arange(S)
    q2, k2, v2, logf2, g2 = c_project(rms_norm(xs, norm), w_in, qn, kn, fb)
    c2 = jnp.cumsum(jnp.concatenate([clogf.astype(jnp.float32), logf2], axis=1), axis=1)
    o2 = c_attend(q2, jnp.concatenate([ck, k2], axis=1), jnp.concatenate([cv, v2], axis=1),
                  c2[:, PAST_LEN:], c2, pos_s, jnp.arange(PAST_LEN + S))
    ys = xs + gated_out(o2, g2, w_out)
    return yp, ys, (k, v, logf.astype(xp.dtype), k2, v2, logf2.astype(xs.dtype))


def setup_inputs(seed: int = 0) -> dict:
    key = jax.random.key(seed)
    ks = iter(jax.random.split(key, 64))
    def nrm(shape, scale=1.0):
        return jax.random.normal(next(ks), shape, jnp.float32) * scale
    def gain(n):
        return 1.0 + 0.02 * nrm((n,))
    inp = {}
    inp["x_prompt"] = nrm((BATCH, SEQ, D_MODEL))
    inp["x_sample"] = nrm((DEC_BATCH, DEC_SEQ, D_MODEL))
    for i in range(DEPTH):
        kind = i % N_MIXERS
        if kind == 0:
            inp[f"cache_l{i}_k"] = nrm((DEC_BATCH, PAST_LEN, A_KV_HEADS, HEAD_DIM))
            inp[f"cache_l{i}_v"] = nrm((DEC_BATCH, PAST_LEN, A_KV_HEADS, HEAD_DIM))
            inp[f"cache_l{i}_kidx"] = nrm((DEC_BATCH, PAST_LEN, IDX_DIM))
        elif kind == 1:
            inp[f"cache_l{i}_k"] = nrm((DEC_BATCH, PAST_LEN, 2 * B_HEADS, B_QK_DIM))
            inp[f"cache_l{i}_v"] = nrm((DEC_BATCH, PAST_LEN, B_HEADS, B_V_DIM))
        else:
            inp[f"cache_l{i}_k"] = nrm((DEC_BATCH, PAST_LEN, C_HEADS, HEAD_DIM))
            inp[f"cache_l{i}_v"] = nrm((DEC_BATCH, PAST_LEN, C_HEADS, HEAD_DIM))
            inp[f"cache_l{i}_logf"] = jax.nn.log_sigmoid(2.5 + nrm((DEC_BATCH, PAST_LEN, C_HEADS)))
    for i in range(DEPTH):
        kind = i % N_MIXERS
        sizes = (A_SIZES, B_SIZES, C_SIZES)[kind]
        inp[f"l{i}_norm"] = gain(D_MODEL)
        inp[f"l{i}_w_in"] = nrm((D_MODEL, sum(sizes)), D_MODEL ** -0.5)
        inp[f"l{i}_w_out"] = nrm((MIX_WIDTH, D_MODEL), MIX_WIDTH ** -0.5)
        inp[f"l{i}_q_norm"] = gain(HEAD_DIM)
        inp[f"l{i}_k_norm"] = gain(HEAD_DIM)
        if kind == 0:
            inp[f"l{i}_idx_k_norm"] = gain(IDX_DIM)
        elif kind == 1:
            inp[f"l{i}_lambda_q1"] = nrm((B_QK_DIM,), 0.1)
            inp[f"l{i}_lambda_k1"] = nrm((B_QK_DIM,), 0.1)
            inp[f"l{i}_lambda_q2"] = nrm((B_QK_DIM,), 0.1)
            inp[f"l{i}_lambda_k2"] = nrm((B_QK_DIM,), 0.1)
            inp[f"l{i}_subln"] = gain(B_V_DIM)
        else:
            inp[f"l{i}_forget_bias"] = jax.random.uniform(next(ks), (C_HEADS,), jnp.float32, 1.0, 4.0)
    return inp


def reference(x_prompt, x_sample,
              cache_l0_k, cache_l0_v, cache_l0_kidx,
              cache_l1_k, cache_l1_v,
              cache_l2_k, cache_l2_v, cache_l2_logf,
              cache_l3_k, cache_l3_v, cache_l3_kidx,
              l0_norm, l0_w_in, l0_w_out, l0_q_norm, l0_k_norm, l0_idx_k_norm,
              l1_norm, l1_w_in, l1_w_out, l1_q_norm, l1_k_norm,
              l1_lambda_q1, l1_lambda_k1, l1_lambda_q2, l1_lambda_k2, l1_subln,
              l2_norm, l2_w_in, l2_w_out, l2_q_norm, l2_k_norm, l2_forget_bias,
              l3_norm, l3_w_in, l3_w_out, l3_q_norm, l3_k_norm, l3_idx_k_norm):
    layer_args = (
        (cache_l0_k, cache_l0_v, cache_l0_kidx, l0_norm, l0_w_in, l0_w_out, l0_q_norm, l0_k_norm, l0_idx_k_norm),
        (cache_l1_k, cache_l1_v, l1_norm, l1_w_in, l1_w_out, l1_q_norm, l1_k_norm,
         l1_lambda_q1, l1_lambda_k1, l1_lambda_q2, l1_lambda_k2, l1_subln),
        (cache_l2_k, cache_l2_v, cache_l2_logf, l2_norm, l2_w_in, l2_w_out, l2_q_norm, l2_k_norm, l2_forget_bias),
        (cache_l3_k, cache_l3_v, cache_l3_kidx, l3_norm, l3_w_in, l3_w_out, l3_q_norm, l3_k_norm, l3_idx_k_norm),
    )
    mixers = (mixer_a, mixer_b, mixer_c)
    xp, xs = x_prompt, x_sample
    states = []
    for i in range(DEPTH):
        xp, xs, st = mixers[i % N_MIXERS](i, xp, xs, *layer_args[i])
        states.append(st)
    l0_pk, l0_pv, l0_pki, l0_sk, l0_sv, l0_ski = states[0]
    l1_pk, l1_pv, l1_sk, l1_sv = states[1]
    l2_pk, l2_pv, l2_pf, l2_sk, l2_sv, l2_sf = states[2]
    l3_pk, l3_pv, l3_pki, l3_sk, l3_sv, l3_ski = states[3]
    return (xp, xs,
            l0_pk, l0_pv, l0_pki, l0_sk, l0_sv, l0_ski,
            l1_pk, l1_pv, l1_sk, l1_sv,
            l2_pk, l2_pv, l2_pf, l2_sk, l2_sv, l2_sf,
            l3_pk, l3_pv, l3_pki, l3_sk, l3_sv, l3_ski)
```

```python
import functools

import numpy as np
import jax
import jax.numpy as jnp
from jax import lax
from jax.experimental import pallas as pl
from jax.experimental.pallas import tpu as pltpu

F32 = jnp.float32
BF16 = jnp.bfloat16
I32 = jnp.int32

D_MODEL = 2048
PAST_LEN = 1024
CHUNK_SHIFT = 6
ROPE_THETA = 10000.0
EPS = 1e-6
HEAD_DIM = 128
A_HEADS = 16
A_KV_HEADS = 4
A_REP = A_HEADS // A_KV_HEADS
IDX_HEADS = 16
IDX_DIM = 64
TOPK_MAX = 256
B_HEADS = 8
B_V_DIM = 256
C_HEADS = 16
MIX_WIDTH = D_MODEL

A_SIZES = (2048, 512, 512, 1024, 64, 16, 2048)
B_SIZES = (2048, 2048, 2048, 2048)
C_SIZES = (2048, 2048, 2048, 16, 2048)

LANES = 128
MXU_N = 256
VMEM_LIMIT = 56 * 1024 * 1024
MASKED = -1e30
INT_MIN = -2 ** 31
NEG_INF_KEY = -2139095041


def _params(sem):
    return pltpu.CompilerParams(dimension_semantics=sem, vmem_limit_bytes=VMEM_LIMIT)


def _norm_rows(x_ref, g_ref):
    x = x_ref[...]
    ms = jnp.mean(x * x, axis=-1, keepdims=True)
    return (x * lax.rsqrt(ms + EPS) * g_ref[...]).astype(BF16)


def _rope128(y, cos, sin):
    return y * cos + pltpu.roll(y, 64, 1) * sin


def _rope64(y, cos, sin):
    lane = lax.broadcasted_iota(I32, y.shape, 1)
    rot = jnp.where((lane & 63) < 32, pltpu.roll(y, 96, 1), pltpu.roll(y, 32, 1))
    return y * cos + rot * sin


def _proj_qk_kernel(x_ref, g_ref, w_ref, cos_ref, sin_ref, hg_ref, *outs,
                    n_heads, scale, use_rope, head_major, emit_f32):
    h = _norm_rows(x_ref, g_ref)
    hg = hg_ref[...]
    of_ref = outs[0] if emit_f32 else None
    ob_ref = outs[-1]
    for c in range(n_heads // 2):
        y = jnp.dot(h, w_ref[:, c * MXU_N:(c + 1) * MXU_N], preferred_element_type=F32)
        for j in range(2):
            hd = 2 * c + j
            yh = y[:, j * LANES:(j + 1) * LANES]
            ms = jnp.mean(yh * yh, axis=-1, keepdims=True)
            yr = yh * lax.rsqrt(ms + EPS) * hg
            if use_rope:
                yr = _rope128(yr, cos_ref[...], sin_ref[...])
            if emit_f32:
                of_ref[:, hd * LANES:(hd + 1) * LANES] = yr
            yb = (yr * scale).astype(BF16)
            if head_major:
                ob_ref[hd] = yb
            else:
                ob_ref[:, hd * LANES:(hd + 1) * LANES] = yb


def _proj_plain_kernel(x_ref, g_ref, w_ref, of_ref, ob_ref, *, n_cols):
    h = _norm_rows(x_ref, g_ref)
    for c in range(n_cols // MXU_N):
        sl = slice(c * MXU_N, (c + 1) * MXU_N)
        y = jnp.dot(h, w_ref[:, sl], preferred_element_type=F32)
        of_ref[:, sl] = y
        ob_ref[:, sl] = y.astype(BF16)


def _proj_gate_kernel(x_ref, g_ref, w_ref, ob_ref, *, n_cols):
    h = _norm_rows(x_ref, g_ref)
    for c in range(n_cols // MXU_N):
        sl = slice(c * MXU_N, (c + 1) * MXU_N)
        y = jnp.dot(h, w_ref[:, sl], preferred_element_type=F32)
        ob_ref[:, sl] = (y * (1.0 / (1.0 + jnp.exp(-y)))).astype(BF16)


def _proj_rope64_kernel(x_ref, g_ref, w_ref, cos_ref, sin_ref, ob_ref, *, n_cols):
    h = _norm_rows(x_ref, g_ref)
    cos = cos_ref[...]
    sin = sin_ref[...]
    for c in range(n_cols // MXU_N):
        y = jnp.dot(h, w_ref[:, c * MXU_N:(c + 1) * MXU_N], preferred_element_type=F32)
        for j in range(2):
            col = c * MXU_N + j * LANES
            ob_ref[:, col:col + LANES] = _rope64(y[:, j * LANES:(j + 1) * LANES], cos, sin).astype(BF16)


def _proj_idx_small_kernel(x_ref, g_ref, w_ref, cos_ref, sin_ref, hg_ref, o_ref):
    h = _norm_rows(x_ref, g_ref)
    y = jnp.dot(h, w_ref[...], preferred_element_type=F32)
    lane = lax.broadcasted_iota(I32, y.shape, 1)
    is_key = lane < IDX_DIM
    ms = jnp.sum(jnp.where(is_key, y * y, 0.0), axis=-1, keepdims=True) * (1.0 / IDX_DIM)
    kn = _rope64(y * lax.rsqrt(ms + EPS) * hg_ref[...], cos_ref[...], sin_ref[...])
    o_ref[...] = jnp.where(is_key, kn, y)


def _proj_logf_kernel(x_ref, g_ref, w_ref, fb_ref, o_ref):
    h = _norm_rows(x_ref, g_ref)
    z = jnp.dot(h, w_ref[...], preferred_element_type=F32) + fb_ref[...]
    o_ref[...] = jnp.minimum(z, 0.0) - jnp.log1p(jnp.exp(-jnp.abs(z)))


def _proj_call(kern, x, g, w, extra, extra_specs, out_shape, out_specs, tm):
    m, d = x.shape
    n = w.shape[1]
    in_specs = [pl.BlockSpec((tm, d), lambda i: (i, 0)),
                pl.BlockSpec((1, d), lambda i: (0, 0)),
                pl.BlockSpec((d, n), lambda i: (0, 0))] + list(extra_specs)
    return pl.pallas_call(kern, grid=(m // tm,), in_specs=in_specs, out_specs=out_specs,
                          out_shape=out_shape, compiler_params=_params(("parallel",)))(
                              x, g.reshape(1, d), w, *extra)


class _Rows:
    def __init__(self, batch, t_q, pos0, tm):
        self.batch, self.t_q, self.pos0, self.tm = batch, t_q, pos0, tm
        self.m = batch * t_q
        self.tab_blocks = max(t_q // tm, 1)

    def tab_spec(self):
        nb = self.tab_blocks
        return pl.BlockSpec((self.tm, LANES), lambda i: (i % nb, 0))

    def tables(self, d):
        period = max(self.t_q, self.tm)
        pos = self.pos0 + (jnp.arange(period) % self.t_q)
        half = d // 2
        inv = ROPE_THETA ** (-2.0 * jnp.arange(half, dtype=F32) / d)
        ang = pos.astype(F32)[:, None] * inv[None, :]
        cos, sin = jnp.cos(ang), jnp.sin(ang)
        reps = LANES // d
        cos_f = jnp.tile(jnp.concatenate([cos, cos], axis=-1), (1, reps))
        sin_f = jnp.tile(jnp.concatenate([-sin, sin], axis=-1), (1, reps))
        return cos_f, sin_f


def _row_spec(tm, n):
    return pl.BlockSpec((tm, n), lambda i: (i, 0))


def _vec_spec(n):
    return pl.BlockSpec((1, n), lambda i: (0, 0))


def _proj_qk(rows, x, g, w, head_gain, *, scale, use_rope, head_major, emit_f32):
    n = w.shape[1]
    n_heads = n // HEAD_DIM
    cos, sin = rows.tables(HEAD_DIM)
    kern = functools.partial(_proj_qk_kernel, n_heads=n_heads, scale=scale, use_rope=use_rope,
                             head_major=head_major, emit_f32=emit_f32)
    out_shape, out_specs = [], []
    if emit_f32:
        out_shape.append(jax.ShapeDtypeStruct((rows.m, n), F32))
        out_specs.append(_row_spec(rows.tm, n))
    if head_major:
        out_shape.append(jax.ShapeDtypeStruct((n_heads, rows.m, HEAD_DIM), BF16))
        out_specs.append(pl.BlockSpec((n_heads, rows.tm, HEAD_DIM), lambda i: (0, i, 0)))
    else:
        out_shape.append(jax.ShapeDtypeStruct((rows.m, n), BF16))
        out_specs.append(_row_spec(rows.tm, n))
    return _proj_call(kern, x, g, w, (cos, sin, head_gain.reshape(1, HEAD_DIM)),
                      (rows.tab_spec(), rows.tab_spec(), _vec_spec(HEAD_DIM)),
                      out_shape, out_specs, rows.tm)


def _proj_plain(rows, x, g, w):
    n = w.shape[1]
    kern = functools.partial(_proj_plain_kernel, n_cols=n)
    return _proj_call(kern, x, g, w, (), (),
                      [jax.ShapeDtypeStruct((rows.m, n), F32), jax.ShapeDtypeStruct((rows.m, n), BF16)],
                      [_row_spec(rows.tm, n), _row_spec(rows.tm, n)], rows.tm)


def _proj_gate(rows, x, g, w):
    n = w.shape[1]
    kern = functools.partial(_proj_gate_kernel, n_cols=n)
    return _proj_call(kern, x, g, w, (), (), jax.ShapeDtypeStruct((rows.m, n), BF16),
                      _row_spec(rows.tm, n), rows.tm)


def _proj_rope64(rows, x, g, w):
    n = w.shape[1]
    cos, sin = rows.tables(IDX_DIM)
    kern = functools.partial(_proj_rope64_kernel, n_cols=n)
    return _proj_call(kern, x, g, w, (cos, sin), (rows.tab_spec(), rows.tab_spec()),
                      jax.ShapeDtypeStruct((rows.m, n), BF16), _row_spec(rows.tm, n), rows.tm)


def _proj_idx_small(rows, x, g, w, key_gain):
    cos, sin = rows.tables(IDX_DIM)
    gain = jnp.concatenate([key_gain, jnp.zeros((LANES - IDX_DIM,), F32)]).reshape(1, LANES)
    return _proj_call(_proj_idx_small_kernel, x, g, w, (cos, sin, gain),
                      (rows.tab_spec(), rows.tab_spec(), _vec_spec(LANES)),
                      jax.ShapeDtypeStruct((rows.m, LANES), F32), _row_spec(rows.tm, LANES), rows.tm)


def _proj_logf(rows, x, g, w, fb):
    fbp = jnp.concatenate([fb, jnp.zeros((LANES - C_HEADS,), F32)]).reshape(1, LANES)
    return _proj_call(_proj_logf_kernel, x, g, w, (fbp,), (_vec_spec(LANES),),
                      jax.ShapeDtypeStruct((rows.m, LANES), F32), _row_spec(rows.tm, LANES), rows.tm)


def _out_kernel(x_ref, o_ref, sg_ref, w_ref, y_ref, *, n_cols):
    a = o_ref[...] * sg_ref[...]
    for c in range(n_cols // MXU_N):
        sl = slice(c * MXU_N, (c + 1) * MXU_N)
        y_ref[:, sl] = x_ref[:, sl] + jnp.dot(a, w_ref[:, sl], preferred_element_type=F32)


def _out_proj(rows, x, o, sg, w):
    m, d = x.shape
    tm = rows.tm
    kern = functools.partial(_out_kernel, n_cols=d)
    return pl.pallas_call(
        kern, grid=(m // tm,),
        in_specs=[_row_spec(tm, d), _row_spec(tm, d), _row_spec(tm, d),
                  pl.BlockSpec((d, d), lambda i: (0, 0))],
        out_specs=_row_spec(tm, d), out_shape=jax.ShapeDtypeStruct((m, d), F32),
        compiler_params=_params(("parallel",)))(x, o, sg, w)


class _Attn:
    def __init__(self, batch, t_q, s_pad, s_valid, q_off, tq, tk):
        self.batch, self.t_q, self.s_pad, self.s_valid = batch, t_q, s_pad, s_valid
        self.q_off, self.tq, self.tk = q_off, tq, tk
        self.nq = t_q // tq


def _chunk_limits(i, cfg):
    q0 = cfg.q_off + i * cfg.tq
    qpos = q0 + lax.broadcasted_iota(I32, (cfg.tq, 1), 0)
    qend = jnp.minimum(((qpos >> CHUNK_SHIFT) + 1) << CHUNK_SHIFT, cfg.s_valid)
    kend = jnp.minimum((((q0 + cfg.tq - 1) >> CHUNK_SHIFT) + 1) << CHUNK_SHIFT, cfg.s_valid)
    return qend, (kend + cfg.tk - 1) // cfg.tk


def _causal_limits(i, cfg):
    q0 = cfg.q_off + i * cfg.tq
    qpos = q0 + lax.broadcasted_iota(I32, (cfg.tq, 1), 0)
    return qpos + 1, (q0 + cfg.tq + cfg.tk - 1) // cfg.tk


def _qk(q, k):
    return lax.dot_general(q, k, (((1,), (1,)), ((), ())), preferred_element_type=F32)


def _online_update(s, v, m_ref, l_ref, acc_ref, idx):
    m_old = m_ref[idx]
    m_new = jnp.maximum(m_old, jnp.max(s, axis=-1, keepdims=True))
    alpha = jnp.exp(m_old - m_new)
    p = jnp.exp(s - m_new)
    l_ref[idx] = alpha * l_ref[idx] + jnp.sum(p, axis=-1, keepdims=True)
    acc_ref[idx] = alpha * acc_ref[idx] + jnp.dot(p.astype(BF16), v, preferred_element_type=F32)
    m_ref[idx] = m_new


def _init_state(m_ref, l_ref, acc_ref):
    m_ref[...] = jnp.full(m_ref.shape, MASKED, F32)
    l_ref[...] = jnp.zeros(l_ref.shape, F32)
    acc_ref[...] = jnp.zeros(acc_ref.shape, F32)


def _dsa_kernel(q_ref, qi_ref, sm_ref, k_ref, v_ref, kit_ref, o_ref,
                key_ref, m_ref, l_ref, acc_ref, *, cfg, topk):
    tq, tk = cfg.tq, cfg.tk
    i = pl.program_id(1)
    qend, n_kv = _chunk_limits(i, cfg)
    w = sm_ref[:, IDX_DIM:IDX_DIM + IDX_HEADS] * (IDX_DIM ** -0.5 * IDX_HEADS ** -0.5)

    def score_block(j, carry):
        ks = pl.multiple_of(j * tk, tk)
        acc = jnp.zeros((tq, tk), F32)
        for pair in range(IDX_HEADS // 2):
            qp = qi_ref[:, pair * LANES:(pair + 1) * LANES]
            for e in range(2):
                hd = 2 * pair + e
                sc = jnp.dot(qp, kit_ref[e, :, pl.ds(ks, tk)], preferred_element_type=F32)
                acc = acc + w[:, hd:hd + 1] * jnp.maximum(sc, 0.0)
        bits = lax.bitcast_convert_type(acc, I32)
        key = bits ^ ((bits >> 31) & 0x7FFFFFFF)
        key = jnp.where(acc == 0.0, 0, key)
        kpos = ks + lax.broadcasted_iota(I32, (tq, tk), 1)
        key_ref[:, pl.ds(ks, tk)] = jnp.where(kpos < qend, key, NEG_INF_KEY)
        return carry

    lax.fori_loop(0, n_kv, score_block, 0)

    def count_ge(cand):
        def body(j, part):
            ks = pl.multiple_of(j * tk, tk)
            ge = jnp.where(key_ref[:, pl.ds(ks, tk)] >= cand, 1, 0)
            for c in range(tk // LANES):
                part = part + ge[:, c * LANES:(c + 1) * LANES]
            return part
        part = lax.fori_loop(0, n_kv, body, jnp.zeros((tq, LANES), I32))
        return jnp.sum(part, axis=-1, keepdims=True)

    zero = jnp.zeros((tq, 1), I32)
    lo = jnp.where(count_ge(zero) >= topk, zero, INT_MIN)

    def bit_step(t, lo):
        cand = lo + jnp.left_shift(1, 30 - t)
        return jnp.where(count_ge(cand) >= topk, cand, lo)

    thr = lax.fori_loop(0, 31, bit_step, lo)
    n_gt = count_ge(thr + 1)
    n_ge = count_ge(thr)
    need = topk - n_gt
    tie_rows = (n_ge - n_gt > need) & (thr > NEG_INF_KEY)

    @pl.when(jnp.max(jnp.where(tie_rows, 1, 0)) > 0)
    def _drop_late_ties():
        need_f = need.astype(F32)
        upper = jnp.where(lax.broadcasted_iota(I32, (tk, tk), 0) <= lax.broadcasted_iota(I32, (tk, tk), 1),
                          1.0, 0.0).astype(BF16)

        def body(j, seen):
            ks = pl.multiple_of(j * tk, tk)
            kb = key_ref[:, pl.ds(ks, tk)]
            eq = kb == thr
            eq_f = jnp.where(eq, 1.0, 0.0)
            rank = seen + jnp.dot(eq_f.astype(BF16), upper, preferred_element_type=F32)
            drop = eq & (rank > need_f) & tie_rows
            key_ref[:, pl.ds(ks, tk)] = jnp.where(drop, thr - 1, kb)
            return seen + jnp.sum(eq_f, axis=-1, keepdims=True)

        lax.fori_loop(0, n_kv, body, jnp.zeros((tq, 1), F32))

    thr_keep = jnp.maximum(thr, NEG_INF_KEY + 1)
    _init_state(m_ref, l_ref, acc_ref)

    def attn_block(j, carry):
        ks = pl.multiple_of(j * tk, tk)
        bias = jnp.where(key_ref[:, pl.ds(ks, tk)] >= thr_keep, 0.0, MASKED)
        for g in range(A_KV_HEADS):
            kg = k_ref[pl.ds(ks, tk), g * HEAD_DIM:(g + 1) * HEAD_DIM]
            vg = v_ref[pl.ds(ks, tk), g * HEAD_DIM:(g + 1) * HEAD_DIM]
            for r in range(A_REP):
                hd = g * A_REP + r
                s = _qk(q_ref[hd], kg) + bias
                _online_update(s, vg, m_ref, l_ref, acc_ref, hd)
        return carry

    lax.fori_loop(0, n_kv, attn_block, 0)
    for hd in range(A_HEADS):
        o_ref[:, hd * HEAD_DIM:(hd + 1) * HEAD_DIM] = (acc_ref[hd] / l_ref[hd]).astype(BF16)


def _dsa_attention(cfg, q_hm, q_row0, qi, sm, k, v, kit, topk):
    tq, nq, s = cfg.tq, cfg.nq, cfg.s_pad
    qb0 = q_row0 // tq
    kern = functools.partial(_dsa_kernel, cfg=cfg, topk=topk)
    return pl.pallas_call(
        kern, grid=(cfg.batch, nq),
        in_specs=[pl.BlockSpec((A_HEADS, tq, HEAD_DIM), lambda b, i: (0, qb0 + b * nq + i, 0)),
                  pl.BlockSpec((tq, IDX_HEADS * IDX_DIM), lambda b, i: (qb0 + b * nq + i, 0)),
                  pl.BlockSpec((tq, LANES), lambda b, i: (qb0 + b * nq + i, 0)),
                  pl.BlockSpec((s, A_KV_HEADS * HEAD_DIM), lambda b, i: (b, 0)),
                  pl.BlockSpec((s, A_KV_HEADS * HEAD_DIM), lambda b, i: (b, 0)),
                  pl.BlockSpec((None, 2, LANES, s), lambda b, i: (b, 0, 0, 0))],
        out_specs=pl.BlockSpec((tq, MIX_WIDTH), lambda b, i: (b * nq + i, 0)),
        out_shape=jax.ShapeDtypeStruct((cfg.batch * cfg.t_q, MIX_WIDTH), BF16),
        scratch_shapes=[pltpu.VMEM((tq, s), I32),
                        pltpu.VMEM((A_HEADS, tq, 1), F32),
                        pltpu.VMEM((A_HEADS, tq, 1), F32),
                        pltpu.VMEM((A_HEADS, tq, HEAD_DIM), F32)],
        compiler_params=_params(("parallel", "arbitrary")))(q_hm, qi, sm, k, v, kit)


def _diff_kernel(q_ref, k_ref, v_ref, lq1_ref, lk1_ref, lq2_ref, lk2_ref, sub_ref, o_ref,
                 m_ref, l_ref, acc_ref, *, cfg, lam_init):
    tq, tk = cfg.tq, cfg.tk
    qend, n_kv = _chunk_limits(pl.program_id(2), cfg)
    _init_state(m_ref, l_ref, acc_ref)

    def block(j, carry):
        ks = pl.multiple_of(j * tk, tk)
        visible = ks + lax.broadcasted_iota(I32, (tq, tk), 1) < qend
        vb = v_ref[pl.ds(ks, tk), :]
        for c in range(2):
            sl = slice(c * HEAD_DIM, (c + 1) * HEAD_DIM)
            s = jnp.where(visible, _qk(q_ref[:, sl], k_ref[pl.ds(ks, tk), sl]), MASKED)
            _online_update(s, vb, m_ref, l_ref, acc_ref, c)
        return carry

    lax.fori_loop(0, n_kv, block, 0)
    lam = (jnp.exp(jnp.sum(lq1_ref[...] * lk1_ref[...], axis=-1, keepdims=True))
           - jnp.exp(jnp.sum(lq2_ref[...] * lk2_ref[...], axis=-1, keepdims=True)) + lam_init)
    o = acc_ref[0] / l_ref[0] - lam * (acc_ref[1] / l_ref[1])
    ms = jnp.mean(o * o, axis=-1, keepdims=True)
    o_ref[...] = (o * lax.rsqrt(ms + EPS) * sub_ref[...] * (1.0 - lam_init)).astype(BF16)


def _diff_attention(cfg, q, q_row0, k, v, lam_vecs, subln, lam_init):
    tq, nq, s = cfg.tq, cfg.nq, cfg.s_pad
    qb0 = q_row0 // tq
    kern = functools.partial(_diff_kernel, cfg=cfg, lam_init=lam_init)
    vec = pl.BlockSpec((1, HEAD_DIM), lambda b, h, i: (0, 0))
    return pl.pallas_call(
        kern, grid=(cfg.batch, B_HEADS, nq),
        in_specs=[pl.BlockSpec((tq, 2 * HEAD_DIM), lambda b, h, i: (qb0 + b * nq + i, h)),
                  pl.BlockSpec((s, 2 * HEAD_DIM), lambda b, h, i: (b, h)),
                  pl.BlockSpec((s, B_V_DIM), lambda b, h, i: (b, h)),
                  vec, vec, vec, vec,
                  pl.BlockSpec((1, B_V_DIM), lambda b, h, i: (0, 0))],
        out_specs=pl.BlockSpec((tq, B_V_DIM), lambda b, h, i: (b * nq + i, h)),
        out_shape=jax.ShapeDtypeStruct((cfg.batch * cfg.t_q, MIX_WIDTH), BF16),
        scratch_shapes=[pltpu.VMEM((2, tq, 1), F32),
                        pltpu.VMEM((2, tq, 1), F32),
                        pltpu.VMEM((2, tq, B_V_DIM), F32)],
        compiler_params=_params(("parallel", "parallel", "arbitrary")))(
            q, k, v, *[a.reshape(1, HEAD_DIM) for a in lam_vecs], subln.reshape(1, B_V_DIM))


def _fox_kernel(q_ref, k_ref, v_ref, cq_ref, ck_ref, o_ref, m_ref, l_ref, acc_ref, *, cfg):
    tq, tk = cfg.tq, cfg.tk
    hd = pl.program_id(1)
    qend, n_kv = _causal_limits(pl.program_id(2), cfg)
    _init_state(m_ref, l_ref, acc_ref)
    head_lane = lax.broadcasted_iota(I32, (tq, LANES), 1) == hd
    cq = jnp.sum(jnp.where(head_lane, cq_ref[...], 0.0), axis=-1, keepdims=True)

    def block(j, carry):
        ks = pl.multiple_of(j * tk, tk)
        visible = ks + lax.broadcasted_iota(I32, (tq, tk), 1) < qend
        s = _qk(q_ref[...], k_ref[pl.ds(ks, tk), :]) + (cq - ck_ref[:, pl.ds(ks, tk)])
        _online_update(jnp.where(visible, s, MASKED), v_ref[pl.ds(ks, tk), :], m_ref, l_ref, acc_ref, 0)
        return carry

    lax.fori_loop(0, n_kv, block, 0)
    o_ref[...] = (acc_ref[0] / l_ref[0]).astype(BF16)


def _fox_attention(cfg, q, q_row0, k, v, cq, ck_rows):
    tq, nq, s = cfg.tq, cfg.nq, cfg.s_pad
    qb0 = q_row0 // tq
    kern = functools.partial(_fox_kernel, cfg=cfg)
    return pl.pallas_call(
        kern, grid=(cfg.batch, C_HEADS, nq),
        in_specs=[pl.BlockSpec((tq, HEAD_DIM), lambda b, h, i: (qb0 + b * nq + i, h)),
                  pl.BlockSpec((s, HEAD_DIM), lambda b, h, i: (b, h)),
                  pl.BlockSpec((s, HEAD_DIM), lambda b, h, i: (b, h)),
                  pl.BlockSpec((tq, LANES), lambda b, h, i: (qb0 + b * nq + i, 0)),
                  pl.BlockSpec((None, 1, s), lambda b, h, i: (b * C_HEADS + h, 0, 0))],
        out_specs=pl.BlockSpec((tq, HEAD_DIM), lambda b, h, i: (b * nq + i, h)),
        out_shape=jax.ShapeDtypeStruct((cfg.batch * cfg.t_q, MIX_WIDTH), BF16),
        scratch_shapes=[pltpu.VMEM((1, tq, 1), F32),
                        pltpu.VMEM((1, tq, 1), F32),
                        pltpu.VMEM((1, tq, HEAD_DIM), F32)],
        compiler_params=_params(("parallel", "parallel", "arbitrary")))(q, k, v, cq, ck_rows)


def _cumsum_kernel(x_ref, o_ref, carry_ref, *, tb):
    @pl.when(pl.program_id(1) == 0)
    def _():
        carry_ref[...] = jnp.zeros(carry_ref.shape, F32)

    x = x_ref[...]
    hi = x.astype(BF16)
    r1 = x - hi.astype(F32)
    mid = r1.astype(BF16)
    low = (r1 - mid.astype(F32)).astype(BF16)
    tri = jnp.where(lax.broadcasted_iota(I32, (tb, tb), 0) >= lax.broadcasted_iota(I32, (tb, tb), 1),
                    1.0, 0.0).astype(BF16)
    c = (jnp.dot(tri, hi, preferred_element_type=F32) + jnp.dot(tri, mid, preferred_element_type=F32)
         + jnp.dot(tri, low, preferred_element_type=F32)) + carry_ref[...]
    o_ref[...] = c
    carry_ref[...] = c[tb - 1:tb, :]


def _cumsum_rows(x, batch, t, tb):
    nt = t // tb
    return pl.pallas_call(
        functools.partial(_cumsum_kernel, tb=tb), grid=(batch, nt),
        in_specs=[pl.BlockSpec((tb, LANES), lambda b, j: (b * nt + j, 0))],
        out_specs=pl.BlockSpec((tb, LANES), lambda b, j: (b * nt + j, 0)),
        out_shape=jax.ShapeDtypeStruct((batch * t, LANES), F32),
        scratch_shapes=[pltpu.VMEM((1, LANES), F32)],
        compiler_params=_params(("parallel", "arbitrary")))(x)


def _split_w(w_in, sizes):
    offs = np.cumsum((0,) + tuple(sizes))
    return [w_in[:, int(offs[i]):int(offs[i + 1])] for i in range(len(sizes))]


def _pad_rows(a, rows):
    return jnp.pad(a, ((0, 0), (0, rows - a.shape[1])) + ((0, 0),) * (a.ndim - 2))


def _sample_keys(cache, new, s_pad):
    b = cache.shape[0]
    full = jnp.concatenate([cache.reshape(b, cache.shape[1], -1), new.reshape(b, new.shape[1], -1)], axis=1)
    return _pad_rows(full, s_pad).astype(BF16).reshape(b * s_pad, -1)


def _key_transposed_pairs(ki, batch, s):
    kt = jnp.swapaxes(ki.reshape(batch, s, IDX_DIM), 1, 2).astype(BF16)
    z = jnp.zeros_like(kt)
    return jnp.stack([jnp.concatenate([kt, z], axis=1), jnp.concatenate([z, kt], axis=1)], axis=1)


def _layer_a(pr, sr, cfg_p, cfg_s, xp, xs, ck, cv, cki, norm, w_in, w_out, qn, kn, ikn):
    wq, wk, wv, wqi, wki, wwi, wg = _split_w(w_in, A_SIZES)
    wsm = jnp.concatenate([wki, wwi, jnp.zeros((D_MODEL, LANES - IDX_DIM - IDX_HEADS), F32)], axis=1)
    wq, wk, wv, wqi, wg, wsm, wo = [a.astype(BF16) for a in (wq, wk, wv, wqi, wg, wsm, w_out)]

    def project(rows, x):
        q = _proj_qk(rows, x, norm, wq, qn, scale=HEAD_DIM ** -0.5, use_rope=True,
                     head_major=True, emit_f32=False)[0]
        k_f, k_b = _proj_qk(rows, x, norm, wk, kn, scale=1.0, use_rope=True, head_major=False, emit_f32=True)
        v_f, v_b = _proj_plain(rows, x, norm, wv)
        qi = _proj_rope64(rows, x, norm, wqi)
        sm = _proj_idx_small(rows, x, norm, wsm, ikn)
        sg = _proj_gate(rows, x, norm, wg)
        return q, k_f, k_b, v_f, v_b, qi, sm, sg

    q, k_f, k_b, v_f, v_b, qi, sm, sg = project(pr, xp)
    ki_f = sm[:, :IDX_DIM]
    o = _dsa_attention(cfg_p, q, 0, qi, sm, k_b, v_b,
                       _key_transposed_pairs(ki_f, pr.batch, pr.t_q), min(TOPK_MAX, pr.t_q // 4))
    yp = _out_proj(pr, xp, o, sg, wo)

    q2, k2_f, _, v2_f, _, qi2, sm2, sg2 = project(sr, xs)
    ki2_f = sm2[:, :IDX_DIM]
    b, s = sr.batch, sr.t_q
    k_all = _sample_keys(ck, k2_f.reshape(b, s, -1), cfg_s.s_pad)
    v_all = _sample_keys(cv, v2_f.reshape(b, s, -1), cfg_s.s_pad)
    ki_all = _pad_rows(jnp.concatenate([cki, ki2_f.reshape(b, s, IDX_DIM)], axis=1), cfg_s.s_pad)
    o2 = _dsa_attention(cfg_s, q2, 0, qi2, sm2, k_all, v_all,
                        _key_transposed_pairs(ki_all.reshape(-1, IDX_DIM), b, cfg_s.s_pad),
                        min(TOPK_MAX, cfg_s.s_valid // 4))
    ys = _out_proj(sr, xs, o2, sg2, wo)
    kv = (A_KV_HEADS, HEAD_DIM)
    state = (k_f.reshape(pr.batch, pr.t_q, *kv), v_f.reshape(pr.batch, pr.t_q, *kv),
             ki_f.reshape(pr.batch, pr.t_q, IDX_DIM),
             k2_f.reshape(b, s, *kv), v2_f.reshape(b, s, *kv), ki2_f.reshape(b, s, IDX_DIM))
    return yp, ys, state


def _layer_b(layer, pr, sr, cfg_p, cfg_s, xp, xs, ck, cv, norm, w_in, w_out, qn, kn,
             lq1, lk1, lq2, lk2, subln):
    lam_init = 0.8 - 0.6 * float(np.exp(-0.3 * layer))
    wq, wk, wv, wg, wo = [a.astype(BF16) for a in _split_w(w_in, B_SIZES) + [w_out]]

    def project(rows, x):
        q = _proj_qk(rows, x, norm, wq, qn, scale=HEAD_DIM ** -0.5, use_rope=True,
                     head_major=False, emit_f32=False)[0]
        k_f, k_b = _proj_qk(rows, x, norm, wk, kn, scale=1.0, use_rope=True, head_major=False, emit_f32=True)
        v_f, v_b = _proj_plain(rows, x, norm, wv)
        return q, k_f, k_b, v_f, v_b, _proj_gate(rows, x, norm, wg)

    lam_vecs = (lq1, lk1, lq2, lk2)
    q, k_f, k_b, v_f, v_b, sg = project(pr, xp)
    o = _diff_attention(cfg_p, q, 0, k_b, v_b, lam_vecs, subln, lam_init)
    yp = _out_proj(pr, xp, o, sg, wo)

    q2, k2_f, _, v2_f, _, sg2 = project(sr, xs)
    b, s = sr.batch, sr.t_q
    k_all = _sample_keys(ck, k2_f.reshape(b, s, -1), cfg_s.s_pad)
    v_all = _sample_keys(cv, v2_f.reshape(b, s, -1), cfg_s.s_pad)
    o2 = _diff_attention(cfg_s, q2, 0, k_all, v_all, lam_vecs, subln, lam_init)
    ys = _out_proj(sr, xs, o2, sg2, wo)
    state = (k_f.reshape(pr.batch, pr.t_q, 2 * B_HEADS, HEAD_DIM), v_f.reshape(pr.batch, pr.t_q, B_HEADS, B_V_DIM),
             k2_f.reshape(b, s, 2 * B_HEADS, HEAD_DIM), v2_f.reshape(b, s, B_HEADS, B_V_DIM))
    return yp, ys, state


def _head_rows(c, batch, s):
    return jnp.swapaxes(c.reshape(batch, s, LANES)[:, :, :C_HEADS], 1, 2).reshape(batch * C_HEADS, 1, s)


def _layer_c(pr, sr, cfg_p, cfg_s, xp, xs, ck, cv, clogf, norm, w_in, w_out, qn, kn, fb):
    wq, wk, wv, wf, wg = _split_w(w_in, C_SIZES)
    wf = jnp.concatenate([wf, jnp.zeros((D_MODEL, LANES - C_HEADS), F32)], axis=1)
    wq, wk, wv, wf, wg, wo = [a.astype(BF16) for a in (wq, wk, wv, wf, wg, w_out)]

    def project(rows, x):
        q = _proj_qk(rows, x, norm, wq, qn, scale=HEAD_DIM ** -0.5, use_rope=False,
                     head_major=False, emit_f32=False)[0]
        k_f, k_b = _proj_qk(rows, x, norm, wk, kn, scale=1.0, use_rope=False, head_major=False, emit_f32=True)
        v_f, v_b = _proj_plain(rows, x, norm, wv)
        return q, k_f, k_b, v_f, v_b, _proj_logf(rows, x, norm, wf, fb), _proj_gate(rows, x, norm, wg)

    q, k_f, k_b, v_f, v_b, logf, sg = project(pr, xp)
    c = _cumsum_rows(logf, pr.batch, pr.t_q, 512)
    o = _fox_attention(cfg_p, q, 0, k_b, v_b, c, _head_rows(c, pr.batch, pr.t_q))
    yp = _out_proj(pr, xp, o, sg, wo)

    q2, k2_f, _, v2_f, _, logf2, sg2 = project(sr, xs)
    b, s = sr.batch, sr.t_q
    k_all = _sample_keys(ck, k2_f.reshape(b, s, -1), cfg_s.s_pad)
    v_all = _sample_keys(cv, v2_f.reshape(b, s, -1), cfg_s.s_pad)
    logf_all = jnp.concatenate([jnp.pad(clogf.astype(F32), ((0, 0), (0, 0), (0, LANES - C_HEADS))),
                                logf2.reshape(b, s, LANES)], axis=1)
    c2 = _cumsum_rows(_pad_rows(logf_all, cfg_s.s_pad).reshape(b * cfg_s.s_pad, LANES), b, cfg_s.s_pad, LANES)
    cq2 = c2.reshape(b, cfg_s.s_pad, LANES)[:, PAST_LEN:PAST_LEN + s].reshape(b * s, LANES)
    o2 = _fox_attention(cfg_s, q2, 0, k_all, v_all, cq2, _head_rows(c2, b, cfg_s.s_pad))
    ys = _out_proj(sr, xs, o2, sg2, wo)
    hd = (C_HEADS, HEAD_DIM)
    state = (k_f.reshape(pr.batch, pr.t_q, *hd), v_f.reshape(pr.batch, pr.t_q, *hd),
             logf[:, :C_HEADS].reshape(pr.batch, pr.t_q, C_HEADS),
             k2_f.reshape(b, s, *hd), v2_f.reshape(b, s, *hd), logf2[:, :C_HEADS].reshape(b, s, C_HEADS))
    return yp, ys, state


def kernel(x_prompt, x_sample, cache_l0_k, cache_l0_v, cache_l0_kidx, cache_l1_k, cache_l1_v, cache_l2_k, cache_l2_v, cache_l2_logf, cache_l3_k, cache_l3_v, cache_l3_kidx, l0_norm, l0_w_in, l0_w_out, l0_q_norm, l0_k_norm, l0_idx_k_norm, l1_norm, l1_w_in, l1_w_out, l1_q_norm, l1_k_norm, l1_lambda_q1, l1_lambda_k1, l1_lambda_q2, l1_lambda_k2, l1_subln, l2_norm, l2_w_in, l2_w_out, l2_q_norm, l2_k_norm, l2_forget_bias, l3_norm, l3_w_in, l3_w_out, l3_q_norm, l3_k_norm, l3_idx_k_norm):
    bp, tp, d = x_prompt.shape
    bs, ts, _ = x_sample.shape
    s_valid = PAST_LEN + ts
    s_pad = -(-s_valid // LANES) * LANES
    pr = _Rows(bp, tp, 0, 512)
    sr = _Rows(bs, ts, PAST_LEN, bs * ts)
    a_p = _Attn(bp, tp, tp, tp, 0, 128, 512)
    a_s = _Attn(bs, ts, s_pad, s_valid, PAST_LEN, ts, LANES)
    bc_p = _Attn(bp, tp, tp, tp, 0, 256, 512)
    bc_s = a_s

    xp = x_prompt.reshape(bp * tp, d)
    xs = x_sample.reshape(bs * ts, d)
    xp, xs, st0 = _layer_a(pr, sr, a_p, a_s, xp, xs, cache_l0_k, cache_l0_v, cache_l0_kidx,
                           l0_norm, l0_w_in, l0_w_out, l0_q_norm, l0_k_norm, l0_idx_k_norm)
    xp, xs, st1 = _layer_b(1, pr, sr, bc_p, bc_s, xp, xs, cache_l1_k, cache_l1_v,
                           l1_norm, l1_w_in, l1_w_out, l1_q_norm, l1_k_norm,
                           l1_lambda_q1, l1_lambda_k1, l1_lambda_q2, l1_lambda_k2, l1_subln)
    xp, xs, st2 = _layer_c(pr, sr, bc_p, bc_s, xp, xs, cache_l2_k, cache_l2_v, cache_l2_logf,
                           l2_norm, l2_w_in, l2_w_out, l2_q_norm, l2_k_norm, l2_forget_bias)
    xp, xs, st3 = _layer_a(pr, sr, a_p, a_s, xp, xs, cache_l3_k, cache_l3_v, cache_l3_kidx,
                           l3_norm, l3_w_in, l3_w_out, l3_q_norm, l3_k_norm, l3_idx_k_norm)
    return (xp.reshape(bp, tp, d), xs.reshape(bs, ts, d)) + st0 + st1 + st2 + st3
```

```python
import functools

import numpy as np
import jax
import jax.numpy as jnp
from jax import lax
from jax.experimental import pallas as pl
from jax.experimental.pallas import tpu as pltpu

F32 = jnp.float32
BF16 = jnp.bfloat16
I32 = jnp.int32

D_MODEL = 2048
PAST_LEN = 1024
CHUNK_SHIFT = 6
ROPE_THETA = 10000.0
EPS = 1e-6
HEAD_DIM = 128
A_HEADS = 16
A_KV_HEADS = 4
A_REP = A_HEADS // A_KV_HEADS
IDX_HEADS = 16
IDX_DIM = 64
TOPK_MAX = 256
B_HEADS = 8
B_V_DIM = 256
C_HEADS = 16
MIX_WIDTH = D_MODEL

A_SIZES = (2048, 512, 512, 1024, 64, 16, 2048)
B_SIZES = (2048, 2048, 2048, 2048)
C_SIZES = (2048, 2048, 2048, 16, 2048)

LANES = 128
MXU_N = 256
VMEM_LIMIT = 56 * 1024 * 1024
MASKED = -1e30
LOG2E = 1.4426950408889634
STRIP_ROWS = 32
BISECT_UNROLL = 4
INT_MIN = -2 ** 31
NEG_INF_KEY = -2139095041


def _params(sem):
    return pltpu.CompilerParams(dimension_semantics=sem, vmem_limit_bytes=VMEM_LIMIT)


def _norm_rows(x_ref, g_ref):
    x = x_ref[...]
    ms = jnp.mean(x * x, axis=-1, keepdims=True)
    return (x * lax.rsqrt(ms + EPS) * g_ref[...]).astype(BF16)


def _rope128(y, cos, sin):
    return y * cos + pltpu.roll(y, 64, 1) * sin


def _rope64(y, cos, sin):
    lane = lax.broadcasted_iota(I32, y.shape, 1)
    rot = jnp.where((lane & 63) < 32, pltpu.roll(y, 96, 1), pltpu.roll(y, 32, 1))
    return y * cos + rot * sin


def _proj_qk_kernel(x_ref, g_ref, w_ref, cos_ref, sin_ref, hg_ref, *outs,
                    n_heads, scale, use_rope, head_major, emit_f32):
    h = _norm_rows(x_ref, g_ref)
    hg = hg_ref[...]
    of_ref = outs[0] if emit_f32 else None
    ob_ref = outs[-1]
    for c in range(n_heads // 2):
        y = jnp.dot(h, w_ref[:, c * MXU_N:(c + 1) * MXU_N], preferred_element_type=F32)
        for j in range(2):
            hd = 2 * c + j
            yh = y[:, j * LANES:(j + 1) * LANES]
            ms = jnp.mean(yh * yh, axis=-1, keepdims=True)
            yr = yh * lax.rsqrt(ms + EPS) * hg
            if use_rope:
                yr = _rope128(yr, cos_ref[...], sin_ref[...])
            if emit_f32:
                of_ref[:, hd * LANES:(hd + 1) * LANES] = yr
            yb = (yr * scale).astype(BF16)
            if head_major:
                ob_ref[hd] = yb
            else:
                ob_ref[:, hd * LANES:(hd + 1) * LANES] = yb


def _proj_plain_kernel(x_ref, g_ref, w_ref, of_ref, ob_ref, *, n_cols):
    h = _norm_rows(x_ref, g_ref)
    for c in range(n_cols // MXU_N):
        sl = slice(c * MXU_N, (c + 1) * MXU_N)
        y = jnp.dot(h, w_ref[:, sl], preferred_element_type=F32)
        of_ref[:, sl] = y
        ob_ref[:, sl] = y.astype(BF16)


def _proj_gate_kernel(x_ref, g_ref, w_ref, ob_ref, *, n_cols):
    h = _norm_rows(x_ref, g_ref)
    for c in range(n_cols // MXU_N):
        sl = slice(c * MXU_N, (c + 1) * MXU_N)
        y = jnp.dot(h, w_ref[:, sl], preferred_element_type=F32)
        ob_ref[:, sl] = (y * (1.0 / (1.0 + jnp.exp(-y)))).astype(BF16)


def _proj_rope64_kernel(x_ref, g_ref, w_ref, cos_ref, sin_ref, ob_ref, *, n_cols):
    h = _norm_rows(x_ref, g_ref)
    cos = cos_ref[...]
    sin = sin_ref[...]
    for c in range(n_cols // MXU_N):
        y = jnp.dot(h, w_ref[:, c * MXU_N:(c + 1) * MXU_N], preferred_element_type=F32)
        for j in range(2):
            col = c * MXU_N + j * LANES
            ob_ref[:, col:col + LANES] = _rope64(y[:, j * LANES:(j + 1) * LANES], cos, sin).astype(BF16)


def _proj_idx_small_kernel(x_ref, g_ref, w_ref, cos_ref, sin_ref, hg_ref, o_ref):
    h = _norm_rows(x_ref, g_ref)
    y = jnp.dot(h, w_ref[...], preferred_element_type=F32)
    lane = lax.broadcasted_iota(I32, y.shape, 1)
    is_key = lane < IDX_DIM
    ms = jnp.sum(jnp.where(is_key, y * y, 0.0), axis=-1, keepdims=True) * (1.0 / IDX_DIM)
    kn = _rope64(y * lax.rsqrt(ms + EPS) * hg_ref[...], cos_ref[...], sin_ref[...])
    o_ref[...] = jnp.where(is_key, kn, y)


def _proj_logf_kernel(x_ref, g_ref, w_ref, fb_ref, o_ref):
    h = _norm_rows(x_ref, g_ref)
    z = jnp.dot(h, w_ref[...], preferred_element_type=F32) + fb_ref[...]
    o_ref[...] = jnp.minimum(z, 0.0) - jnp.log1p(jnp.exp(-jnp.abs(z)))


def _proj_call(kern, x, g, w, extra, extra_specs, out_shape, out_specs, tm):
    m, d = x.shape
    n = w.shape[1]
    in_specs = [pl.BlockSpec((tm, d), lambda i: (i, 0)),
                pl.BlockSpec((1, d), lambda i: (0, 0)),
                pl.BlockSpec((d, n), lambda i: (0, 0))] + list(extra_specs)
    name = getattr(kern, "func", kern).__name__.strip("_")
    return pl.pallas_call(kern, grid=(m // tm,), in_specs=in_specs, out_specs=out_specs, name=name,
                          out_shape=out_shape, compiler_params=_params(("parallel",)))(
                              x, g.reshape(1, d), w, *extra)


class _Rows:
    def __init__(self, batch, t_q, pos0, tm):
        self.batch, self.t_q, self.pos0, self.tm = batch, t_q, pos0, tm
        self.m = batch * t_q
        self.tab_blocks = max(t_q // tm, 1)

    def tab_spec(self):
        nb = self.tab_blocks
        return pl.BlockSpec((self.tm, LANES), lambda i: (i % nb, 0))

    def tables(self, d):
        period = max(self.t_q, self.tm)
        pos = self.pos0 + (jnp.arange(period) % self.t_q)
        half = d // 2
        inv = ROPE_THETA ** (-2.0 * jnp.arange(half, dtype=F32) / d)
        ang = pos.astype(F32)[:, None] * inv[None, :]
        cos, sin = jnp.cos(ang), jnp.sin(ang)
        reps = LANES // d
        cos_f = jnp.tile(jnp.concatenate([cos, cos], axis=-1), (1, reps))
        sin_f = jnp.tile(jnp.concatenate([-sin, sin], axis=-1), (1, reps))
        return cos_f, sin_f


def _row_spec(tm, n):
    return pl.BlockSpec((tm, n), lambda i: (i, 0))


def _vec_spec(n):
    return pl.BlockSpec((1, n), lambda i: (0, 0))


def _proj_qk(rows, x, g, w, head_gain, *, scale, use_rope, head_major, emit_f32):
    n = w.shape[1]
    n_heads = n // HEAD_DIM
    cos, sin = rows.tables(HEAD_DIM)
    kern = functools.partial(_proj_qk_kernel, n_heads=n_heads, scale=scale, use_rope=use_rope,
                             head_major=head_major, emit_f32=emit_f32)
    out_shape, out_specs = [], []
    if emit_f32:
        out_shape.append(jax.ShapeDtypeStruct((rows.m, n), F32))
        out_specs.append(_row_spec(rows.tm, n))
    if head_major:
        out_shape.append(jax.ShapeDtypeStruct((n_heads, rows.m, HEAD_DIM), BF16))
        out_specs.append(pl.BlockSpec((n_heads, rows.tm, HEAD_DIM), lambda i: (0, i, 0)))
    else:
        out_shape.append(jax.ShapeDtypeStruct((rows.m, n), BF16))
        out_specs.append(_row_spec(rows.tm, n))
    return _proj_call(kern, x, g, w, (cos, sin, head_gain.reshape(1, HEAD_DIM)),
                      (rows.tab_spec(), rows.tab_spec(), _vec_spec(HEAD_DIM)),
                      out_shape, out_specs, rows.tm)


def _proj_plain(rows, x, g, w):
    n = w.shape[1]
    kern = functools.partial(_proj_plain_kernel, n_cols=n)
    return _proj_call(kern, x, g, w, (), (),
                      [jax.ShapeDtypeStruct((rows.m, n), F32), jax.ShapeDtypeStruct((rows.m, n), BF16)],
                      [_row_spec(rows.tm, n), _row_spec(rows.tm, n)], rows.tm)


def _proj_gate(rows, x, g, w):
    n = w.shape[1]
    kern = functools.partial(_proj_gate_kernel, n_cols=n)
    return _proj_call(kern, x, g, w, (), (), jax.ShapeDtypeStruct((rows.m, n), BF16),
                      _row_spec(rows.tm, n), rows.tm)


def _proj_rope64(rows, x, g, w):
    n = w.shape[1]
    cos, sin = rows.tables(IDX_DIM)
    kern = functools.partial(_proj_rope64_kernel, n_cols=n)
    return _proj_call(kern, x, g, w, (cos, sin), (rows.tab_spec(), rows.tab_spec()),
                      jax.ShapeDtypeStruct((rows.m, n), BF16), _row_spec(rows.tm, n), rows.tm)


def _proj_idx_small(rows, x, g, w, key_gain):
    cos, sin = rows.tables(IDX_DIM)
    gain = jnp.concatenate([key_gain, jnp.zeros((LANES - IDX_DIM,), F32)]).reshape(1, LANES)
    return _proj_call(_proj_idx_small_kernel, x, g, w, (cos, sin, gain),
                      (rows.tab_spec(), rows.tab_spec(), _vec_spec(LANES)),
                      jax.ShapeDtypeStruct((rows.m, LANES), F32), _row_spec(rows.tm, LANES), rows.tm)


def _proj_logf(rows, x, g, w, fb):
    fbp = jnp.concatenate([fb, jnp.zeros((LANES - C_HEADS,), F32)]).reshape(1, LANES)
    return _proj_call(_proj_logf_kernel, x, g, w, (fbp,), (_vec_spec(LANES),),
                      jax.ShapeDtypeStruct((rows.m, LANES), F32), _row_spec(rows.tm, LANES), rows.tm)


def _out_kernel(x_ref, o_ref, sg_ref, w_ref, y_ref, *, n_cols):
    a = o_ref[...] * sg_ref[...]
    for c in range(n_cols // MXU_N):
        sl = slice(c * MXU_N, (c + 1) * MXU_N)
        y_ref[:, sl] = x_ref[:, sl] + jnp.dot(a, w_ref[:, sl], preferred_element_type=F32)


def _out_proj(rows, x, o, sg, w):
    m, d = x.shape
    tm = rows.tm
    kern = functools.partial(_out_kernel, n_cols=d)
    return pl.pallas_call(
        kern, grid=(m // tm,), name="out_proj",
        in_specs=[_row_spec(tm, d), _row_spec(tm, d), _row_spec(tm, d),
                  pl.BlockSpec((d, d), lambda i: (0, 0))],
        out_specs=_row_spec(tm, d), out_shape=jax.ShapeDtypeStruct((m, d), F32),
        compiler_params=_params(("parallel",)))(x, o, sg, w)


class _Attn:
    def __init__(self, batch, t_q, s_pad, s_valid, q_off, tq, tk):
        self.batch, self.t_q, self.s_pad, self.s_valid = batch, t_q, s_pad, s_valid
        self.q_off, self.tq, self.tk = q_off, tq, tk
        self.nq = t_q // tq


def _chunk_limits(i, cfg):
    q0 = cfg.q_off + i * cfg.tq
    qpos = q0 + lax.broadcasted_iota(I32, (cfg.tq, 1), 0)
    qend = jnp.minimum(((qpos >> CHUNK_SHIFT) + 1) << CHUNK_SHIFT, cfg.s_valid)
    kmin = jnp.minimum(((q0 >> CHUNK_SHIFT) + 1) << CHUNK_SHIFT, cfg.s_valid)
    kend = jnp.minimum((((q0 + cfg.tq - 1) >> CHUNK_SHIFT) + 1) << CHUNK_SHIFT, cfg.s_valid)
    return qend, kmin // cfg.tk, (kend + cfg.tk - 1) // cfg.tk


def _causal_limits(i, cfg):
    q0 = cfg.q_off + i * cfg.tq
    qpos = q0 + lax.broadcasted_iota(I32, (cfg.tq, 1), 0)
    return qpos + 1, (q0 + 1) // cfg.tk, (q0 + cfg.tq + cfg.tk - 1) // cfg.tk


def _qk(q, k):
    return lax.dot_general(q, k, (((1,), (1,)), ((), ())), preferred_element_type=F32)


def _visibility_bias(ks, qend, shape):
    kpos = ks + lax.broadcasted_iota(I32, shape, 1)
    return jnp.where(kpos < qend, 0.0, MASKED)


def _flash_step(qs, ks, vs, bias_fn, state, *, rows, strip):
    s_ref, p_ref, m_ref, l_ref, alpha_ref, acc_ref = state
    n = len(qs)
    strips = [slice(r * strip, (r + 1) * strip) for r in range(rows // strip)]
    chunks = [slice(a, a + LANES) for a in range(0, s_ref.shape[-1], LANES)]
    for c in range(n):
        s_ref[c] = _qk(qs[c], ks[c])
    for c in range(n):
        for r, rs in enumerate(strips):
            mx = None
            for ch in chunks:
                s = s_ref[c, rs, ch]
                b = bias_fn(c, r, ch)
                if b is not None:
                    s = s + b
                    s_ref[c, rs, ch] = s
                mx = s if mx is None else jnp.maximum(mx, s)
            m_old = m_ref[c, rs, :]
            m_new = jnp.maximum(m_old, jnp.broadcast_to(jnp.max(mx, axis=-1, keepdims=True), mx.shape))
            alpha_ref[c, rs, :] = jnp.exp2(m_old - m_new)
            m_ref[c, rs, :] = m_new
        for rs in strips:
            m = m_ref[c, rs, :]
            psum = None
            for ch in chunks:
                p = jnp.exp2(s_ref[c, rs, ch] - m)
                if l_ref is not None:
                    psum = p if psum is None else psum + p
                p_ref[c, rs, ch] = p.astype(BF16)
            if l_ref is not None:
                l_ref[c, rs, :] = alpha_ref[c, rs, :] * l_ref[c, rs, :] + psum
    for c in range(n):
        pv = jnp.dot(p_ref[c], vs[c], preferred_element_type=F32)
        alpha = alpha_ref[c]
        for a in range(0, acc_ref.shape[-1], LANES):
            acc_ref[c, :, a:a + LANES] = alpha * acc_ref[c, :, a:a + LANES] + pv[:, a:a + LANES]


def _init_state(state):
    _, _, m_ref, l_ref, _, acc_ref = state
    m_ref[...] = jnp.full(m_ref.shape, MASKED, F32)
    acc_ref[...] = jnp.zeros(acc_ref.shape, F32)
    if l_ref is not None:
        l_ref[...] = jnp.zeros(l_ref.shape, F32)


def _flash_scratch(chains, rows, tk, dv, with_l):
    stat = pltpu.VMEM((chains, rows, LANES), F32)
    return ([pltpu.VMEM((chains, rows, tk), F32), pltpu.VMEM((chains, rows, tk), BF16), stat]
            + ([stat] if with_l else []) + [stat, pltpu.VMEM((chains, rows, dv), F32)])


def _with_ones(v):
    return jnp.concatenate([v, jnp.ones(v.shape, v.dtype)], axis=1)


def _dsa_kernel(q_ref, qi_ref, sm_ref, k_ref, v_ref, kit_ref, o_ref,
                key_ref, bias_ref, s_ref, p_ref, m_ref, alpha_ref, acc_ref, *, cfg, topk):
    tq, tk = cfg.tq, cfg.tk
    i = pl.program_id(1)
    qend, _, n_kv = _chunk_limits(i, cfg)
    w = sm_ref[:, IDX_DIM:IDX_DIM + IDX_HEADS] * (IDX_DIM ** -0.5 * IDX_HEADS ** -0.5)

    def score_block(j, carry):
        ks = pl.multiple_of(j * tk, tk)
        acc = jnp.zeros((tq, tk), F32)
        for pair in range(IDX_HEADS // 2):
            qp = qi_ref[:, pair * LANES:(pair + 1) * LANES]
            for e in range(2):
                hd = 2 * pair + e
                sc = jnp.dot(qp, kit_ref[e, :, pl.ds(ks, tk)], preferred_element_type=F32)
                acc = acc + w[:, hd:hd + 1] * jnp.maximum(sc, 0.0)
        bits = lax.bitcast_convert_type(acc, I32)
        key = bits ^ ((bits >> 31) & 0x7FFFFFFF)
        key = jnp.where(acc == 0.0, 0, key)
        kpos = ks + lax.broadcasted_iota(I32, (tq, tk), 1)
        key_ref[:, pl.ds(ks, tk)] = jnp.where(kpos < qend, key, NEG_INF_KEY)
        return carry

    lax.fori_loop(0, n_kv, score_block, 0)

    def count_ge(cand):
        def body(j, part):
            ks = pl.multiple_of(j * tk, tk)
            ge = jnp.where(key_ref[:, pl.ds(ks, tk)] >= cand, 1, 0)
            for c in range(tk // LANES):
                part = part + ge[:, c * LANES:(c + 1) * LANES]
            return part
        part = lax.fori_loop(0, n_kv, body, jnp.zeros((tq, LANES), I32))
        return jnp.sum(part, axis=-1, keepdims=True)

    small = count_ge(jnp.full((tq, 1), NEG_INF_KEY + 1, I32)) <= topk

    def pending(cnt):
        return jnp.max(jnp.where((cnt != topk) & jnp.logical_not(small), 1, 0)) > 0

    def bits_cond(carry):
        t, _, cnt = carry
        return (t < 32) & pending(cnt)

    def bits_body(carry):
        t, lo, cnt = carry
        for u in range(BISECT_UNROLL):
            cand = lo + jnp.left_shift(1, 31 - (t + u))
            c = count_ge(cand)
            ok = c >= topk
            lo = jnp.where(ok, cand, lo)
            cnt = jnp.where(ok, c, cnt)
        return t + BISECT_UNROLL, lo, cnt

    _, thr, n_ge = lax.while_loop(
        bits_cond, bits_body,
        (jnp.int32(0), jnp.full((tq, 1), INT_MIN, I32), jnp.zeros((tq, 1), I32) + n_kv * tk))
    tie_rows = (n_ge > topk) & jnp.logical_not(small)

    @pl.when(jnp.max(jnp.where(tie_rows, 1, 0)) > 0)
    def _drop_late_ties():
        need_f = (topk - count_ge(thr + 1)).astype(F32)
        upper = jnp.where(lax.broadcasted_iota(I32, (tk, tk), 0) <= lax.broadcasted_iota(I32, (tk, tk), 1),
                          1.0, 0.0).astype(BF16)

        def body(j, seen):
            ks = pl.multiple_of(j * tk, tk)
            kb = key_ref[:, pl.ds(ks, tk)]
            eq = kb == thr
            eq_f = jnp.where(eq, 1.0, 0.0)
            rank = seen + jnp.dot(eq_f.astype(BF16), upper, preferred_element_type=F32)
            drop = eq & (rank > need_f) & tie_rows
            key_ref[:, pl.ds(ks, tk)] = jnp.where(drop, thr - 1, kb)
            return seen + jnp.sum(eq_f, axis=-1, keepdims=True)

        lax.fori_loop(0, n_kv, body, jnp.zeros((tq, 1), F32))

    thr_keep = jnp.where(small, NEG_INF_KEY + 1, jnp.maximum(thr, NEG_INF_KEY + 1))
    state = (s_ref, p_ref, m_ref, None, alpha_ref, acc_ref)
    _init_state(state)
    strip = min(tq, STRIP_ROWS)
    strips_per_head = tq // strip

    def attn_block(j, carry):
        ks = pl.multiple_of(j * tk, tk)
        bias_ref[...] = jnp.where(key_ref[:, pl.ds(ks, tk)] >= thr_keep, 0.0, MASKED)
        cols = [slice(g * HEAD_DIM, (g + 1) * HEAD_DIM) for g in range(A_KV_HEADS)]
        _flash_step([q_ref[g * A_REP:(g + 1) * A_REP].reshape(A_REP * tq, HEAD_DIM) for g in range(A_KV_HEADS)],
                    [k_ref[pl.ds(ks, tk), c] for c in cols],
                    [_with_ones(v_ref[pl.ds(ks, tk), c]) for c in cols],
                    lambda c, r, ch: bias_ref[(r % strips_per_head) * strip:(r % strips_per_head + 1) * strip, ch],
                    state, rows=A_REP * tq, strip=strip)
        return carry

    lax.fori_loop(0, n_kv, attn_block, 0)
    for g in range(A_KV_HEADS):
        o = acc_ref[g, :, :HEAD_DIM] / acc_ref[g, :, HEAD_DIM:]
        for r in range(A_REP):
            hd = g * A_REP + r
            o_ref[:, hd * HEAD_DIM:(hd + 1) * HEAD_DIM] = o[r * tq:(r + 1) * tq].astype(BF16)


def _dsa_attention(cfg, q_hm, q_row0, qi, sm, k, v, kit, topk):
    tq, nq, s = cfg.tq, cfg.nq, cfg.s_pad
    qb0 = q_row0 // tq
    kern = functools.partial(_dsa_kernel, cfg=cfg, topk=topk)
    return pl.pallas_call(
        kern, grid=(cfg.batch, nq), name="dsa_attention",
        in_specs=[pl.BlockSpec((A_HEADS, tq, HEAD_DIM), lambda b, i: (0, qb0 + b * nq + i, 0)),
                  pl.BlockSpec((tq, IDX_HEADS * IDX_DIM), lambda b, i: (qb0 + b * nq + i, 0)),
                  pl.BlockSpec((tq, LANES), lambda b, i: (qb0 + b * nq + i, 0)),
                  pl.BlockSpec((s, A_KV_HEADS * HEAD_DIM), lambda b, i: (b, 0)),
                  pl.BlockSpec((s, A_KV_HEADS * HEAD_DIM), lambda b, i: (b, 0)),
                  pl.BlockSpec((None, 2, LANES, s), lambda b, i: (b, 0, 0, 0))],
        out_specs=pl.BlockSpec((tq, MIX_WIDTH), lambda b, i: (b * nq + i, 0)),
        out_shape=jax.ShapeDtypeStruct((cfg.batch * cfg.t_q, MIX_WIDTH), BF16),
        scratch_shapes=[pltpu.VMEM((tq, s), I32), pltpu.VMEM((tq, cfg.tk), F32)]
        + _flash_scratch(A_KV_HEADS, A_REP * tq, cfg.tk, 2 * HEAD_DIM, with_l=False),
        compiler_params=_params(("parallel", "arbitrary")))(q_hm, qi, sm, k, v, kit)


B_STEP_HEADS = 2


def _run_blocks(n_full, n_kv, step):
    def plain(j, carry):
        step(j, False)
        return carry

    def masked(j, carry):
        step(j, True)
        return carry

    lax.fori_loop(0, n_full, plain, 0)
    lax.fori_loop(n_full, n_kv, masked, 0)


def _diff_kernel(q_ref, k_ref, v_ref, lq1_ref, lk1_ref, lq2_ref, lk2_ref, sub_ref, o_ref,
                 bias_ref, s_ref, p_ref, m_ref, l_ref, alpha_ref, acc_ref, *, cfg, lam_init):
    tq, tk = cfg.tq, cfg.tk
    qend, n_full, n_kv = _chunk_limits(pl.program_id(2), cfg)
    state = (s_ref, p_ref, m_ref, l_ref, alpha_ref, acc_ref)
    _init_state(state)
    strip = min(tq, STRIP_ROWS)
    n_chains = 2 * B_STEP_HEADS

    def step(j, masked):
        ks = pl.multiple_of(j * tk, tk)
        if masked:
            bias_ref[...] = _visibility_bias(ks, qend, (tq, tk))
        qk_cols = [slice(c * HEAD_DIM, (c + 1) * HEAD_DIM) for c in range(n_chains)]
        v_cols = [slice((c // 2) * B_V_DIM, (c // 2 + 1) * B_V_DIM) for c in range(n_chains)]
        _flash_step([q_ref[:, c] for c in qk_cols],
                    [k_ref[pl.ds(ks, tk), c] for c in qk_cols],
                    [v_ref[pl.ds(ks, tk), c] for c in v_cols],
                    (lambda c, r, ch: bias_ref[r * strip:(r + 1) * strip, ch]) if masked
                    else (lambda c, r, ch: None),
                    state, rows=tq, strip=strip)

    _run_blocks(n_full, n_kv, step)
    lam = (jnp.exp(jnp.sum(lq1_ref[...] * lk1_ref[...], axis=-1, keepdims=True))
           - jnp.exp(jnp.sum(lq2_ref[...] * lk2_ref[...], axis=-1, keepdims=True)) + lam_init)
    def normalised(c):
        return acc_ref[c] / jnp.sum(l_ref[c], axis=-1, keepdims=True)

    for h in range(B_STEP_HEADS):
        o = normalised(2 * h) - lam * normalised(2 * h + 1)
        ms = jnp.mean(o * o, axis=-1, keepdims=True)
        o_ref[:, h * B_V_DIM:(h + 1) * B_V_DIM] = (
            o * lax.rsqrt(ms + EPS) * sub_ref[...] * (1.0 - lam_init)).astype(BF16)


def _diff_attention(cfg, q, q_row0, k, v, lam_vecs, subln, lam_init):
    tq, nq, s = cfg.tq, cfg.nq, cfg.s_pad
    qb0 = q_row0 // tq
    width = B_STEP_HEADS * B_V_DIM
    kern = functools.partial(_diff_kernel, cfg=cfg, lam_init=lam_init)
    vec = pl.BlockSpec((1, HEAD_DIM), lambda b, h, i: (0, 0))
    return pl.pallas_call(
        kern, grid=(cfg.batch, B_HEADS // B_STEP_HEADS, nq), name="diff_attention",
        in_specs=[pl.BlockSpec((tq, width), lambda b, h, i: (qb0 + b * nq + i, h)),
                  pl.BlockSpec((s, width), lambda b, h, i: (b, h)),
                  pl.BlockSpec((s, width), lambda b, h, i: (b, h)),
                  vec, vec, vec, vec,
                  pl.BlockSpec((1, B_V_DIM), lambda b, h, i: (0, 0))],
        out_specs=pl.BlockSpec((tq, width), lambda b, h, i: (b * nq + i, h)),
        out_shape=jax.ShapeDtypeStruct((cfg.batch * cfg.t_q, MIX_WIDTH), BF16),
        scratch_shapes=[pltpu.VMEM((tq, cfg.tk), F32)]
        + _flash_scratch(2 * B_STEP_HEADS, tq, cfg.tk, B_V_DIM, with_l=True),
        compiler_params=_params(("parallel", "parallel", "arbitrary")))(
            q, k, v, *[a.reshape(1, HEAD_DIM) for a in lam_vecs], subln.reshape(1, B_V_DIM))


C_STEP_HEADS = 4


def _fox_kernel(q_ref, k_ref, v_ref, cq_ref, ck_ref, o_ref,
                bias_ref, cqh_ref, s_ref, p_ref, m_ref, alpha_ref, acc_ref, *, cfg):
    tq, tk = cfg.tq, cfg.tk
    hg = pl.program_id(1)
    qend, n_full, n_kv = _causal_limits(pl.program_id(2), cfg)
    state = (s_ref, p_ref, m_ref, None, alpha_ref, acc_ref)
    _init_state(state)
    strip = min(tq, STRIP_ROWS)
    lane = lax.broadcasted_iota(I32, (tq, LANES), 1)
    for c in range(C_STEP_HEADS):
        head_lane = lane == hg * C_STEP_HEADS + c
        cqh_ref[c] = jnp.broadcast_to(
            jnp.sum(jnp.where(head_lane, cq_ref[...], 0.0), axis=-1, keepdims=True), (tq, LANES))

    def step(j, masked):
        ks = pl.multiple_of(j * tk, tk)
        if masked:
            bias_ref[...] = _visibility_bias(ks, qend, (tq, tk))

        def bias(c, r, ch):
            rs = slice(r * strip, (r + 1) * strip)
            b = cqh_ref[c, rs, :] - ck_ref[c:c + 1, pl.ds(pl.multiple_of(ks + ch.start, LANES), LANES)]
            return b + bias_ref[rs, ch] if masked else b

        cols = [slice(c * HEAD_DIM, (c + 1) * HEAD_DIM) for c in range(C_STEP_HEADS)]
        _flash_step([q_ref[:, c] for c in cols],
                    [k_ref[pl.ds(ks, tk), c] for c in cols],
                    [_with_ones(v_ref[pl.ds(ks, tk), c]) for c in cols],
                    bias, state, rows=tq, strip=strip)

    _run_blocks(n_full, n_kv, step)
    for c in range(C_STEP_HEADS):
        o_ref[:, c * HEAD_DIM:(c + 1) * HEAD_DIM] = (
            acc_ref[c, :, :HEAD_DIM] / acc_ref[c, :, HEAD_DIM:]).astype(BF16)


def _fox_attention(cfg, q, q_row0, k, v, cq, ck_rows):
    tq, nq, s = cfg.tq, cfg.nq, cfg.s_pad
    qb0 = q_row0 // tq
    width = C_STEP_HEADS * HEAD_DIM
    groups = C_HEADS // C_STEP_HEADS
    kern = functools.partial(_fox_kernel, cfg=cfg)
    return pl.pallas_call(
        kern, grid=(cfg.batch, groups, nq), name="fox_attention",
        in_specs=[pl.BlockSpec((tq, width), lambda b, h, i: (qb0 + b * nq + i, h)),
                  pl.BlockSpec((s, width), lambda b, h, i: (b, h)),
                  pl.BlockSpec((s, width), lambda b, h, i: (b, h)),
                  pl.BlockSpec((tq, LANES), lambda b, h, i: (qb0 + b * nq + i, 0)),
                  pl.BlockSpec((None, C_STEP_HEADS, s), lambda b, h, i: (b * groups + h, 0, 0))],
        out_specs=pl.BlockSpec((tq, width), lambda b, h, i: (b * nq + i, h)),
        out_shape=jax.ShapeDtypeStruct((cfg.batch * cfg.t_q, MIX_WIDTH), BF16),
        scratch_shapes=[pltpu.VMEM((tq, cfg.tk), F32), pltpu.VMEM((C_STEP_HEADS, tq, LANES), F32)]
        + _flash_scratch(C_STEP_HEADS, tq, cfg.tk, 2 * HEAD_DIM, with_l=False),
        compiler_params=_params(("parallel", "parallel", "arbitrary")))(q, k, v, cq, ck_rows)


def _cumsum_kernel(x_ref, o_ref, carry_ref, *, tb):
    @pl.when(pl.program_id(1) == 0)
    def _():
        carry_ref[...] = jnp.zeros(carry_ref.shape, F32)

    x = x_ref[...]
    hi = x.astype(BF16)
    r1 = x - hi.astype(F32)
    mid = r1.astype(BF16)
    low = (r1 - mid.astype(F32)).astype(BF16)
    tri = jnp.where(lax.broadcasted_iota(I32, (tb, tb), 0) >= lax.broadcasted_iota(I32, (tb, tb), 1),
                    1.0, 0.0).astype(BF16)
    c = (jnp.dot(tri, hi, preferred_element_type=F32) + jnp.dot(tri, mid, preferred_element_type=F32)
         + jnp.dot(tri, low, preferred_element_type=F32)) + carry_ref[...]
    o_ref[...] = c * LOG2E
    carry_ref[...] = c[tb - 1:tb, :]


def _cumsum_rows(x, batch, t, tb):
    nt = t // tb
    return pl.pallas_call(
        functools.partial(_cumsum_kernel, tb=tb), grid=(batch, nt), name="cumsum_rows",
        in_specs=[pl.BlockSpec((tb, LANES), lambda b, j: (b * nt + j, 0))],
        out_specs=pl.BlockSpec((tb, LANES), lambda b, j: (b * nt + j, 0)),
        out_shape=jax.ShapeDtypeStruct((batch * t, LANES), F32),
        scratch_shapes=[pltpu.VMEM((1, LANES), F32)],
        compiler_params=_params(("parallel", "arbitrary")))(x)


def _split_w(w_in, sizes):
    offs = np.cumsum((0,) + tuple(sizes))
    return [w_in[:, int(offs[i]):int(offs[i + 1])] for i in range(len(sizes))]


def _pad_rows(a, rows):
    return jnp.pad(a, ((0, 0), (0, rows - a.shape[1])) + ((0, 0),) * (a.ndim - 2))


def _sample_keys(cache, new, s_pad):
    b = cache.shape[0]
    full = jnp.concatenate([cache.reshape(b, cache.shape[1], -1), new.reshape(b, new.shape[1], -1)], axis=1)
    return _pad_rows(full, s_pad).astype(BF16).reshape(b * s_pad, -1)


def _key_transposed_pairs(ki, batch, s):
    kt = jnp.swapaxes(ki.reshape(batch, s, IDX_DIM), 1, 2).astype(BF16)
    z = jnp.zeros_like(kt)
    return jnp.stack([jnp.concatenate([kt, z], axis=1), jnp.concatenate([z, kt], axis=1)], axis=1)


def _layer_a(pr, sr, cfg_p, cfg_s, xp, xs, ck, cv, cki, norm, w_in, w_out, qn, kn, ikn):
    wq, wk, wv, wqi, wki, wwi, wg = _split_w(w_in, A_SIZES)
    wsm = jnp.concatenate([wki, wwi, jnp.zeros((D_MODEL, LANES - IDX_DIM - IDX_HEADS), F32)], axis=1)
    wq, wk, wv, wqi, wg, wsm, wo = [a.astype(BF16) for a in (wq, wk, wv, wqi, wg, wsm, w_out)]

    def project(rows, x):
        q = _proj_qk(rows, x, norm, wq, qn, scale=HEAD_DIM ** -0.5 * LOG2E, use_rope=True,
                     head_major=True, emit_f32=False)[0]
        k_f, k_b = _proj_qk(rows, x, norm, wk, kn, scale=1.0, use_rope=True, head_major=False, emit_f32=True)
        v_f, v_b = _proj_plain(rows, x, norm, wv)
        qi = _proj_rope64(rows, x, norm, wqi)
        sm = _proj_idx_small(rows, x, norm, wsm, ikn)
        sg = _proj_gate(rows, x, norm, wg)
        return q, k_f, k_b, v_f, v_b, qi, sm, sg

    q, k_f, k_b, v_f, v_b, qi, sm, sg = project(pr, xp)
    ki_f = sm[:, :IDX_DIM]
    o = _dsa_attention(cfg_p, q, 0, qi, sm, k_b, v_b,
                       _key_transposed_pairs(ki_f, pr.batch, pr.t_q), min(TOPK_MAX, pr.t_q // 4))
    yp = _out_proj(pr, xp, o, sg, wo)

    q2, k2_f, _, v2_f, _, qi2, sm2, sg2 = project(sr, xs)
    ki2_f = sm2[:, :IDX_DIM]
    b, s = sr.batch, sr.t_q
    k_all = _sample_keys(ck, k2_f.reshape(b, s, -1), cfg_s.s_pad)
    v_all = _sample_keys(cv, v2_f.reshape(b, s, -1), cfg_s.s_pad)
    ki_all = _pad_rows(jnp.concatenate([cki, ki2_f.reshape(b, s, IDX_DIM)], axis=1), cfg_s.s_pad)
    o2 = _dsa_attention(cfg_s, q2, 0, qi2, sm2, k_all, v_all,
                        _key_transposed_pairs(ki_all.reshape(-1, IDX_DIM), b, cfg_s.s_pad),
                        min(TOPK_MAX, cfg_s.s_valid // 4))
    ys = _out_proj(sr, xs, o2, sg2, wo)
    kv = (A_KV_HEADS, HEAD_DIM)
    state = (k_f.reshape(pr.batch, pr.t_q, *kv), v_f.reshape(pr.batch, pr.t_q, *kv),
             ki_f.reshape(pr.batch, pr.t_q, IDX_DIM),
             k2_f.reshape(b, s, *kv), v2_f.reshape(b, s, *kv), ki2_f.reshape(b, s, IDX_DIM))
    return yp, ys, state


def _layer_b(layer, pr, sr, cfg_p, cfg_s, xp, xs, ck, cv, norm, w_in, w_out, qn, kn,
             lq1, lk1, lq2, lk2, subln):
    lam_init = 0.8 - 0.6 * float(np.exp(-0.3 * layer))
    wq, wk, wv, wg, wo = [a.astype(BF16) for a in _split_w(w_in, B_SIZES) + [w_out]]

    def project(rows, x):
        q = _proj_qk(rows, x, norm, wq, qn, scale=HEAD_DIM ** -0.5 * LOG2E, use_rope=True,
                     head_major=False, emit_f32=False)[0]
        k_f, k_b = _proj_qk(rows, x, norm, wk, kn, scale=1.0, use_rope=True, head_major=False, emit_f32=True)
        v_f, v_b = _proj_plain(rows, x, norm, wv)
        return q, k_f, k_b, v_f, v_b, _proj_gate(rows, x, norm, wg)

    lam_vecs = (lq1, lk1, lq2, lk2)
    q, k_f, k_b, v_f, v_b, sg = project(pr, xp)
    o = _diff_attention(cfg_p, q, 0, k_b, v_b, lam_vecs, subln, lam_init)
    yp = _out_proj(pr, xp, o, sg, wo)

    q2, k2_f, _, v2_f, _, sg2 = project(sr, xs)
    b, s = sr.batch, sr.t_q
    k_all = _sample_keys(ck, k2_f.reshape(b, s, -1), cfg_s.s_pad)
    v_all = _sample_keys(cv, v2_f.reshape(b, s, -1), cfg_s.s_pad)
    o2 = _diff_attention(cfg_s, q2, 0, k_all, v_all, lam_vecs, subln, lam_init)
    ys = _out_proj(sr, xs, o2, sg2, wo)
    state = (k_f.reshape(pr.batch, pr.t_q, 2 * B_HEADS, HEAD_DIM), v_f.reshape(pr.batch, pr.t_q, B_HEADS, B_V_DIM),
             k2_f.reshape(b, s, 2 * B_HEADS, HEAD_DIM), v2_f.reshape(b, s, B_HEADS, B_V_DIM))
    return yp, ys, state


def _head_rows(c, batch, s):
    rows = jnp.swapaxes(c.reshape(batch, s, LANES)[:, :, :C_HEADS], 1, 2)
    return rows.reshape(batch * C_HEADS // C_STEP_HEADS, C_STEP_HEADS, s)


def _layer_c(pr, sr, cfg_p, cfg_s, xp, xs, ck, cv, clogf, norm, w_in, w_out, qn, kn, fb):
    wq, wk, wv, wf, wg = _split_w(w_in, C_SIZES)
    wf = jnp.concatenate([wf, jnp.zeros((D_MODEL, LANES - C_HEADS), F32)], axis=1)
    wq, wk, wv, wf, wg, wo = [a.astype(BF16) for a in (wq, wk, wv, wf, wg, w_out)]

    def project(rows, x):
        q = _proj_qk(rows, x, norm, wq, qn, scale=HEAD_DIM ** -0.5 * LOG2E, use_rope=False,
                     head_major=False, emit_f32=False)[0]
        k_f, k_b = _proj_qk(rows, x, norm, wk, kn, scale=1.0, use_rope=False, head_major=False, emit_f32=True)
        v_f, v_b = _proj_plain(rows, x, norm, wv)
        return q, k_f, k_b, v_f, v_b, _proj_logf(rows, x, norm, wf, fb), _proj_gate(rows, x, norm, wg)

    q, k_f, k_b, v_f, v_b, logf, sg = project(pr, xp)
    c = _cumsum_rows(logf, pr.batch, pr.t_q, 512)
    o = _fox_attention(cfg_p, q, 0, k_b, v_b, c, _head_rows(c, pr.batch, pr.t_q))
    yp = _out_proj(pr, xp, o, sg, wo)

    q2, k2_f, _, v2_f, _, logf2, sg2 = project(sr, xs)
    b, s = sr.batch, sr.t_q
    k_all = _sample_keys(ck, k2_f.reshape(b, s, -1), cfg_s.s_pad)
    v_all = _sample_keys(cv, v2_f.reshape(b, s, -1), cfg_s.s_pad)
    logf_all = jnp.concatenate([jnp.pad(clogf.astype(F32), ((0, 0), (0, 0), (0, LANES - C_HEADS))),
                                logf2.reshape(b, s, LANES)], axis=1)
    c2 = _cumsum_rows(_pad_rows(logf_all, cfg_s.s_pad).reshape(b * cfg_s.s_pad, LANES), b, cfg_s.s_pad, LANES)
    cq2 = c2.reshape(b, cfg_s.s_pad, LANES)[:, PAST_LEN:PAST_LEN + s].reshape(b * s, LANES)
    o2 = _fox_attention(cfg_s, q2, 0, k_all, v_all, cq2, _head_rows(c2, b, cfg_s.s_pad))
    ys = _out_proj(sr, xs, o2, sg2, wo)
    hd = (C_HEADS, HEAD_DIM)
    state = (k_f.reshape(pr.batch, pr.t_q, *hd), v_f.reshape(pr.batch, pr.t_q, *hd),
             logf[:, :C_HEADS].reshape(pr.batch, pr.t_q, C_HEADS),
             k2_f.reshape(b, s, *hd), v2_f.reshape(b, s, *hd), logf2[:, :C_HEADS].reshape(b, s, C_HEADS))
    return yp, ys, state


def kernel(x_prompt, x_sample, cache_l0_k, cache_l0_v, cache_l0_kidx, cache_l1_k, cache_l1_v, cache_l2_k, cache_l2_v, cache_l2_logf, cache_l3_k, cache_l3_v, cache_l3_kidx, l0_norm, l0_w_in, l0_w_out, l0_q_norm, l0_k_norm, l0_idx_k_norm, l1_norm, l1_w_in, l1_w_out, l1_q_norm, l1_k_norm, l1_lambda_q1, l1_lambda_k1, l1_lambda_q2, l1_lambda_k2, l1_subln, l2_norm, l2_w_in, l2_w_out, l2_q_norm, l2_k_norm, l2_forget_bias, l3_norm, l3_w_in, l3_w_out, l3_q_norm, l3_k_norm, l3_idx_k_norm):
    bp, tp, d = x_prompt.shape
    bs, ts, _ = x_sample.shape
    s_valid = PAST_LEN + ts
    s_pad = -(-s_valid // LANES) * LANES
    pr = _Rows(bp, tp, 0, 512)
    sr = _Rows(bs, ts, PAST_LEN, bs * ts)
    a_p = _Attn(bp, tp, tp, tp, 0, 128, 512)
    a_s = _Attn(bs, ts, s_pad, s_valid, PAST_LEN, ts, LANES)
    bc_p = _Attn(bp, tp, tp, tp, 0, 256, 512)
    bc_s = a_s

    xp = x_prompt.reshape(bp * tp, d)
    xs = x_sample.reshape(bs * ts, d)
    xp, xs, st0 = _layer_a(pr, sr, a_p, a_s, xp, xs, cache_l0_k, cache_l0_v, cache_l0_kidx,
                           l0_norm, l0_w_in, l0_w_out, l0_q_norm, l0_k_norm, l0_idx_k_norm)
    xp, xs, st1 = _layer_b(1, pr, sr, bc_p, bc_s, xp, xs, cache_l1_k, cache_l1_v,
                           l1_norm, l1_w_in, l1_w_out, l1_q_norm, l1_k_norm,
                           l1_lambda_q1, l1_lambda_k1, l1_lambda_q2, l1_lambda_k2, l1_subln)
    xp, xs, st2 = _layer_c(pr, sr, bc_p, bc_s, xp, xs, cache_l2_k, cache_l2_v, cache_l2_logf,
                           l2_norm, l2_w_in, l2_w_out, l2_q_norm, l2_k_norm, l2_forget_bias)
    xp, xs, st3 = _layer_a(pr, sr, a_p, a_s, xp, xs, cache_l3_k, cache_l3_v, cache_l3_kidx,
                           l3_norm, l3_w_in, l3_w_out, l3_q_norm, l3_k_norm, l3_idx_k_norm)
    return (xp.reshape(bp, tp, d), xs.reshape(bs, ts, d)) + st0 + st1 + st2 + st3
```

```python
import functools

import numpy as np
import jax
import jax.numpy as jnp
from jax import lax
from jax.experimental import pallas as pl
from jax.experimental.pallas import tpu as pltpu

F32 = jnp.float32
BF16 = jnp.bfloat16
I32 = jnp.int32

D_MODEL = 2048
PAST_LEN = 1024
CHUNK_SHIFT = 6
ROPE_THETA = 10000.0
EPS = 1e-6
HEAD_DIM = 128
A_HEADS = 16
A_KV_HEADS = 4
A_REP = A_HEADS // A_KV_HEADS
IDX_HEADS = 16
IDX_DIM = 64
TOPK_MAX = 256
B_HEADS = 8
B_V_DIM = 256
C_HEADS = 16
MIX_WIDTH = D_MODEL

A_SIZES = (2048, 512, 512, 1024, 64, 16, 2048)
B_SIZES = (2048, 2048, 2048, 2048)
C_SIZES = (2048, 2048, 2048, 16, 2048)

LANES = 128
MXU_N = 256
VMEM_LIMIT = 56 * 1024 * 1024
MASKED = -1e30
LOG2E = 1.4426950408889634
STRIP_ROWS = 32
BISECT_UNROLL = 2
INT_MIN = -2 ** 31
NEG_INF_KEY = -2139095041


def _params(sem):
    return pltpu.CompilerParams(dimension_semantics=sem, vmem_limit_bytes=VMEM_LIMIT)


def _norm_rows(x_ref, g_ref):
    x = x_ref[...]
    ms = jnp.mean(x * x, axis=-1, keepdims=True)
    return (x * lax.rsqrt(ms + EPS) * g_ref[...]).astype(BF16)


def _rope128(y, cos, sin):
    return y * cos + pltpu.roll(y, 64, 1) * sin


def _rope64(y, cos, sin):
    lane = lax.broadcasted_iota(I32, y.shape, 1)
    rot = jnp.where((lane & 63) < 32, pltpu.roll(y, 96, 1), pltpu.roll(y, 32, 1))
    return y * cos + rot * sin


def _proj_qk_kernel(x_ref, g_ref, w_ref, cos_ref, sin_ref, hg_ref, *outs,
                    n_heads, scale, use_rope, head_major, emit_f32):
    h = _norm_rows(x_ref, g_ref)
    hg = hg_ref[...]
    of_ref = outs[0] if emit_f32 else None
    ob_ref = outs[-1]
    for c in range(n_heads // 2):
        y = jnp.dot(h, w_ref[:, c * MXU_N:(c + 1) * MXU_N], preferred_element_type=F32)
        for j in range(2):
            hd = 2 * c + j
            yh = y[:, j * LANES:(j + 1) * LANES]
            ms = jnp.mean(yh * yh, axis=-1, keepdims=True)
            yr = yh * lax.rsqrt(ms + EPS) * hg
            if use_rope:
                yr = _rope128(yr, cos_ref[...], sin_ref[...])
            if emit_f32:
                of_ref[:, hd * LANES:(hd + 1) * LANES] = yr
            yb = (yr * scale).astype(BF16)
            if head_major:
                ob_ref[hd] = yb
            else:
                ob_ref[:, hd * LANES:(hd + 1) * LANES] = yb


def _proj_plain_kernel(x_ref, g_ref, w_ref, of_ref, ob_ref, *, n_cols):
    h = _norm_rows(x_ref, g_ref)
    for c in range(n_cols // MXU_N):
        sl = slice(c * MXU_N, (c + 1) * MXU_N)
        y = jnp.dot(h, w_ref[:, sl], preferred_element_type=F32)
        of_ref[:, sl] = y
        ob_ref[:, sl] = y.astype(BF16)


def _proj_gate_kernel(x_ref, g_ref, w_ref, ob_ref, *, n_cols):
    h = _norm_rows(x_ref, g_ref)
    for c in range(n_cols // MXU_N):
        sl = slice(c * MXU_N, (c + 1) * MXU_N)
        y = jnp.dot(h, w_ref[:, sl], preferred_element_type=F32)
        ob_ref[:, sl] = (y * (1.0 / (1.0 + jnp.exp(-y)))).astype(BF16)


def _proj_rope64_kernel(x_ref, g_ref, w_ref, cos_ref, sin_ref, ob_ref, *, n_cols):
    h = _norm_rows(x_ref, g_ref)
    cos = cos_ref[...]
    sin = sin_ref[...]
    for c in range(n_cols // MXU_N):
        y = jnp.dot(h, w_ref[:, c * MXU_N:(c + 1) * MXU_N], preferred_element_type=F32)
        for j in range(2):
            col = c * MXU_N + j * LANES
            ob_ref[:, col:col + LANES] = _rope64(y[:, j * LANES:(j + 1) * LANES], cos, sin).astype(BF16)


def _proj_idx_small_kernel(x_ref, g_ref, w_ref, cos_ref, sin_ref, hg_ref, o_ref):
    h = _norm_rows(x_ref, g_ref)
    y = jnp.dot(h, w_ref[...], preferred_element_type=F32)
    lane = lax.broadcasted_iota(I32, y.shape, 1)
    is_key = lane < IDX_DIM
    ms = jnp.sum(jnp.where(is_key, y * y, 0.0), axis=-1, keepdims=True) * (1.0 / IDX_DIM)
    kn = _rope64(y * lax.rsqrt(ms + EPS) * hg_ref[...], cos_ref[...], sin_ref[...])
    o_ref[...] = jnp.where(is_key, kn, y)


def _proj_logf_kernel(x_ref, g_ref, w_ref, fb_ref, o_ref):
    h = _norm_rows(x_ref, g_ref)
    z = jnp.dot(h, w_ref[...], preferred_element_type=F32) + fb_ref[...]
    o_ref[...] = jnp.minimum(z, 0.0) - jnp.log1p(jnp.exp(-jnp.abs(z)))


class _Cols:
    def __init__(self, array, n, block=0):
        assert array.shape[1] >= (block + 1) * n
        self.array, self.n, self.block = array, n, block


def _proj_call(kern, x, g, w, extra, extra_specs, out_shape, out_specs, tm):
    m, d = x.shape
    block = w.block
    in_specs = [pl.BlockSpec((tm, d), lambda i: (i, 0)),
                pl.BlockSpec((1, d), lambda i: (0, 0)),
                pl.BlockSpec((d, w.n), lambda i: (0, block))] + list(extra_specs)
    name = getattr(kern, "func", kern).__name__.strip("_")
    return pl.pallas_call(kern, grid=(m // tm,), in_specs=in_specs, out_specs=out_specs, name=name,
                          out_shape=out_shape, compiler_params=_params(("parallel",)))(
                              x, g.reshape(1, d), w.array, *extra)


class _Rows:
    def __init__(self, batch, t_q, pos0, tm):
        self.batch, self.t_q, self.pos0, self.tm = batch, t_q, pos0, tm
        self.m = batch * t_q
        self.tab_blocks = max(t_q // tm, 1)

    def tab_spec(self):
        nb = self.tab_blocks
        return pl.BlockSpec((self.tm, LANES), lambda i: (i % nb, 0))

    def tables(self, d):
        period = max(self.t_q, self.tm)
        pos = self.pos0 + (jnp.arange(period) % self.t_q)
        half = d // 2
        inv = ROPE_THETA ** (-2.0 * jnp.arange(half, dtype=F32) / d)
        ang = pos.astype(F32)[:, None] * inv[None, :]
        cos, sin = jnp.cos(ang), jnp.sin(ang)
        reps = LANES // d
        cos_f = jnp.tile(jnp.concatenate([cos, cos], axis=-1), (1, reps))
        sin_f = jnp.tile(jnp.concatenate([-sin, sin], axis=-1), (1, reps))
        return cos_f, sin_f


def _row_spec(tm, n):
    return pl.BlockSpec((tm, n), lambda i: (i, 0))


def _vec_spec(n):
    return pl.BlockSpec((1, n), lambda i: (0, 0))


def _proj_qk(rows, x, g, w, head_gain, *, scale, use_rope, head_major, emit_f32):
    n = w.n
    n_heads = n // HEAD_DIM
    cos, sin = rows.tables(HEAD_DIM)
    kern = functools.partial(_proj_qk_kernel, n_heads=n_heads, scale=scale, use_rope=use_rope,
                             head_major=head_major, emit_f32=emit_f32)
    out_shape, out_specs = [], []
    if emit_f32:
        out_shape.append(jax.ShapeDtypeStruct((rows.m, n), F32))
        out_specs.append(_row_spec(rows.tm, n))
    if head_major:
        out_shape.append(jax.ShapeDtypeStruct((n_heads, rows.m, HEAD_DIM), BF16))
        out_specs.append(pl.BlockSpec((n_heads, rows.tm, HEAD_DIM), lambda i: (0, i, 0)))
    else:
        out_shape.append(jax.ShapeDtypeStruct((rows.m, n), BF16))
        out_specs.append(_row_spec(rows.tm, n))
    return _proj_call(kern, x, g, w, (cos, sin, head_gain.reshape(1, HEAD_DIM)),
                      (rows.tab_spec(), rows.tab_spec(), _vec_spec(HEAD_DIM)),
                      out_shape, out_specs, rows.tm)


def _proj_plain(rows, x, g, w):
    n = w.n
    kern = functools.partial(_proj_plain_kernel, n_cols=n)
    return _proj_call(kern, x, g, w, (), (),
                      [jax.ShapeDtypeStruct((rows.m, n), F32), jax.ShapeDtypeStruct((rows.m, n), BF16)],
                      [_row_spec(rows.tm, n), _row_spec(rows.tm, n)], rows.tm)


def _proj_gate(rows, x, g, w):
    n = w.n
    kern = functools.partial(_proj_gate_kernel, n_cols=n)
    return _proj_call(kern, x, g, w, (), (), jax.ShapeDtypeStruct((rows.m, n), BF16),
                      _row_spec(rows.tm, n), rows.tm)


def _proj_rope64(rows, x, g, w):
    n = w.n
    cos, sin = rows.tables(IDX_DIM)
    kern = functools.partial(_proj_rope64_kernel, n_cols=n)
    return _proj_call(kern, x, g, w, (cos, sin), (rows.tab_spec(), rows.tab_spec()),
                      jax.ShapeDtypeStruct((rows.m, n), BF16), _row_spec(rows.tm, n), rows.tm)


def _proj_idx_small(rows, x, g, w, key_gain):
    cos, sin = rows.tables(IDX_DIM)
    gain = jnp.concatenate([key_gain, jnp.zeros((LANES - IDX_DIM,), F32)]).reshape(1, LANES)
    return _proj_call(_proj_idx_small_kernel, x, g, w, (cos, sin, gain),
                      (rows.tab_spec(), rows.tab_spec(), _vec_spec(LANES)),
                      jax.ShapeDtypeStruct((rows.m, LANES), F32), _row_spec(rows.tm, LANES), rows.tm)


def _proj_logf(rows, x, g, w, fb):
    fbp = jnp.concatenate([fb, jnp.zeros((LANES - C_HEADS,), F32)]).reshape(1, LANES)
    return _proj_call(_proj_logf_kernel, x, g, w, (fbp,), (_vec_spec(LANES),),
                      jax.ShapeDtypeStruct((rows.m, LANES), F32), _row_spec(rows.tm, LANES), rows.tm)


def _out_kernel(x_ref, o_ref, sg_ref, w_ref, y_ref, *, n_cols):
    a = o_ref[...] * sg_ref[...]
    for c in range(n_cols // MXU_N):
        sl = slice(c * MXU_N, (c + 1) * MXU_N)
        y_ref[:, sl] = x_ref[:, sl] + jnp.dot(a, w_ref[:, sl], preferred_element_type=F32)


def _out_proj(rows, x, o, sg, w):
    m, d = x.shape
    tm = rows.tm
    kern = functools.partial(_out_kernel, n_cols=d)
    return pl.pallas_call(
        kern, grid=(m // tm,), name="out_proj",
        in_specs=[_row_spec(tm, d), _row_spec(tm, d), _row_spec(tm, d),
                  pl.BlockSpec((d, d), lambda i: (0, 0))],
        out_specs=_row_spec(tm, d), out_shape=jax.ShapeDtypeStruct((m, d), F32),
        compiler_params=_params(("parallel",)))(x, o, sg, w)


class _Attn:
    def __init__(self, batch, t_q, s_pad, s_valid, q_off, tq, tk):
        self.batch, self.t_q, self.s_pad, self.s_valid = batch, t_q, s_pad, s_valid
        self.q_off, self.tq, self.tk = q_off, tq, tk
        self.nq = t_q // tq


def _chunk_limits(i, cfg):
    q0 = cfg.q_off + i * cfg.tq
    qpos = q0 + lax.broadcasted_iota(I32, (cfg.tq, 1), 0)
    qend = jnp.minimum(((qpos >> CHUNK_SHIFT) + 1) << CHUNK_SHIFT, cfg.s_valid)
    kmin = jnp.minimum(((q0 >> CHUNK_SHIFT) + 1) << CHUNK_SHIFT, cfg.s_valid)
    kend = jnp.minimum((((q0 + cfg.tq - 1) >> CHUNK_SHIFT) + 1) << CHUNK_SHIFT, cfg.s_valid)
    return qend, kmin // cfg.tk, (kend + cfg.tk - 1) // cfg.tk


def _causal_limits(i, cfg):
    q0 = cfg.q_off + i * cfg.tq
    qpos = q0 + lax.broadcasted_iota(I32, (cfg.tq, 1), 0)
    return qpos + 1, (q0 + 1) // cfg.tk, (q0 + cfg.tq + cfg.tk - 1) // cfg.tk


def _qk(q, k):
    return lax.dot_general(q, k, (((1,), (1,)), ((), ())), preferred_element_type=F32)


def _visibility_bias(ks, qend, shape):
    kpos = ks + lax.broadcasted_iota(I32, shape, 1)
    return jnp.where(kpos < qend, 0.0, MASKED)


def _flash_step(qs, ks, vs, bias_fn, state, *, rows, strip):
    s_ref, p_ref, m_ref, l_ref, alpha_ref, acc_ref = state
    n = len(qs)
    strips = [slice(r * strip, (r + 1) * strip) for r in range(rows // strip)]
    chunks = [slice(a, a + LANES) for a in range(0, s_ref.shape[-1], LANES)]
    for c in range(n):
        s_ref[c] = _qk(qs[c], ks[c])
    for c in range(n):
        for r, rs in enumerate(strips):
            mx = None
            for ch in chunks:
                s = s_ref[c, rs, ch]
                b = bias_fn(c, r, ch)
                if b is not None:
                    s = s + b
                    s_ref[c, rs, ch] = s
                mx = s if mx is None else jnp.maximum(mx, s)
            m_old = m_ref[c, rs, :]
            m_new = jnp.maximum(m_old, jnp.broadcast_to(jnp.max(mx, axis=-1, keepdims=True), mx.shape))
            alpha_ref[c, rs, :] = jnp.exp2(m_old - m_new)
            m_ref[c, rs, :] = m_new
        for rs in strips:
            m = m_ref[c, rs, :]
            psum = None
            for ch in chunks:
                p = jnp.exp2(s_ref[c, rs, ch] - m)
                if l_ref is not None:
                    psum = p if psum is None else psum + p
                p_ref[c, rs, ch] = p.astype(BF16)
            if l_ref is not None:
                l_ref[c, rs, :] = alpha_ref[c, rs, :] * l_ref[c, rs, :] + psum
    for c in range(n):
        alpha = jnp.concatenate([alpha_ref[c]] * (acc_ref.shape[-1] // LANES), axis=1)
        acc_ref[c] = alpha * acc_ref[c] + jnp.dot(p_ref[c], vs[c], preferred_element_type=F32)


def _init_state(state):
    _, _, m_ref, l_ref, _, acc_ref = state
    m_ref[...] = jnp.full(m_ref.shape, MASKED, F32)
    acc_ref[...] = jnp.zeros(acc_ref.shape, F32)
    if l_ref is not None:
        l_ref[...] = jnp.zeros(l_ref.shape, F32)


def _flash_scratch(chains, rows, tk, dv, with_l):
    stat = pltpu.VMEM((chains, rows, LANES), F32)
    return ([pltpu.VMEM((chains, rows, tk), F32), pltpu.VMEM((chains, rows, tk), BF16), stat]
            + ([stat] if with_l else []) + [stat, pltpu.VMEM((chains, rows, dv), F32)])


def _with_ones(v):
    return jnp.concatenate([v, jnp.ones(v.shape, v.dtype)], axis=1)


def _dsa_kernel(q_ref, qi_ref, sm_ref, k_ref, v_ref, kit_ref, o_ref,
                key_ref, bias_ref, s_ref, p_ref, m_ref, alpha_ref, acc_ref, *, cfg, topk):
    tq, tk = cfg.tq, cfg.tk
    i = pl.program_id(1)
    qend, _, n_kv = _chunk_limits(i, cfg)
    w = sm_ref[:, IDX_DIM:IDX_DIM + IDX_HEADS] * (IDX_DIM ** -0.5 * IDX_HEADS ** -0.5)

    def score_block(j, carry):
        ks = pl.multiple_of(j * tk, tk)
        acc = jnp.zeros((tq, tk), F32)
        for pair in range(IDX_HEADS // 2):
            qp = qi_ref[:, pair * LANES:(pair + 1) * LANES]
            for e in range(2):
                hd = 2 * pair + e
                sc = jnp.dot(qp, kit_ref[e, :, pl.ds(ks, tk)], preferred_element_type=F32)
                acc = acc + w[:, hd:hd + 1] * jnp.maximum(sc, 0.0)
        bits = lax.bitcast_convert_type(acc, I32)
        key = bits ^ ((bits >> 31) & 0x7FFFFFFF)
        key = jnp.where(acc == 0.0, 0, key)
        kpos = ks + lax.broadcasted_iota(I32, (tq, tk), 1)
        key_ref[:, pl.ds(ks, tk)] = jnp.where(kpos < qend, key, NEG_INF_KEY)
        return carry

    lax.fori_loop(0, n_kv, score_block, 0)

    def key_chunk(j, c):
        return key_ref[:, pl.ds(pl.multiple_of(j * tk + c * LANES, LANES), LANES)]

    def lane_total(x):
        return jnp.broadcast_to(jnp.sum(x, axis=-1, keepdims=True), x.shape)

    def count_ge(cand):
        def body(j, part):
            for c in range(tk // LANES):
                part = part + jnp.where(key_chunk(j, c) >= cand, 1, 0)
            return part
        return lane_total(lax.fori_loop(0, n_kv, body, jnp.zeros((tq, LANES), I32)))

    def count_ge3(c1, c2, c3):
        p1, p2, p3 = 1, 1 + (1 << 10), 1 + (1 << 10) + (1 << 20)

        def body(j, part):
            for c in range(tk // LANES):
                kb = key_chunk(j, c)
                part = part + jnp.where(kb >= c3, p3, jnp.where(kb >= c2, p2, jnp.where(kb >= c1, p1, 0)))
            return part
        part = lax.fori_loop(0, n_kv, body, jnp.zeros((tq, LANES), I32))
        return lane_total(part & 1023), lane_total((part >> 10) & 1023), lane_total(part >> 20)

    small = count_ge(jnp.full((tq, LANES), NEG_INF_KEY + 1, I32)) <= topk

    def pending(cnt):
        return jnp.max(jnp.where((cnt != topk) & jnp.logical_not(small), 1, 0)) > 0

    def digits_cond(carry):
        t, _, cnt = carry
        return (t < 16) & pending(cnt)

    def digits_body(carry):
        t, lo, cnt = carry
        for u in range(BISECT_UNROLL):
            step = jnp.left_shift(1, 30 - 2 * (t + u))
            c1 = lo + step
            c2 = c1 + step
            c3 = c2 + step
            n1, n2, n3 = count_ge3(c1, c2, c3)
            lo = jnp.where(n3 >= topk, c3, jnp.where(n2 >= topk, c2, jnp.where(n1 >= topk, c1, lo)))
            cnt = jnp.where(n3 >= topk, n3, jnp.where(n2 >= topk, n2, jnp.where(n1 >= topk, n1, cnt)))
        return t + BISECT_UNROLL, lo, cnt

    _, thr, n_ge = lax.while_loop(
        digits_cond, digits_body,
        (jnp.int32(0), jnp.full((tq, LANES), INT_MIN, I32), jnp.zeros((tq, LANES), I32) + n_kv * tk))
    tie_rows = (n_ge > topk) & jnp.logical_not(small)

    @pl.when(jnp.max(jnp.where(tie_rows, 1, 0)) > 0)
    def _drop_late_ties():
        need_f = (topk - count_ge(thr + 1)).astype(F32)
        upper = jnp.where(lax.broadcasted_iota(I32, (LANES, LANES), 0) <= lax.broadcasted_iota(I32, (LANES, LANES), 1),
                          1.0, 0.0).astype(BF16)

        def body(j, seen):
            for c in range(tk // LANES):
                kb = key_chunk(j, c)
                eq = kb == thr
                eq_f = jnp.where(eq, 1.0, 0.0)
                rank = seen + jnp.dot(eq_f.astype(BF16), upper, preferred_element_type=F32)
                drop = eq & (rank > need_f) & tie_rows
                key_ref[:, pl.ds(pl.multiple_of(j * tk + c * LANES, LANES), LANES)] = jnp.where(drop, thr - 1, kb)
                seen = seen + lane_total(eq_f)
            return seen

        lax.fori_loop(0, n_kv, body, jnp.zeros((tq, LANES), F32))

    thr_keep = jnp.where(small, NEG_INF_KEY + 1, jnp.maximum(thr, NEG_INF_KEY + 1))
    state = (s_ref, p_ref, m_ref, None, alpha_ref, acc_ref)
    _init_state(state)
    strip = min(tq, STRIP_ROWS)
    strips_per_head = tq // strip

    def attn_block(j, carry):
        ks = pl.multiple_of(j * tk, tk)
        for c in range(tk // LANES):
            bias_ref[:, c * LANES:(c + 1) * LANES] = jnp.where(key_chunk(j, c) >= thr_keep, 0.0, MASKED)
        cols =[slice(g * HEAD_DIM, (g + 1) * HEAD_DIM) for g in range(A_KV_HEADS)]
        _flash_step([q_ref[g * A_REP:(g + 1) * A_REP].reshape(A_REP * tq, HEAD_DIM) for g in range(A_KV_HEADS)],
                    [k_ref[pl.ds(ks, tk), c] for c in cols],
                    [_with_ones(v_ref[pl.ds(ks, tk), c]) for c in cols],
                    lambda c, r, ch: bias_ref[(r % strips_per_head) * strip:(r % strips_per_head + 1) * strip, ch],
                    state, rows=A_REP * tq, strip=strip)
        return carry

    lax.fori_loop(0, n_kv, attn_block, 0)
    for g in range(A_KV_HEADS):
        o = acc_ref[g, :, :HEAD_DIM] / acc_ref[g, :, HEAD_DIM:]
        for r in range(A_REP):
            hd = g * A_REP + r
            o_ref[:, hd * HEAD_DIM:(hd + 1) * HEAD_DIM] = o[r * tq:(r + 1) * tq].astype(BF16)


def _dsa_attention(cfg, q_hm, q_row0, qi, sm, k, v, kit, topk):
    tq, nq, s = cfg.tq, cfg.nq, cfg.s_pad
    qb0 = q_row0 // tq
    kern = functools.partial(_dsa_kernel, cfg=cfg, topk=topk)
    return pl.pallas_call(
        kern, grid=(cfg.batch, nq), name="dsa_attention",
        in_specs=[pl.BlockSpec((A_HEADS, tq, HEAD_DIM), lambda b, i: (0, qb0 + b * nq + i, 0)),
                  pl.BlockSpec((tq, IDX_HEADS * IDX_DIM), lambda b, i: (qb0 + b * nq + i, 0)),
                  pl.BlockSpec((tq, LANES), lambda b, i: (qb0 + b * nq + i, 0)),
                  pl.BlockSpec((s, A_KV_HEADS * HEAD_DIM), lambda b, i: (b, 0)),
                  pl.BlockSpec((s, A_KV_HEADS * HEAD_DIM), lambda b, i: (b, 0)),
                  pl.BlockSpec((None, 2, LANES, s), lambda b, i: (b, 0, 0, 0))],
        out_specs=pl.BlockSpec((tq, MIX_WIDTH), lambda b, i: (b * nq + i, 0)),
        out_shape=jax.ShapeDtypeStruct((cfg.batch * cfg.t_q, MIX_WIDTH), BF16),
        scratch_shapes=[pltpu.VMEM((tq, s), I32), pltpu.VMEM((tq, cfg.tk), F32)]
        + _flash_scratch(A_KV_HEADS, A_REP * tq, cfg.tk, 2 * HEAD_DIM, with_l=False),
        compiler_params=_params(("parallel", "arbitrary")))(q_hm, qi, sm, k, v, kit)


B_STEP_HEADS = 2


def _run_blocks(n_full, n_kv, step):
    def plain(j, carry):
        step(j, False)
        return carry

    def masked(j, carry):
        step(j, True)
        return carry

    lax.fori_loop(0, n_full, plain, 0)
    lax.fori_loop(n_full, n_kv, masked, 0)


def _diff_kernel(q_ref, k_ref, v_ref, lq1_ref, lk1_ref, lq2_ref, lk2_ref, sub_ref, o_ref,
                 bias_ref, s_ref, p_ref, m_ref, l_ref, alpha_ref, acc_ref, *, cfg, lam_init):
    tq, tk = cfg.tq, cfg.tk
    qend, n_full, n_kv = _chunk_limits(pl.program_id(2), cfg)
    state = (s_ref, p_ref, m_ref, l_ref, alpha_ref, acc_ref)
    _init_state(state)
    strip = min(tq, STRIP_ROWS)
    n_chains = 2 * B_STEP_HEADS

    def step(j, masked):
        ks = pl.multiple_of(j * tk, tk)
        if masked:
            bias_ref[...] = _visibility_bias(ks, qend, (tq, tk))
        qk_cols = [slice(c * HEAD_DIM, (c + 1) * HEAD_DIM) for c in range(n_chains)]
        v_cols = [slice((c // 2) * B_V_DIM, (c // 2 + 1) * B_V_DIM) for c in range(n_chains)]
        _flash_step([q_ref[:, c] for c in qk_cols],
                    [k_ref[pl.ds(ks, tk), c] for c in qk_cols],
                    [v_ref[pl.ds(ks, tk), c] for c in v_cols],
                    (lambda c, r, ch: bias_ref[r * strip:(r + 1) * strip, ch]) if masked
                    else (lambda c, r, ch: None),
                    state, rows=tq, strip=strip)

    _run_blocks(n_full, n_kv, step)
    lam = (jnp.exp(jnp.sum(lq1_ref[...] * lk1_ref[...], axis=-1, keepdims=True))
           - jnp.exp(jnp.sum(lq2_ref[...] * lk2_ref[...], axis=-1, keepdims=True)) + lam_init)
    def normalised(c):
        return acc_ref[c] / jnp.sum(l_ref[c], axis=-1, keepdims=True)

    for h in range(B_STEP_HEADS):
        o = normalised(2 * h) - lam * normalised(2 * h + 1)
        ms = jnp.mean(o * o, axis=-1, keepdims=True)
        o_ref[:, h * B_V_DIM:(h + 1) * B_V_DIM] = (
            o * lax.rsqrt(ms + EPS) * sub_ref[...] * (1.0 - lam_init)).astype(BF16)


def _diff_attention(cfg, q, q_row0, k, v, lam_vecs, subln, lam_init):
    tq, nq, s = cfg.tq, cfg.nq, cfg.s_pad
    qb0 = q_row0 // tq
    width = B_STEP_HEADS * B_V_DIM
    kern = functools.partial(_diff_kernel, cfg=cfg, lam_init=lam_init)
    vec = pl.BlockSpec((1, HEAD_DIM), lambda b, h, i: (0, 0))
    return pl.pallas_call(
        kern, grid=(cfg.batch, B_HEADS // B_STEP_HEADS, nq), name="diff_attention",
        in_specs=[pl.BlockSpec((tq, width), lambda b, h, i: (qb0 + b * nq + i, h)),
                  pl.BlockSpec((s, width), lambda b, h, i: (b, h)),
                  pl.BlockSpec((s, width), lambda b, h, i: (b, h)),
                  vec, vec, vec, vec,
                  pl.BlockSpec((1, B_V_DIM), lambda b, h, i: (0, 0))],
        out_specs=pl.BlockSpec((tq, width), lambda b, h, i: (b * nq + i, h)),
        out_shape=jax.ShapeDtypeStruct((cfg.batch * cfg.t_q, MIX_WIDTH), BF16),
        scratch_shapes=[pltpu.VMEM((tq, cfg.tk), F32)]
        + _flash_scratch(2 * B_STEP_HEADS, tq, cfg.tk, B_V_DIM, with_l=True),
        compiler_params=_params(("parallel", "parallel", "arbitrary")))(
            q, k, v, *[a.reshape(1, HEAD_DIM) for a in lam_vecs], subln.reshape(1, B_V_DIM))


C_STEP_HEADS = 4


def _fox_kernel(q_ref, k_ref, v_ref, cq_ref, ck_ref, o_ref,
                bias_ref, cqh_ref, s_ref, p_ref, m_ref, alpha_ref, acc_ref, *, cfg):
    tq, tk = cfg.tq, cfg.tk
    hg = pl.program_id(1)
    qend, n_full, n_kv = _causal_limits(pl.program_id(2), cfg)
    state = (s_ref, p_ref, m_ref, None, alpha_ref, acc_ref)
    _init_state(state)
    strip = min(tq, STRIP_ROWS)
    lane = lax.broadcasted_iota(I32, (tq, LANES), 1)
    for c in range(C_STEP_HEADS):
        head_lane = lane == hg * C_STEP_HEADS + c
        cqh_ref[c] = jnp.broadcast_to(
            jnp.sum(jnp.where(head_lane, cq_ref[...], 0.0), axis=-1, keepdims=True), (tq, LANES))

    def step(j, masked):
        ks = pl.multiple_of(j * tk, tk)
        if masked:
            bias_ref[...] = _visibility_bias(ks, qend, (tq, tk))

        def bias(c, r, ch):
            rs = slice(r * strip, (r + 1) * strip)
            b = cqh_ref[c, rs, :] - ck_ref[c:c + 1, pl.ds(pl.multiple_of(ks + ch.start, LANES), LANES)]
            return b + bias_ref[rs, ch] if masked else b

        cols = [slice(c * HEAD_DIM, (c + 1) * HEAD_DIM) for c in range(C_STEP_HEADS)]
        _flash_step([q_ref[:, c] for c in cols],
                    [k_ref[pl.ds(ks, tk), c] for c in cols],
                    [_with_ones(v_ref[pl.ds(ks, tk), c]) for c in cols],
                    bias, state, rows=tq, strip=strip)

    _run_blocks(n_full, n_kv, step)
    for c in range(C_STEP_HEADS):
        o_ref[:, c * HEAD_DIM:(c + 1) * HEAD_DIM] = (
            acc_ref[c, :, :HEAD_DIM] / acc_ref[c, :, HEAD_DIM:]).astype(BF16)


def _fox_attention(cfg, q, q_row0, k, v, cq, ck_rows):
    tq, nq, s = cfg.tq, cfg.nq, cfg.s_pad
    qb0 = q_row0 // tq
    width = C_STEP_HEADS * HEAD_DIM
    groups = C_HEADS // C_STEP_HEADS
    kern = functools.partial(_fox_kernel, cfg=cfg)
    return pl.pallas_call(
        kern, grid=(cfg.batch, groups, nq), name="fox_attention",
        in_specs=[pl.BlockSpec((tq, width), lambda b, h, i: (qb0 + b * nq + i, h)),
                  pl.BlockSpec((s, width), lambda b, h, i: (b, h)),
                  pl.BlockSpec((s, width), lambda b, h, i: (b, h)),
                  pl.BlockSpec((tq, LANES), lambda b, h, i: (qb0 + b * nq + i, 0)),
                  pl.BlockSpec((None, C_STEP_HEADS, s), lambda b, h, i: (b * groups + h, 0, 0))],
        out_specs=pl.BlockSpec((tq, width), lambda b, h, i: (b * nq + i, h)),
        out_shape=jax.ShapeDtypeStruct((cfg.batch * cfg.t_q, MIX_WIDTH), BF16),
        scratch_shapes=[pltpu.VMEM((tq, cfg.tk), F32), pltpu.VMEM((C_STEP_HEADS, tq, LANES), F32)]
        + _flash_scratch(C_STEP_HEADS, tq, cfg.tk, 2 * HEAD_DIM, with_l=False),
        compiler_params=_params(("parallel", "parallel", "arbitrary")))(q, k, v, cq, ck_rows)


def _cumsum_kernel(x_ref, o_ref, carry_ref, *, tb):
    @pl.when(pl.program_id(1) == 0)
    def _():
        carry_ref[...] = jnp.zeros(carry_ref.shape, F32)

    x = x_ref[...]
    hi = x.astype(BF16)
    r1 = x - hi.astype(F32)
    mid = r1.astype(BF16)
    low = (r1 - mid.astype(F32)).astype(BF16)
    tri = jnp.where(lax.broadcasted_iota(I32, (tb, tb), 0) >= lax.broadcasted_iota(I32, (tb, tb), 1),
                    1.0, 0.0).astype(BF16)
    c = (jnp.dot(tri, hi, preferred_element_type=F32) + jnp.dot(tri, mid, preferred_element_type=F32)
         + jnp.dot(tri, low, preferred_element_type=F32)) + carry_ref[...]
    o_ref[...] = c * LOG2E
    carry_ref[...] = c[tb - 1:tb, :]


def _cumsum_rows(x, batch, t, tb):
    nt = t // tb
    return pl.pallas_call(
        functools.partial(_cumsum_kernel, tb=tb), grid=(batch, nt), name="cumsum_rows",
        in_specs=[pl.BlockSpec((tb, LANES), lambda b, j: (b * nt + j, 0))],
        out_specs=pl.BlockSpec((tb, LANES), lambda b, j: (b * nt + j, 0)),
        out_shape=jax.ShapeDtypeStruct((batch * t, LANES), F32),
        scratch_shapes=[pltpu.VMEM((1, LANES), F32)],
        compiler_params=_params(("parallel", "arbitrary")))(x)


def _split_w(w_in, sizes):
    offs = np.cumsum((0,) + tuple(sizes))
    return [w_in[:, int(offs[i]):int(offs[i + 1])] for i in range(len(sizes))]


def _pad_rows(a, rows):
    return jnp.pad(a, ((0, 0), (0, rows - a.shape[1])) + ((0, 0),) * (a.ndim - 2))


def _sample_keys(cache, new, s_pad):
    b = cache.shape[0]
    full = jnp.concatenate([cache.reshape(b, cache.shape[1], -1), new.reshape(b, new.shape[1], -1)], axis=1)
    return _pad_rows(full, s_pad).astype(BF16).reshape(b * s_pad, -1)


def _key_transposed_pairs(ki, batch, s):
    kt = jnp.swapaxes(ki.reshape(batch, s, IDX_DIM), 1, 2).astype(BF16)
    z = jnp.zeros_like(kt)
    return jnp.stack([jnp.concatenate([kt, z], axis=1), jnp.concatenate([z, kt], axis=1)], axis=1)


def _layer_a(pr, sr, cfg_p, cfg_s, xp, xs, ck, cv, cki, norm, w_in, w_out, qn, kn, ikn):
    w_all = w_in.astype(BF16)
    wq, wk, wv, wqi = _Cols(w_all, 2048, 0), _Cols(w_all, 512, 4), _Cols(w_all, 512, 5), _Cols(w_all, 1024, 3)
    wsm = _Cols(w_all, LANES, sum(A_SIZES[:4]) // LANES)
    wg = _Cols(_split_w(w_in, A_SIZES)[6].astype(BF16), MIX_WIDTH)
    wo = w_out.astype(BF16)

    def project(rows, x):
        q = _proj_qk(rows, x, norm, wq, qn, scale=HEAD_DIM ** -0.5 * LOG2E, use_rope=True,
                     head_major=True, emit_f32=False)[0]
        k_f, k_b = _proj_qk(rows, x, norm, wk, kn, scale=1.0, use_rope=True, head_major=False, emit_f32=True)
        v_f, v_b = _proj_plain(rows, x, norm, wv)
        qi = _proj_rope64(rows, x, norm, wqi)
        sm = _proj_idx_small(rows, x, norm, wsm, ikn)
        sg = _proj_gate(rows, x, norm, wg)
        return q, k_f, k_b, v_f, v_b, qi, sm, sg

    q, k_f, k_b, v_f, v_b, qi, sm, sg = project(pr, xp)
    ki_f = sm[:, :IDX_DIM]
    o = _dsa_attention(cfg_p, q, 0, qi, sm, k_b, v_b,
                       _key_transposed_pairs(ki_f, pr.batch, pr.t_q), min(TOPK_MAX, pr.t_q // 4))
    yp = _out_proj(pr, xp, o, sg, wo)

    q2, k2_f, _, v2_f, _, qi2, sm2, sg2 = project(sr, xs)
    ki2_f = sm2[:, :IDX_DIM]
    b, s = sr.batch, sr.t_q
    k_all = _sample_keys(ck, k2_f.reshape(b, s, -1), cfg_s.s_pad)
    v_all = _sample_keys(cv, v2_f.reshape(b, s, -1), cfg_s.s_pad)
    ki_all = _pad_rows(jnp.concatenate([cki, ki2_f.reshape(b, s, IDX_DIM)], axis=1), cfg_s.s_pad)
    o2 = _dsa_attention(cfg_s, q2, 0, qi2, sm2, k_all, v_all,
                        _key_transposed_pairs(ki_all.reshape(-1, IDX_DIM), b, cfg_s.s_pad),
                        min(TOPK_MAX, cfg_s.s_valid // 4))
    ys = _out_proj(sr, xs, o2, sg2, wo)
    kv = (A_KV_HEADS, HEAD_DIM)
    state = (k_f.reshape(pr.batch, pr.t_q, *kv), v_f.reshape(pr.batch, pr.t_q, *kv),
             ki_f.reshape(pr.batch, pr.t_q, IDX_DIM),
             k2_f.reshape(b, s, *kv), v2_f.reshape(b, s, *kv), ki2_f.reshape(b, s, IDX_DIM))
    return yp, ys, state


def _layer_b(layer, pr, sr, cfg_p, cfg_s, xp, xs, ck, cv, norm, w_in, w_out, qn, kn,
             lq1, lk1, lq2, lk2, subln):
    lam_init = 0.8 - 0.6 * float(np.exp(-0.3 * layer))
    w_all = w_in.astype(BF16)
    wq, wk, wv, wg = [_Cols(w_all, 2048, j) for j in range(4)]
    wo = w_out.astype(BF16)

    def project(rows, x):
        q = _proj_qk(rows, x, norm, wq, qn, scale=HEAD_DIM ** -0.5 * LOG2E, use_rope=True,
                     head_major=False, emit_f32=False)[0]
        k_f, k_b = _proj_qk(rows, x, norm, wk, kn, scale=1.0, use_rope=True, head_major=False, emit_f32=True)
        v_f, v_b = _proj_plain(rows, x, norm, wv)
        return q, k_f, k_b, v_f, v_b, _proj_gate(rows, x, norm, wg)

    lam_vecs = (lq1, lk1, lq2, lk2)
    q, k_f, k_b, v_f, v_b, sg = project(pr, xp)
    o = _diff_attention(cfg_p, q, 0, k_b, v_b, lam_vecs, subln, lam_init)
    yp = _out_proj(pr, xp, o, sg, wo)

    q2, k2_f, _, v2_f, _, sg2 = project(sr, xs)
    b, s = sr.batch, sr.t_q
    k_all = _sample_keys(ck, k2_f.reshape(b, s, -1), cfg_s.s_pad)
    v_all = _sample_keys(cv, v2_f.reshape(b, s, -1), cfg_s.s_pad)
    o2 = _diff_attention(cfg_s, q2, 0, k_all, v_all, lam_vecs, subln, lam_init)
    ys = _out_proj(sr, xs, o2, sg2, wo)
    state = (k_f.reshape(pr.batch, pr.t_q, 2 * B_HEADS, HEAD_DIM), v_f.reshape(pr.batch, pr.t_q, B_HEADS, B_V_DIM),
             k2_f.reshape(b, s, 2 * B_HEADS, HEAD_DIM), v2_f.reshape(b, s, B_HEADS, B_V_DIM))
    return yp, ys, state


def _head_rows(c, batch, s):
    rows = jnp.swapaxes(c.reshape(batch, s, LANES)[:, :, :C_HEADS], 1, 2)
    return rows.reshape(batch * C_HEADS // C_STEP_HEADS, C_STEP_HEADS, s)


def _layer_c(pr, sr, cfg_p, cfg_s, xp, xs, ck, cv, clogf, norm, w_in, w_out, qn, kn, fb):
    w_all = w_in.astype(BF16)
    wq, wk, wv = [_Cols(w_all, 2048, j) for j in range(3)]
    wf = _Cols(w_all, LANES, sum(C_SIZES[:3]) // LANES)
    wg = _Cols(_split_w(w_in, C_SIZES)[4].astype(BF16), MIX_WIDTH)
    wo = w_out.astype(BF16)

    def project(rows, x):
        q = _proj_qk(rows, x, norm, wq, qn, scale=HEAD_DIM ** -0.5 * LOG2E, use_rope=False,
                     head_major=False, emit_f32=False)[0]
        k_f, k_b = _proj_qk(rows, x, norm, wk, kn, scale=1.0, use_rope=False, head_major=False, emit_f32=True)
        v_f, v_b = _proj_plain(rows, x, norm, wv)
        return q, k_f, k_b, v_f, v_b, _proj_logf(rows, x, norm, wf, fb), _proj_gate(rows, x, norm, wg)

    q, k_f, k_b, v_f, v_b, logf, sg = project(pr, xp)
    c = _cumsum_rows(logf, pr.batch, pr.t_q, 512)
    o = _fox_attention(cfg_p, q, 0, k_b, v_b, c, _head_rows(c, pr.batch, pr.t_q))
    yp = _out_proj(pr, xp, o, sg, wo)

    q2, k2_f, _, v2_f, _, logf2, sg2 = project(sr, xs)
    b, s = sr.batch, sr.t_q
    k_all = _sample_keys(ck, k2_f.reshape(b, s, -1), cfg_s.s_pad)
    v_all = _sample_keys(cv, v2_f.reshape(b, s, -1), cfg_s.s_pad)
    logf_all = jnp.concatenate([jnp.pad(clogf.astype(F32), ((0, 0), (0, 0), (0, LANES - C_HEADS))),
                                logf2.reshape(b, s, LANES)], axis=1)
    c2 = _cumsum_rows(_pad_rows(logf_all, cfg_s.s_pad).reshape(b * cfg_s.s_pad, LANES), b, cfg_s.s_pad, LANES)
    cq2 = c2.reshape(b, cfg_s.s_pad, LANES)[:, PAST_LEN:PAST_LEN + s].reshape(b * s, LANES)
    o2 = _fox_attention(cfg_s, q2, 0, k_all, v_all, cq2, _head_rows(c2, b, cfg_s.s_pad))
    ys = _out_proj(sr, xs, o2, sg2, wo)
    hd = (C_HEADS, HEAD_DIM)
    state = (k_f.reshape(pr.batch, pr.t_q, *hd), v_f.reshape(pr.batch, pr.t_q, *hd),
             logf[:, :C_HEADS].reshape(pr.batch, pr.t_q, C_HEADS),
             k2_f.reshape(b, s, *hd), v2_f.reshape(b, s, *hd), logf2[:, :C_HEADS].reshape(b, s, C_HEADS))
    return yp, ys, state


def kernel(x_prompt, x_sample, cache_l0_k, cache_l0_v, cache_l0_kidx, cache_l1_k, cache_l1_v, cache_l2_k, cache_l2_v, cache_l2_logf, cache_l3_k, cache_l3_v, cache_l3_kidx, l0_norm, l0_w_in, l0_w_out, l0_q_norm, l0_k_norm, l0_idx_k_norm, l1_norm, l1_w_in, l1_w_out, l1_q_norm, l1_k_norm, l1_lambda_q1, l1_lambda_k1, l1_lambda_q2, l1_lambda_k2, l1_subln, l2_norm, l2_w_in, l2_w_out, l2_q_norm, l2_k_norm, l2_forget_bias, l3_norm, l3_w_in, l3_w_out, l3_q_norm, l3_k_norm, l3_idx_k_norm):
    bp, tp, d = x_prompt.shape
    bs, ts, _ = x_sample.shape
    s_valid = PAST_LEN + ts
    s_pad = -(-s_valid // LANES) * LANES
    pr = _Rows(bp, tp, 0, 512)
    sr = _Rows(bs, ts, PAST_LEN, bs * ts)
    a_p = _Attn(bp, tp, tp, tp, 0, 128, 512)
    a_s = _Attn(bs, ts, s_pad, s_valid, PAST_LEN, ts, LANES)
    bc_p = _Attn(bp, tp, tp, tp, 0, 512, 512)
    bc_s = a_s

    xp = x_prompt.reshape(bp * tp, d)
    xs = x_sample.reshape(bs * ts, d)
    xp, xs, st0 = _layer_a(pr, sr, a_p, a_s, xp, xs, cache_l0_k, cache_l0_v, cache_l0_kidx,
                           l0_norm, l0_w_in, l0_w_out, l0_q_norm, l0_k_norm, l0_idx_k_norm)
    xp, xs, st1 = _layer_b(1, pr, sr, bc_p, bc_s, xp, xs, cache_l1_k, cache_l1_v,
                           l1_norm, l1_w_in, l1_w_out, l1_q_norm, l1_k_norm,
                           l1_lambda_q1, l1_lambda_k1, l1_lambda_q2, l1_lambda_k2, l1_subln)
    xp, xs, st2 = _layer_c(pr, sr, bc_p, bc_s, xp, xs, cache_l2_k, cache_l2_v, cache_l2_logf,
                           l2_norm, l2_w_in, l2_w_out, l2_q_norm, l2_k_norm, l2_forget_bias)
    xp, xs, st3 = _layer_a(pr, sr, a_p, a_s, xp, xs, cache_l3_k, cache_l3_v, cache_l3_kidx,
                           l3_norm, l3_w_in, l3_w_out, l3_q_norm, l3_k_norm, l3_idx_k_norm)
    return (xp.reshape(bp, tp, d), xs.reshape(bs, ts, d)) + st0 + st1 + st2 + st3
```

```python
import functools

import numpy as np
import jax
import jax.numpy as jnp
from jax import lax
from jax.experimental import pallas as pl
from jax.experimental.pallas import tpu as pltpu

F32 = jnp.float32
BF16 = jnp.bfloat16
I32 = jnp.int32

D_MODEL = 2048
PAST_LEN = 1024
CHUNK_SHIFT = 6
ROPE_THETA = 10000.0
EPS = 1e-6
HEAD_DIM = 128
A_HEADS = 16
A_KV_HEADS = 4
A_REP = A_HEADS // A_KV_HEADS
IDX_HEADS = 16
IDX_DIM = 64
TOPK_MAX = 256
B_HEADS = 8
B_V_DIM = 256
C_HEADS = 16
MIX_WIDTH = D_MODEL

A_SIZES = (2048, 512, 512, 1024, 64, 16, 2048)
B_SIZES = (2048, 2048, 2048, 2048)
C_SIZES = (2048, 2048, 2048, 16, 2048)

LANES = 128
MXU_N = 256
VMEM_LIMIT = 56 * 1024 * 1024
MASKED = -1e30
LOG2E = 1.4426950408889634
STRIP_ROWS = 32
BISECT_UNROLL = 3
INT_MIN = -2 ** 31
SCORE_ROWS = 128
NEG_INF_KEY = -2139095041


def _params(sem):
    return pltpu.CompilerParams(dimension_semantics=sem, vmem_limit_bytes=VMEM_LIMIT)


def _norm_rows(x_ref, g_ref):
    x = x_ref[...]
    ms = jnp.mean(x * x, axis=-1, keepdims=True)
    return (x * lax.rsqrt(ms + EPS) * g_ref[...]).astype(BF16)


def _rope128(y, cos, sin):
    return y * cos + pltpu.roll(y, 64, 1) * sin


def _rope64(y, cos, sin):
    lane = lax.broadcasted_iota(I32, y.shape, 1)
    rot = jnp.where((lane & 63) < 32, pltpu.roll(y, 96, 1), pltpu.roll(y, 32, 1))
    return y * cos + rot * sin


class _Cols:
    def __init__(self, array, n, block=0):
        assert array.shape[1] >= (block + 1) * n
        self.array, self.n, self.block = array, n, block


class _Rows:
    def __init__(self, batch, t_q, pos0, tm):
        self.batch, self.t_q, self.pos0, self.tm = batch, t_q, pos0, tm
        self.m = batch * t_q
        self.tab_blocks = max(t_q // tm, 1)

    def tab_spec(self):
        nb = self.tab_blocks
        return pl.BlockSpec((self.tm, LANES), lambda i: (i % nb, 0))

    def tables(self, d):
        period = max(self.t_q, self.tm)
        pos = self.pos0 + (jnp.arange(period) % self.t_q)
        half = d // 2
        inv = ROPE_THETA ** (-2.0 * jnp.arange(half, dtype=F32) / d)
        ang = pos.astype(F32)[:, None] * inv[None, :]
        cos, sin = jnp.cos(ang), jnp.sin(ang)
        reps = LANES // d
        cos_f = jnp.tile(jnp.concatenate([cos, cos], axis=-1), (1, reps))
        sin_f = jnp.tile(jnp.concatenate([-sin, sin], axis=-1), (1, reps))
        return cos_f, sin_f


def _row_spec(tm, n):
    return pl.BlockSpec((tm, n), lambda i: (i, 0))


def _vec_spec(n):
    return pl.BlockSpec((1, n), lambda i: (0, 0))


class _Seg:
    def __init__(self, emit, w, aux, aux_specs, out_shape, out_specs):
        self.emit, self.w = emit, w
        self.aux, self.aux_specs = list(aux), list(aux_specs)
        self.out_shape, self.out_specs = list(out_shape), list(out_specs)


def _emit_qk(h, w_ref, aux, outs, *, n_heads, scale, use_rope, head_major, emit_f32):
    cos_ref, sin_ref, hg_ref = aux
    hg = hg_ref[...]
    of_ref = outs[0] if emit_f32 else None
    ob_ref = outs[-1]
    for c in range(n_heads // 2):
        y = jnp.dot(h, w_ref[:, c * MXU_N:(c + 1) * MXU_N], preferred_element_type=F32)
        for j in range(2):
            hd = 2 * c + j
            yh = y[:, j * LANES:(j + 1) * LANES]
            ms = jnp.mean(yh * yh, axis=-1, keepdims=True)
            yr = yh * lax.rsqrt(ms + EPS) * hg
            if use_rope:
                yr = _rope128(yr, cos_ref[...], sin_ref[...])
            if emit_f32:
                of_ref[:, hd * LANES:(hd + 1) * LANES] = yr
            yb = (yr * scale).astype(BF16)
            if head_major:
                ob_ref[hd] = yb
            else:
                ob_ref[:, hd * LANES:(hd + 1) * LANES] = yb


def _emit_plain(h, w_ref, aux, outs, *, n_cols):
    of_ref, ob_ref = outs
    for c in range(n_cols // MXU_N):
        sl = slice(c * MXU_N, (c + 1) * MXU_N)
        y = jnp.dot(h, w_ref[:, sl], preferred_element_type=F32)
        of_ref[:, sl] = y
        ob_ref[:, sl] = y.astype(BF16)


def _emit_gate(h, w_ref, aux, outs, *, n_cols):
    for c in range(n_cols // MXU_N):
        sl = slice(c * MXU_N, (c + 1) * MXU_N)
        y = jnp.dot(h, w_ref[:, sl], preferred_element_type=F32)
        outs[0][:, sl] = (y * (1.0 / (1.0 + jnp.exp(-y)))).astype(BF16)


def _emit_rope64(h, w_ref, aux, outs, *, n_cols):
    cos, sin = aux[0][...], aux[1][...]
    for c in range(n_cols // MXU_N):
        y = jnp.dot(h, w_ref[:, c * MXU_N:(c + 1) * MXU_N], preferred_element_type=F32)
        for j in range(2):
            col = c * MXU_N + j * LANES
            outs[0][:, col:col + LANES] = _rope64(y[:, j * LANES:(j + 1) * LANES], cos, sin).astype(BF16)


def _emit_idx_small(h, w_ref, aux, outs):
    cos_ref, sin_ref, hg_ref = aux
    y = jnp.dot(h, w_ref[...], preferred_element_type=F32)
    is_key = lax.broadcasted_iota(I32, y.shape, 1) < IDX_DIM
    ms = jnp.sum(jnp.where(is_key, y * y, 0.0), axis=-1, keepdims=True) * (1.0 / IDX_DIM)
    kn = _rope64(y * lax.rsqrt(ms + EPS) * hg_ref[...], cos_ref[...], sin_ref[...])
    outs[0][...] = jnp.where(is_key, kn, y)


def _emit_logf(h, w_ref, aux, outs):
    z = jnp.dot(h, w_ref[...], preferred_element_type=F32) + aux[0][...]
    outs[0][...] = jnp.minimum(z, 0.0) - jnp.log1p(jnp.exp(-jnp.abs(z)))


def _proj_kernel(x_ref, g_ref, *refs, segs):
    h = _norm_rows(x_ref, g_ref)
    n_in = sum(1 + len(s.aux) for s in segs)
    ins, outs = refs[:n_in], refs[n_in:]
    i = o = 0
    for s in segs:
        s.emit(h, ins[i], ins[i + 1:i + 1 + len(s.aux)], outs[o:o + len(s.out_shape)])
        i += 1 + len(s.aux)
        o += len(s.out_shape)


def _project(rows, x, g, segs, name):
    m, d = x.shape
    tm = rows.tm
    in_specs = [pl.BlockSpec((tm, d), lambda i: (i, 0)), pl.BlockSpec((1, d), lambda i: (0, 0))]
    args = [x, g.reshape(1, d)]
    for s in segs:
        block = s.w.block
        in_specs.append(pl.BlockSpec((d, s.w.n), lambda i, block=block: (0, block), pipeline_mode=pl.Buffered(1)))
        in_specs += s.aux_specs
        args += [s.w.array] + s.aux
    out = pl.pallas_call(
        functools.partial(_proj_kernel, segs=segs), grid=(m // tm,), in_specs=in_specs, name=name,
        out_specs=[sp for s in segs for sp in s.out_specs],
        out_shape=[sh for s in segs for sh in s.out_shape],
        compiler_params=_params(("parallel",)))(*args)
    res, o = [], 0
    for s in segs:
        res.append(out[o:o + len(s.out_shape)])
        o += len(s.out_shape)
    return res


def _seg_qk(rows, w, head_gain, *, scale, use_rope, head_major, emit_f32):
    n = w.n
    n_heads = n // HEAD_DIM
    cos, sin = rows.tables(HEAD_DIM)
    out_shape, out_specs = [], []
    if emit_f32:
        out_shape.append(jax.ShapeDtypeStruct((rows.m, n), F32))
        out_specs.append(_row_spec(rows.tm, n))
    if head_major:
        out_shape.append(jax.ShapeDtypeStruct((n_heads, rows.m, HEAD_DIM), BF16))
        out_specs.append(pl.BlockSpec((n_heads, rows.tm, HEAD_DIM), lambda i: (0, i, 0)))
    else:
        out_shape.append(jax.ShapeDtypeStruct((rows.m, n), BF16))
        out_specs.append(_row_spec(rows.tm, n))
    emit = functools.partial(_emit_qk, n_heads=n_heads, scale=scale, use_rope=use_rope,
                             head_major=head_major, emit_f32=emit_f32)
    return _Seg(emit, w, (cos, sin, head_gain.reshape(1, HEAD_DIM)),
                (rows.tab_spec(), rows.tab_spec(), _vec_spec(HEAD_DIM)), out_shape, out_specs)


def _seg_plain(rows, w):
    n = w.n
    return _Seg(functools.partial(_emit_plain, n_cols=n), w, (), (),
                [jax.ShapeDtypeStruct((rows.m, n), F32), jax.ShapeDtypeStruct((rows.m, n), BF16)],
                [_row_spec(rows.tm, n), _row_spec(rows.tm, n)])


def _seg_gate(rows, w):
    n = w.n
    return _Seg(functools.partial(_emit_gate, n_cols=n), w, (), (),
                [jax.ShapeDtypeStruct((rows.m, n), BF16)], [_row_spec(rows.tm, n)])


def _seg_rope64(rows, w):
    n = w.n
    cos, sin = rows.tables(IDX_DIM)
    return _Seg(functools.partial(_emit_rope64, n_cols=n), w, (cos, sin), (rows.tab_spec(), rows.tab_spec()),
                [jax.ShapeDtypeStruct((rows.m, n), BF16)], [_row_spec(rows.tm, n)])


def _seg_idx_small(rows, w, key_gain):
    cos, sin = rows.tables(IDX_DIM)
    gain = jnp.concatenate([key_gain, jnp.zeros((LANES - IDX_DIM,), F32)]).reshape(1, LANES)
    return _Seg(_emit_idx_small, w, (cos, sin, gain), (rows.tab_spec(), rows.tab_spec(), _vec_spec(LANES)),
                [jax.ShapeDtypeStruct((rows.m, LANES), F32)], [_row_spec(rows.tm, LANES)])


def _seg_logf(rows, w, fb):
    fbp = jnp.concatenate([fb, jnp.zeros((LANES - C_HEADS,), F32)]).reshape(1, LANES)
    return _Seg(_emit_logf, w, (fbp,), (_vec_spec(LANES),),
                [jax.ShapeDtypeStruct((rows.m, LANES), F32)], [_row_spec(rows.tm, LANES)])


def _out_kernel(x_ref, o_ref, sg_ref, w_ref, y_ref, *, n_cols):
    a = o_ref[...] * sg_ref[...]
    for c in range(n_cols // MXU_N):
        sl = slice(c * MXU_N, (c + 1) * MXU_N)
        y_ref[:, sl] = x_ref[:, sl] + jnp.dot(a, w_ref[:, sl], preferred_element_type=F32)


def _out_proj(rows, x, o, sg, w):
    m, d = x.shape
    tm = rows.tm
    kern = functools.partial(_out_kernel, n_cols=d)
    return pl.pallas_call(
        kern, grid=(m // tm,), name="out_proj",
        in_specs=[_row_spec(tm, d), _row_spec(tm, d), _row_spec(tm, d),
                  pl.BlockSpec((d, d), lambda i: (0, 0))],
        out_specs=_row_spec(tm, d), out_shape=jax.ShapeDtypeStruct((m, d), F32),
        compiler_params=_params(("parallel",)))(x, o, sg, w)


class _Attn:
    def __init__(self, batch, t_q, s_pad, s_valid, q_off, tq, tk):
        self.batch, self.t_q, self.s_pad, self.s_valid = batch, t_q, s_pad, s_valid
        self.q_off, self.tq, self.tk = q_off, tq, tk
        self.nq = t_q // tq


def _chunk_limits(i, cfg):
    q0 = cfg.q_off + i * cfg.tq
    qpos = q0 + lax.broadcasted_iota(I32, (cfg.tq, 1), 0)
    qend = jnp.minimum(((qpos >> CHUNK_SHIFT) + 1) << CHUNK_SHIFT, cfg.s_valid)
    kmin = jnp.minimum(((q0 >> CHUNK_SHIFT) + 1) << CHUNK_SHIFT, cfg.s_valid)
    kend = jnp.minimum((((q0 + cfg.tq - 1) >> CHUNK_SHIFT) + 1) << CHUNK_SHIFT, cfg.s_valid)
    return qend, kmin // cfg.tk, (kend + cfg.tk - 1) // cfg.tk


def _causal_limits(i, cfg):
    q0 = cfg.q_off + i * cfg.tq
    qpos = q0 + lax.broadcasted_iota(I32, (cfg.tq, 1), 0)
    return qpos + 1, (q0 + 1) // cfg.tk, (q0 + cfg.tq + cfg.tk - 1) // cfg.tk


def _qk(q, k):
    return lax.dot_general(q, k, (((1,), (1,)), ((), ())), preferred_element_type=F32)


def _visibility_bias(ks, qend, shape):
    kpos = ks + lax.broadcasted_iota(I32, shape, 1)
    return jnp.where(kpos < qend, 0.0, MASKED)


def _flash_step(qs, ks, vs, bias_fn, state, *, rows, strip):
    s_ref, p_ref, m_ref, l_ref, alpha_ref, acc_ref = state
    n = len(qs)
    strips = [slice(r * strip, (r + 1) * strip) for r in range(rows // strip)]
    chunks = [slice(a, a + LANES) for a in range(0, s_ref.shape[-1], LANES)]
    for c in range(n):
        s_ref[c] = _qk(qs[c], ks[c])
    for c in range(n):
        for r, rs in enumerate(strips):
            mx = None
            for ch in chunks:
                s = s_ref[c, rs, ch]
                b = bias_fn(c, r, ch)
                if b is not None:
                    s = s + b
                    s_ref[c, rs, ch] = s
                mx = s if mx is None else jnp.maximum(mx, s)
            m_old = m_ref[c, rs, :]
            m_new = jnp.maximum(m_old, jnp.broadcast_to(jnp.max(mx, axis=-1, keepdims=True), mx.shape))
            alpha_ref[c, rs, :] = jnp.exp2(m_old - m_new)
            m_ref[c, rs, :] = m_new
        for rs in strips:
            m = m_ref[c, rs, :]
            psum = None
            for ch in chunks:
                p = jnp.exp2(s_ref[c, rs, ch] - m)
                if l_ref is not None:
                    psum = p if psum is None else psum + p
                p_ref[c, rs, ch] = p.astype(BF16)
            if l_ref is not None:
                l_ref[c, rs, :] = alpha_ref[c, rs, :] * l_ref[c, rs, :] + psum
    for c in range(n):
        alpha = jnp.concatenate([alpha_ref[c]] * (acc_ref.shape[-1] // LANES), axis=1)
        acc_ref[c] = alpha * acc_ref[c] + jnp.dot(p_ref[c], vs[c], preferred_element_type=F32)


def _init_state(state):
    _, _, m_ref, l_ref, _, acc_ref = state
    m_ref[...] = jnp.full(m_ref.shape, MASKED, F32)
    acc_ref[...] = jnp.zeros(acc_ref.shape, F32)
    if l_ref is not None:
        l_ref[...] = jnp.zeros(l_ref.shape, F32)


def _flash_scratch(chains, rows, tk, dv, with_l):
    stat = pltpu.VMEM((chains, rows, LANES), F32)
    return ([pltpu.VMEM((chains, rows, tk), F32), pltpu.VMEM((chains, rows, tk), BF16), stat]
            + ([stat] if with_l else []) + [stat, pltpu.VMEM((chains, rows, dv), F32)])


def _with_ones(v):
    return jnp.concatenate([v, jnp.ones(v.shape, v.dtype)], axis=1)


def _dsa_kernel(q_ref, qi_ref, sm_ref, k_ref, v_ref, kit_ref, o_ref,
                key_ref, bias_ref, stat_ref, s_ref, p_ref, m_ref, alpha_ref, acc_ref, *, cfg, topk):
    tq, tk = cfg.tq, cfg.tk
    i = pl.program_id(1)
    qend, _, n_kv = _chunk_limits(i, cfg)
    sub = min(tq, SCORE_ROWS)
    stat_ref[0] = jnp.full((tq, LANES), INT_MIN, I32)
    stat_ref[1] = jnp.zeros((tq, LANES), I32)

    def score_block(j, carry):
        ks = pl.multiple_of(j * tk, tk)
        for r0 in range(0, tq, sub):
            rs = slice(r0, r0 + sub)
            w = sm_ref[rs, IDX_DIM:IDX_DIM + IDX_HEADS] * (IDX_DIM ** -0.5 * IDX_HEADS ** -0.5)
            acc = jnp.zeros((sub, tk), F32)
            for pair in range(IDX_HEADS // 2):
                qp = qi_ref[rs, pair * LANES:(pair + 1) * LANES]
                for e in range(2):
                    hd = 2 * pair + e
                    sc = jnp.dot(qp, kit_ref[e, :, pl.ds(ks, tk)], preferred_element_type=F32)
                    acc = acc + w[:, hd:hd + 1] * jnp.maximum(sc, 0.0)
            bits = lax.bitcast_convert_type(acc, I32)
            key = bits ^ ((bits >> 31) & 0x7FFFFFFF)
            key = jnp.where(acc == 0.0, 0, key)
            kpos = ks + lax.broadcasted_iota(I32, (sub, tk), 1)
            key = jnp.where(kpos < qend[rs], key, NEG_INF_KEY)
            key_ref[rs, pl.ds(ks, tk)] = key
            kmax, nfin = stat_ref[0, rs, :], stat_ref[1, rs, :]
            for c in range(tk // LANES):
                kc = key[:, c * LANES:(c + 1) * LANES]
                kmax = jnp.maximum(kmax, kc)
                nfin = nfin + jnp.where(kc > NEG_INF_KEY, 1, 0)
            stat_ref[0, rs, :], stat_ref[1, rs, :] = kmax, nfin
        return carry

    lax.fori_loop(0, n_kv, score_block, 0)

    def key_chunk(j, c):
        return key_ref[:, pl.ds(pl.multiple_of(j * tk + c * LANES, LANES), LANES)]

    def lane_total(x):
        return jnp.broadcast_to(jnp.sum(x, axis=-1, keepdims=True), x.shape)

    def count_ge(cand):
        def body(j, part):
            ks = pl.multiple_of(j * tk, tk)
            ge = jnp.where(key_ref[:, pl.ds(ks, tk)] >= cand, 1, 0)
            for c in range(tk // LANES):
                part = part + ge[:, c * LANES:(c + 1) * LANES]
            return part
        part = lax.fori_loop(0, n_kv, body, jnp.zeros((tq, LANES), I32))
        return jnp.sum(part, axis=-1, keepdims=True)

    n_fin = jnp.sum(stat_ref[1], axis=-1, keepdims=True)
    small = n_fin <= topk

    def pending(lo, hi, cnt):
        open_row = (cnt != topk) & (hi != lo + 1) & jnp.logical_not(small)
        return jnp.max(jnp.where(open_row, 1, 0)) > 0

    def halve(carry):
        lo, hi, cnt = carry
        for _ in range(BISECT_UNROLL):
            mid = (lo >> 1) + (hi >> 1) + (lo & hi & 1)
            c = count_ge(mid)
            up = c >= topk
            lo, cnt = jnp.where(up, mid, lo), jnp.where(up, c, cnt)
            hi = jnp.where(up, hi, mid)
        return lo, hi, cnt

    thr, _, n_ge = lax.while_loop(
        lambda carry: pending(*carry), halve,
        (jnp.full((tq, 1), NEG_INF_KEY + 1, I32), jnp.max(stat_ref[0], axis=-1, keepdims=True) + 1, n_fin))
    thr, n_ge, n_fin = [jnp.broadcast_to(a, (tq, LANES)) for a in (thr, n_ge, n_fin)]
    small = n_fin <= topk
    tie_rows = (n_ge > topk) & jnp.logical_not(small)

    @pl.when(jnp.max(jnp.where(tie_rows, 1, 0)) > 0)
    def _drop_late_ties():
        need_f = jnp.broadcast_to((topk - count_ge(thr[:, :1] + 1)).astype(F32), (tq, LANES))
        upper = jnp.where(lax.broadcasted_iota(I32, (LANES, LANES), 0) <= lax.broadcasted_iota(I32, (LANES, LANES), 1),
                          1.0, 0.0).astype(BF16)

        def body(j, seen):
            for c in range(tk // LANES):
                kb = key_chunk(j, c)
                eq = kb == thr
                eq_f = jnp.where(eq, 1.0, 0.0)
                rank = seen + jnp.dot(eq_f.astype(BF16), upper, preferred_element_type=F32)
                drop = eq & (rank > need_f) & tie_rows
                key_ref[:, pl.ds(pl.multiple_of(j * tk + c * LANES, LANES), LANES)] = jnp.where(drop, thr - 1, kb)
                seen = seen + lane_total(eq_f)
            return seen

        lax.fori_loop(0, n_kv, body, jnp.zeros((tq, LANES), F32))

    thr_keep = jnp.where(small, NEG_INF_KEY + 1, jnp.maximum(thr, NEG_INF_KEY + 1))
    state = (s_ref, p_ref, m_ref, None, alpha_ref, acc_ref)
    _init_state(state)
    strip = min(tq, STRIP_ROWS)
    strips_per_head = tq // strip

    def attn_block(j, carry):
        ks = pl.multiple_of(j * tk, tk)
        for c in range(tk // LANES):
            bias_ref[:, c * LANES:(c + 1) * LANES] = jnp.where(key_chunk(j, c) >= thr_keep, 0.0, MASKED)
        cols =[slice(g * HEAD_DIM, (g + 1) * HEAD_DIM) for g in range(A_KV_HEADS)]
        _flash_step([q_ref[g * A_REP:(g + 1) * A_REP].reshape(A_REP * tq, HEAD_DIM) for g in range(A_KV_HEADS)],
                    [k_ref[pl.ds(ks, tk), c] for c in cols],
                    [_with_ones(v_ref[pl.ds(ks, tk), c]) for c in cols],
                    lambda c, r, ch: bias_ref[(r % strips_per_head) * strip:(r % strips_per_head + 1) * strip, ch],
                    state, rows=A_REP * tq, strip=strip)
        return carry

    lax.fori_loop(0, n_kv, attn_block, 0)
    for g in range(A_KV_HEADS):
        o = acc_ref[g, :, :HEAD_DIM] / acc_ref[g, :, HEAD_DIM:]
        for r in range(A_REP):
            hd = g * A_REP + r
            o_ref[:, hd * HEAD_DIM:(hd + 1) * HEAD_DIM] = o[r * tq:(r + 1) * tq].astype(BF16)


def _dsa_attention(cfg, q_hm, q_row0, qi, sm, k, v, kit, topk):
    tq, nq, s = cfg.tq, cfg.nq, cfg.s_pad
    qb0 = q_row0 // tq
    kern = functools.partial(_dsa_kernel, cfg=cfg, topk=topk)
    return pl.pallas_call(
        kern, grid=(cfg.batch, nq), name="dsa_attention",
        in_specs=[pl.BlockSpec((A_HEADS, tq, HEAD_DIM), lambda b, i: (0, qb0 + b * nq + i, 0)),
                  pl.BlockSpec((tq, IDX_HEADS * IDX_DIM), lambda b, i: (qb0 + b * nq + i, 0)),
                  pl.BlockSpec((tq, LANES), lambda b, i: (qb0 + b * nq + i, 0)),
                  pl.BlockSpec((s, A_KV_HEADS * HEAD_DIM), lambda b, i: (b, 0), pipeline_mode=pl.Buffered(1)),
                  pl.BlockSpec((s, A_KV_HEADS * HEAD_DIM), lambda b, i: (b, 0), pipeline_mode=pl.Buffered(1)),
                  pl.BlockSpec((None, 2, LANES, s), lambda b, i: (b, 0, 0, 0), pipeline_mode=pl.Buffered(1))],
        out_specs=pl.BlockSpec((tq, MIX_WIDTH), lambda b, i: (b * nq + i, 0)),
        out_shape=jax.ShapeDtypeStruct((cfg.batch * cfg.t_q, MIX_WIDTH), BF16),
        scratch_shapes=[pltpu.VMEM((tq, s), I32), pltpu.VMEM((tq, cfg.tk), F32), pltpu.VMEM((2, tq, LANES), I32)]
        + _flash_scratch(A_KV_HEADS, A_REP * tq, cfg.tk, 2 * HEAD_DIM, with_l=False),
        compiler_params=_params(("parallel", "arbitrary")))(q_hm, qi, sm, k, v, kit)


B_STEP_HEADS = 2


def _run_blocks(n_full, n_kv, step):
    def plain(j, carry):
        step(j, False)
        return carry

    def masked(j, carry):
        step(j, True)
        return carry

    lax.fori_loop(0, n_full, plain, 0)
    lax.fori_loop(n_full, n_kv, masked, 0)


def _diff_kernel(q_ref, k_ref, v_ref, lq1_ref, lk1_ref, lq2_ref, lk2_ref, sub_ref, o_ref,
                 bias_ref, s_ref, p_ref, m_ref, l_ref, alpha_ref, acc_ref, *, cfg, lam_init):
    tq, tk = cfg.tq, cfg.tk
    qend, n_full, n_kv = _chunk_limits(pl.program_id(2), cfg)
    state = (s_ref, p_ref, m_ref, l_ref, alpha_ref, acc_ref)
    _init_state(state)
    strip = min(tq, STRIP_ROWS)
    n_chains = 2 * B_STEP_HEADS

    def step(j, masked):
        ks = pl.multiple_of(j * tk, tk)
        if masked:
            bias_ref[...] = _visibility_bias(ks, qend, (tq, tk))
        qk_cols = [slice(c * HEAD_DIM, (c + 1) * HEAD_DIM) for c in range(n_chains)]
        v_cols = [slice((c // 2) * B_V_DIM, (c // 2 + 1) * B_V_DIM) for c in range(n_chains)]
        _flash_step([q_ref[:, c] for c in qk_cols],
                    [k_ref[pl.ds(ks, tk), c] for c in qk_cols],
                    [v_ref[pl.ds(ks, tk), c] for c in v_cols],
                    (lambda c, r, ch: bias_ref[r * strip:(r + 1) * strip, ch]) if masked
                    else (lambda c, r, ch: None),
                    state, rows=tq, strip=strip)

    _run_blocks(n_full, n_kv, step)
    lam = (jnp.exp(jnp.sum(lq1_ref[...] * lk1_ref[...], axis=-1, keepdims=True))
           - jnp.exp(jnp.sum(lq2_ref[...] * lk2_ref[...], axis=-1, keepdims=True)) + lam_init)
    def normalised(c):
        return acc_ref[c] / jnp.sum(l_ref[c], axis=-1, keepdims=True)

    for h in range(B_STEP_HEADS):
        o = normalised(2 * h) - lam * normalised(2 * h + 1)
        ms = jnp.mean(o * o, axis=-1, keepdims=True)
        o_ref[:, h * B_V_DIM:(h + 1) * B_V_DIM] = (
            o * lax.rsqrt(ms + EPS) * sub_ref[...] * (1.0 - lam_init)).astype(BF16)


def _diff_attention(cfg, q, q_row0, k, v, lam_vecs, subln, lam_init):
    tq, nq, s = cfg.tq, cfg.nq, cfg.s_pad
    qb0 = q_row0 // tq
    width = B_STEP_HEADS * B_V_DIM
    kern = functools.partial(_diff_kernel, cfg=cfg, lam_init=lam_init)
    vec = pl.BlockSpec((1, HEAD_DIM), lambda b, h, i: (0, 0))
    return pl.pallas_call(
        kern, grid=(cfg.batch, B_HEADS // B_STEP_HEADS, nq), name="diff_attention",
        in_specs=[pl.BlockSpec((tq, width), lambda b, h, i: (qb0 + b * nq + i, h)),
                  pl.BlockSpec((s, width), lambda b, h, i: (b, h)),
                  pl.BlockSpec((s, width), lambda b, h, i: (b, h)),
                  vec, vec, vec, vec,
                  pl.BlockSpec((1, B_V_DIM), lambda b, h, i: (0, 0))],
        out_specs=pl.BlockSpec((tq, width), lambda b, h, i: (b * nq + i, h)),
        out_shape=jax.ShapeDtypeStruct((cfg.batch * cfg.t_q, MIX_WIDTH), BF16),
        scratch_shapes=[pltpu.VMEM((tq, cfg.tk), F32)]
        + _flash_scratch(2 * B_STEP_HEADS, tq, cfg.tk, B_V_DIM, with_l=True),
        compiler_params=_params(("parallel", "parallel", "arbitrary")))(
            q, k, v, *[a.reshape(1, HEAD_DIM) for a in lam_vecs], subln.reshape(1, B_V_DIM))


C_STEP_HEADS = 4


def _fox_kernel(q_ref, k_ref, v_ref, cq_ref, ck_ref, o_ref,
                bias_ref, cqh_ref, s_ref, p_ref, m_ref, alpha_ref, acc_ref, *, cfg):
    tq, tk = cfg.tq, cfg.tk
    hg = pl.program_id(1)
    qend, n_full, n_kv = _causal_limits(pl.program_id(2), cfg)
    state = (s_ref, p_ref, m_ref, None, alpha_ref, acc_ref)
    _init_state(state)
    strip = min(tq, STRIP_ROWS)
    lane = lax.broadcasted_iota(I32, (tq, LANES), 1)
    for c in range(C_STEP_HEADS):
        head_lane = lane == hg * C_STEP_HEADS + c
        cqh_ref[c] = jnp.broadcast_to(
            jnp.sum(jnp.where(head_lane, cq_ref[...], 0.0), axis=-1, keepdims=True), (tq, LANES))

    def step(j, masked):
        ks = pl.multiple_of(j * tk, tk)
        if masked:
            bias_ref[...] = _visibility_bias(ks, qend, (tq, tk))

        def bias(c, r, ch):
            rs = slice(r * strip, (r + 1) * strip)
            b = cqh_ref[c, rs, :] - ck_ref[c:c + 1, pl.ds(pl.multiple_of(ks + ch.start, LANES), LANES)]
            return b + bias_ref[rs, ch] if masked else b

        cols = [slice(c * HEAD_DIM, (c + 1) * HEAD_DIM) for c in range(C_STEP_HEADS)]
        _flash_step([q_ref[:, c] for c in cols],
                    [k_ref[pl.ds(ks, tk), c] for c in cols],
                    [_with_ones(v_ref[pl.ds(ks, tk), c]) for c in cols],
                    bias, state, rows=tq, strip=strip)

    _run_blocks(n_full, n_kv, step)
    for c in range(C_STEP_HEADS):
        o_ref[:, c * HEAD_DIM:(c + 1) * HEAD_DIM] = (
            acc_ref[c, :, :HEAD_DIM] / acc_ref[c, :, HEAD_DIM:]).astype(BF16)


def _fox_attention(cfg, q, q_row0, k, v, cq, ck_rows):
    tq, nq, s = cfg.tq, cfg.nq, cfg.s_pad
    qb0 = q_row0 // tq
    width = C_STEP_HEADS * HEAD_DIM
    groups = C_HEADS // C_STEP_HEADS
    kern = functools.partial(_fox_kernel, cfg=cfg)
    return pl.pallas_call(
        kern, grid=(cfg.batch, groups, nq), name="fox_attention",
        in_specs=[pl.BlockSpec((tq, width), lambda b, h, i: (qb0 + b * nq + i, h)),
                  pl.BlockSpec((s, width), lambda b, h, i: (b, h)),
                  pl.BlockSpec((s, width), lambda b, h, i: (b, h)),
                  pl.BlockSpec((tq, LANES), lambda b, h, i: (qb0 + b * nq + i, 0)),
                  pl.BlockSpec((None, C_STEP_HEADS, s), lambda b, h, i: (b * groups + h, 0, 0))],
        out_specs=pl.BlockSpec((tq, width), lambda b, h, i: (b * nq + i, h)),
        out_shape=jax.ShapeDtypeStruct((cfg.batch * cfg.t_q, MIX_WIDTH), BF16),
        scratch_shapes=[pltpu.VMEM((tq, cfg.tk), F32), pltpu.VMEM((C_STEP_HEADS, tq, LANES), F32)]
        + _flash_scratch(C_STEP_HEADS, tq, cfg.tk, 2 * HEAD_DIM, with_l=False),
        compiler_params=_params(("parallel", "parallel", "arbitrary")))(q, k, v, cq, ck_rows)


def _cumsum_kernel(x_ref, o_ref, carry_ref, *, tb):
    @pl.when(pl.program_id(1) == 0)
    def _():
        carry_ref[...] = jnp.zeros(carry_ref.shape, F32)

    x = x_ref[...]
    hi = x.astype(BF16)
    r1 = x - hi.astype(F32)
    mid = r1.astype(BF16)
    low = (r1 - mid.astype(F32)).astype(BF16)
    tri = jnp.where(lax.broadcasted_iota(I32, (tb, tb), 0) >= lax.broadcasted_iota(I32, (tb, tb), 1),
                    1.0, 0.0).astype(BF16)
    c = (jnp.dot(tri, hi, preferred_element_type=F32) + jnp.dot(tri, mid, preferred_element_type=F32)
         + jnp.dot(tri, low, preferred_element_type=F32)) + carry_ref[...]
    o_ref[...] = c * LOG2E
    carry_ref[...] = c[tb - 1:tb, :]


def _cumsum_rows(x, batch, t, tb):
    nt = t // tb
    return pl.pallas_call(
        functools.partial(_cumsum_kernel, tb=tb), grid=(batch, nt), name="cumsum_rows",
        in_specs=[pl.BlockSpec((tb, LANES), lambda b, j: (b * nt + j, 0))],
        out_specs=pl.BlockSpec((tb, LANES), lambda b, j: (b * nt + j, 0)),
        out_shape=jax.ShapeDtypeStruct((batch * t, LANES), F32),
        scratch_shapes=[pltpu.VMEM((1, LANES), F32)],
        compiler_params=_params(("parallel", "arbitrary")))(x)


def _split_w(w_in, sizes):
    offs = np.cumsum((0,) + tuple(sizes))
    return [w_in[:, int(offs[i]):int(offs[i + 1])] for i in range(len(sizes))]


def _pad_rows(a, rows):
    return jnp.pad(a, ((0, 0), (0, rows - a.shape[1])) + ((0, 0),) * (a.ndim - 2))


def _sample_keys(cache, new, s_pad):
    b = cache.shape[0]
    full = jnp.concatenate([cache.reshape(b, cache.shape[1], -1), new.reshape(b, new.shape[1], -1)], axis=1)
    return _pad_rows(full, s_pad).astype(BF16).reshape(b * s_pad, -1)


def _key_transposed_pairs(ki, batch, s):
    kt = jnp.swapaxes(ki.reshape(batch, s, IDX_DIM), 1, 2).astype(BF16)
    z = jnp.zeros_like(kt)
    return jnp.stack([jnp.concatenate([kt, z], axis=1), jnp.concatenate([z, kt], axis=1)], axis=1)


def _layer_a(pr, sr, cfg_p, cfg_s, xp, xs, ck, cv, cki, norm, w_in, w_out, qn, kn, ikn):
    w_all = w_in.astype(BF16)
    wq, wk, wv, wqi = _Cols(w_all, 2048, 0), _Cols(w_all, 512, 4), _Cols(w_all, 512, 5), _Cols(w_all, 1024, 3)
    wsm = _Cols(w_all, LANES, sum(A_SIZES[:4]) // LANES)
    wg = _Cols(_split_w(w_in, A_SIZES)[6].astype(BF16), MIX_WIDTH)
    wo = w_out.astype(BF16)

    def project(rows, x):
        (q,), (qi,), (sm,) = _project(rows, x, norm, [
            _seg_qk(rows, wq, qn, scale=HEAD_DIM ** -0.5 * LOG2E, use_rope=True, head_major=True, emit_f32=False),
            _seg_rope64(rows, wqi), _seg_idx_small(rows, wsm, ikn)], "proj_a_query")
        (k_f, k_b), (v_f, v_b), (sg,) = _project(rows, x, norm, [
            _seg_qk(rows, wk, kn, scale=1.0, use_rope=True, head_major=False, emit_f32=True),
            _seg_plain(rows, wv), _seg_gate(rows, wg)], "proj_a_kv_gate")
        return q, k_f, k_b, v_f, v_b, qi, sm, sg

    q, k_f, k_b, v_f, v_b, qi, sm, sg = project(pr, xp)
    ki_f = sm[:, :IDX_DIM]
    o = _dsa_attention(cfg_p, q, 0, qi, sm, k_b, v_b,
                       _key_transposed_pairs(ki_f, pr.batch, pr.t_q), min(TOPK_MAX, pr.t_q // 4))
    yp = _out_proj(pr, xp, o, sg, wo)

    q2, k2_f, _, v2_f, _, qi2, sm2, sg2 = project(sr, xs)
    ki2_f = sm2[:, :IDX_DIM]
    b, s = sr.batch, sr.t_q
    k_all = _sample_keys(ck, k2_f.reshape(b, s, -1), cfg_s.s_pad)
    v_all = _sample_keys(cv, v2_f.reshape(b, s, -1), cfg_s.s_pad)
    ki_all = _pad_rows(jnp.concatenate([cki, ki2_f.reshape(b, s, IDX_DIM)], axis=1), cfg_s.s_pad)
    o2 = _dsa_attention(cfg_s, q2, 0, qi2, sm2, k_all, v_all,
                        _key_transposed_pairs(ki_all.reshape(-1, IDX_DIM), b, cfg_s.s_pad),
                        min(TOPK_MAX, cfg_s.s_valid // 4))
    ys = _out_proj(sr, xs, o2, sg2, wo)
    kv = (A_KV_HEADS, HEAD_DIM)
    state = (k_f.reshape(pr.batch, pr.t_q, *kv), v_f.reshape(pr.batch, pr.t_q, *kv),
             ki_f.reshape(pr.batch, pr.t_q, IDX_DIM),
             k2_f.reshape(b, s, *kv), v2_f.reshape(b, s, *kv), ki2_f.reshape(b, s, IDX_DIM))
    return yp, ys, state


def _layer_b(layer, pr, sr, cfg_p, cfg_s, xp, xs, ck, cv, norm, w_in, w_out, qn, kn,
             lq1, lk1, lq2, lk2, subln):
    lam_init = 0.8 - 0.6 * float(np.exp(-0.3 * layer))
    w_all = w_in.astype(BF16)
    wq, wk, wv, wg = [_Cols(w_all, 2048, j) for j in range(4)]
    wo = w_out.astype(BF16)

    def project(rows, x):
        (q,), (sg,) = _project(rows, x, norm, [
            _seg_qk(rows, wq, qn, scale=HEAD_DIM ** -0.5 * LOG2E, use_rope=True, head_major=False, emit_f32=False),
            _seg_gate(rows, wg)], "proj_b_query_gate")
        (k_f, k_b), (v_f, v_b) = _project(rows, x, norm, [
            _seg_qk(rows, wk, kn, scale=1.0, use_rope=True, head_major=False, emit_f32=True),
            _seg_plain(rows, wv)], "proj_b_kv")
        return q, k_f, k_b, v_f, v_b, sg

    lam_vecs = (lq1, lk1, lq2, lk2)
    q, k_f, k_b, v_f, v_b, sg = project(pr, xp)
    o = _diff_attention(cfg_p, q, 0, k_b, v_b, lam_vecs, subln, lam_init)
    yp = _out_proj(pr, xp, o, sg, wo)

    q2, k2_f, _, v2_f, _, sg2 = project(sr, xs)
    b, s = sr.batch, sr.t_q
    k_all = _sample_keys(ck, k2_f.reshape(b, s, -1), cfg_s.s_pad)
    v_all = _sample_keys(cv, v2_f.reshape(b, s, -1), cfg_s.s_pad)
    o2 = _diff_attention(cfg_s, q2, 0, k_all, v_all, lam_vecs, subln, lam_init)
    ys = _out_proj(sr, xs, o2, sg2, wo)
    state = (k_f.reshape(pr.batch, pr.t_q, 2 * B_HEADS, HEAD_DIM), v_f.reshape(pr.batch, pr.t_q, B_HEADS, B_V_DIM),
             k2_f.reshape(b, s, 2 * B_HEADS, HEAD_DIM), v2_f.reshape(b, s, B_HEADS, B_V_DIM))
    return yp, ys, state


def _head_rows(c, batch, s):
    rows = jnp.swapaxes(c.reshape(batch, s, LANES)[:, :, :C_HEADS], 1, 2)
    return rows.reshape(batch * C_HEADS // C_STEP_HEADS, C_STEP_HEADS, s)


def _layer_c(pr, sr, cfg_p, cfg_s, xp, xs, ck, cv, clogf, norm, w_in, w_out, qn, kn, fb):
    w_all = w_in.astype(BF16)
    wq, wk, wv = [_Cols(w_all, 2048, j) for j in range(3)]
    wf = _Cols(w_all, LANES, sum(C_SIZES[:3]) // LANES)
    wg = _Cols(_split_w(w_in, C_SIZES)[4].astype(BF16), MIX_WIDTH)
    wo = w_out.astype(BF16)

    def project(rows, x):
        (q,), (logf,), (sg,) = _project(rows, x, norm, [
            _seg_qk(rows, wq, qn, scale=HEAD_DIM ** -0.5 * LOG2E, use_rope=False, head_major=False, emit_f32=False),
            _seg_logf(rows, wf, fb), _seg_gate(rows, wg)], "proj_c_query_gate")
        (k_f, k_b), (v_f, v_b) = _project(rows, x, norm, [
            _seg_qk(rows, wk, kn, scale=1.0, use_rope=False, head_major=False, emit_f32=True),
            _seg_plain(rows, wv)], "proj_c_kv")
        return q, k_f, k_b, v_f, v_b, logf, sg

    q, k_f, k_b, v_f, v_b, logf, sg = project(pr, xp)
    c = _cumsum_rows(logf, pr.batch, pr.t_q, 512)
    o = _fox_attention(cfg_p, q, 0, k_b, v_b, c, _head_rows(c, pr.batch, pr.t_q))
    yp = _out_proj(pr, xp, o, sg, wo)

    q2, k2_f, _, v2_f, _, logf2, sg2 = project(sr, xs)
    b, s = sr.batch, sr.t_q
    k_all = _sample_keys(ck, k2_f.reshape(b, s, -1), cfg_s.s_pad)
    v_all = _sample_keys(cv, v2_f.reshape(b, s, -1), cfg_s.s_pad)
    logf_all = jnp.concatenate([jnp.pad(clogf.astype(F32), ((0, 0), (0, 0), (0, LANES - C_HEADS))),
                                logf2.reshape(b, s, LANES)], axis=1)
    c2 = _cumsum_rows(_pad_rows(logf_all, cfg_s.s_pad).reshape(b * cfg_s.s_pad, LANES), b, cfg_s.s_pad, LANES)
    cq2 = c2.reshape(b, cfg_s.s_pad, LANES)[:, PAST_LEN:PAST_LEN + s].reshape(b * s, LANES)
    o2 = _fox_attention(cfg_s, q2, 0, k_all, v_all, cq2, _head_rows(c2, b, cfg_s.s_pad))
    ys = _out_proj(sr, xs, o2, sg2, wo)
    hd = (C_HEADS, HEAD_DIM)
    state = (k_f.reshape(pr.batch, pr.t_q, *hd), v_f.reshape(pr.batch, pr.t_q, *hd),
             logf[:, :C_HEADS].reshape(pr.batch, pr.t_q, C_HEADS),
             k2_f.reshape(b, s, *hd), v2_f.reshape(b, s, *hd), logf2[:, :C_HEADS].reshape(b, s, C_HEADS))
    return yp, ys, state


def kernel(x_prompt, x_sample, cache_l0_k, cache_l0_v, cache_l0_kidx, cache_l1_k, cache_l1_v, cache_l2_k, cache_l2_v, cache_l2_logf, cache_l3_k, cache_l3_v, cache_l3_kidx, l0_norm, l0_w_in, l0_w_out, l0_q_norm, l0_k_norm, l0_idx_k_norm, l1_norm, l1_w_in, l1_w_out, l1_q_norm, l1_k_norm, l1_lambda_q1, l1_lambda_k1, l1_lambda_q2, l1_lambda_k2, l1_subln, l2_norm, l2_w_in, l2_w_out, l2_q_norm, l2_k_norm, l2_forget_bias, l3_norm, l3_w_in, l3_w_out, l3_q_norm, l3_k_norm, l3_idx_k_norm):
    bp, tp, d = x_prompt.shape
    bs, ts, _ = x_sample.shape
    s_valid = PAST_LEN + ts
    s_pad = -(-s_valid // LANES) * LANES
    pr = _Rows(bp, tp, 0, 512)
    sr = _Rows(bs, ts, PAST_LEN, bs * ts)
    a_p = _Attn(bp, tp, tp, tp, 0, 128, 512)
    a_s = _Attn(bs, ts, s_pad, s_valid, PAST_LEN, ts, LANES)
    bc_p = _Attn(bp, tp, tp, tp, 0, 512, 512)
    bc_s = a_s

    xp = x_prompt.reshape(bp * tp, d)
    xs = x_sample.reshape(bs * ts, d)
    xp, xs, st0 = _layer_a(pr, sr, a_p, a_s, xp, xs, cache_l0_k, cache_l0_v, cache_l0_kidx,
                           l0_norm, l0_w_in, l0_w_out, l0_q_norm, l0_k_norm, l0_idx_k_norm)
    xp, xs, st1 = _layer_b(1, pr, sr, bc_p, bc_s, xp, xs, cache_l1_k, cache_l1_v,
                           l1_norm, l1_w_in, l1_w_out, l1_q_norm, l1_k_norm,
                           l1_lambda_q1, l1_lambda_k1, l1_lambda_q2, l1_lambda_k2, l1_subln)
    xp, xs, st2 = _layer_c(pr, sr, bc_p, bc_s, xp, xs, cache_l2_k, cache_l2_v, cache_l2_logf,
                           l2_norm, l2_w_in, l2_w_out, l2_q_norm, l2_k_norm, l2_forget_bias)
    xp, xs, st3 = _layer_a(pr, sr, a_p, a_s, xp, xs, cache_l3_k, cache_l3_v, cache_l3_kidx,
                           l3_norm, l3_w_in, l3_w_out, l3_q_norm, l3_k_norm, l3_idx_k_norm)
    return (xp.reshape(bp, tp, d), xs.reshape(bs, ts, d)) + st0 + st1 + st2 + st3
```

```python
import functools

import numpy as np
import jax
import jax.numpy as jnp
from jax import lax
from jax.experimental import pallas as pl
from jax.experimental.pallas import tpu as pltpu

F32 = jnp.float32
BF16 = jnp.bfloat16
I32 = jnp.int32

D_MODEL = 2048
PAST_LEN = 1024
CHUNK_SHIFT = 6
ROPE_THETA = 10000.0
EPS = 1e-6
HEAD_DIM = 128
A_HEADS = 16
A_KV_HEADS = 4
A_REP = A_HEADS // A_KV_HEADS
IDX_HEADS = 16
IDX_DIM = 64
TOPK_MAX = 256
B_HEADS = 8
B_V_DIM = 256
C_HEADS = 16
MIX_WIDTH = D_MODEL

A_SIZES = (2048, 512, 512, 1024, 64, 16, 2048)
B_SIZES = (2048, 2048, 2048, 2048)
C_SIZES = (2048, 2048, 2048, 16, 2048)

LANES = 128
MXU_N = 256
VMEM_LIMIT = 56 * 1024 * 1024
MASKED = -1e30
LOG2E = 1.4426950408889634
STRIP_ROWS = 32
BISECT_UNROLL = 3
SEARCH_ROWS = 32
INT_MIN = -2 ** 31
SCORE_ROWS = 128
NEG_INF_KEY = -2139095041


def _params(sem):
    return pltpu.CompilerParams(dimension_semantics=sem, vmem_limit_bytes=VMEM_LIMIT)


def _norm_rows(x_ref, g_ref):
    x = x_ref[...]
    ms = jnp.mean(x * x, axis=-1, keepdims=True)
    return (x * lax.rsqrt(ms + EPS) * g_ref[...]).astype(BF16)


def _rope128(y, cos, sin):
    return y * cos + pltpu.roll(y, 64, 1) * sin


def _rope64(y, cos, sin):
    lane = lax.broadcasted_iota(I32, y.shape, 1)
    rot = jnp.where((lane & 63) < 32, pltpu.roll(y, 96, 1), pltpu.roll(y, 32, 1))
    return y * cos + rot * sin


class _Cols:
    def __init__(self, array, n, block=0):
        assert array.shape[1] >= (block + 1) * n
        self.array, self.n, self.block = array, n, block


class _Rows:
    def __init__(self, batch, t_q, pos0, tm):
        self.batch, self.t_q, self.pos0, self.tm = batch, t_q, pos0, tm
        self.m = batch * t_q
        self.tab_blocks = max(t_q // tm, 1)

    def tab_spec(self):
        nb = self.tab_blocks
        return pl.BlockSpec((self.tm, LANES), lambda i: (i % nb, 0))

    def tables(self, d):
        period = max(self.t_q, self.tm)
        pos = self.pos0 + (jnp.arange(period) % self.t_q)
        half = d // 2
        inv = ROPE_THETA ** (-2.0 * jnp.arange(half, dtype=F32) / d)
        ang = pos.astype(F32)[:, None] * inv[None, :]
        cos, sin = jnp.cos(ang), jnp.sin(ang)
        reps = LANES // d
        cos_f = jnp.tile(jnp.concatenate([cos, cos], axis=-1), (1, reps))
        sin_f = jnp.tile(jnp.concatenate([-sin, sin], axis=-1), (1, reps))
        return cos_f, sin_f


def _row_spec(tm, n):
    return pl.BlockSpec((tm, n), lambda i: (i, 0))


def _vec_spec(n):
    return pl.BlockSpec((1, n), lambda i: (0, 0))


class _Seg:
    def __init__(self, emit, w, aux, aux_specs, out_shape, out_specs):
        self.emit, self.w = emit, w
        self.aux, self.aux_specs = list(aux), list(aux_specs)
        self.out_shape, self.out_specs = list(out_shape), list(out_specs)


def _emit_qk(h, w_ref, aux, outs, *, n_heads, scale, use_rope, head_major, emit_f32):
    cos_ref, sin_ref, hg_ref = aux
    hg = hg_ref[...]
    of_ref = outs[0] if emit_f32 else None
    ob_ref = outs[-1]
    for c in range(n_heads // 2):
        y = jnp.dot(h, w_ref[:, c * MXU_N:(c + 1) * MXU_N], preferred_element_type=F32)
        for j in range(2):
            hd = 2 * c + j
            yh = y[:, j * LANES:(j + 1) * LANES]
            ms = jnp.mean(yh * yh, axis=-1, keepdims=True)
            yr = yh * lax.rsqrt(ms + EPS) * hg
            if use_rope:
                yr = _rope128(yr, cos_ref[...], sin_ref[...])
            if emit_f32:
                of_ref[:, hd * LANES:(hd + 1) * LANES] = yr
            yb = (yr * scale).astype(BF16)
            if head_major:
                ob_ref[hd] = yb
            else:
                ob_ref[:, hd * LANES:(hd + 1) * LANES] = yb


def _emit_plain(h, w_ref, aux, outs, *, n_cols):
    of_ref, ob_ref = outs
    for c in range(n_cols // MXU_N):
        sl = slice(c * MXU_N, (c + 1) * MXU_N)
        y = jnp.dot(h, w_ref[:, sl], preferred_element_type=F32)
        of_ref[:, sl] = y
        ob_ref[:, sl] = y.astype(BF16)


def _emit_gate(h, w_ref, aux, outs, *, n_cols):
    for c in range(n_cols // MXU_N):
        sl = slice(c * MXU_N, (c + 1) * MXU_N)
        y = jnp.dot(h, w_ref[:, sl], preferred_element_type=F32)
        outs[0][:, sl] = (y * (1.0 / (1.0 + jnp.exp(-y)))).astype(BF16)


def _emit_rope64(h, w_ref, aux, outs, *, n_cols):
    cos, sin = aux[0][...], aux[1][...]
    for c in range(n_cols // MXU_N):
        y = jnp.dot(h, w_ref[:, c * MXU_N:(c + 1) * MXU_N], preferred_element_type=F32)
        for j in range(2):
            col = c * MXU_N + j * LANES
            outs[0][:, col:col + LANES] = _rope64(y[:, j * LANES:(j + 1) * LANES], cos, sin).astype(BF16)


def _emit_idx_small(h, w_ref, aux, outs):
    cos_ref, sin_ref, hg_ref = aux
    y = jnp.dot(h, w_ref[...], preferred_element_type=F32)
    is_key = lax.broadcasted_iota(I32, y.shape, 1) < IDX_DIM
    ms = jnp.sum(jnp.where(is_key, y * y, 0.0), axis=-1, keepdims=True) * (1.0 / IDX_DIM)
    kn = _rope64(y * lax.rsqrt(ms + EPS) * hg_ref[...], cos_ref[...], sin_ref[...])
    outs[0][...] = jnp.where(is_key, kn, y)


def _emit_logf(h, w_ref, aux, outs):
    z = jnp.dot(h, w_ref[...], preferred_element_type=F32) + aux[0][...]
    outs[0][...] = jnp.minimum(z, 0.0) - jnp.log1p(jnp.exp(-jnp.abs(z)))


def _proj_kernel(x_ref, g_ref, *refs, segs):
    h = _norm_rows(x_ref, g_ref)
    n_in = sum(1 + len(s.aux) for s in segs)
    ins, outs = refs[:n_in], refs[n_in:]
    i = o = 0
    for s in segs:
        s.emit(h, ins[i], ins[i + 1:i + 1 + len(s.aux)], outs[o:o + len(s.out_shape)])
        i += 1 + len(s.aux)
        o += len(s.out_shape)


def _project(rows, x, g, segs, name):
    m, d = x.shape
    tm = rows.tm
    in_specs = [pl.BlockSpec((tm, d), lambda i: (i, 0)), pl.BlockSpec((1, d), lambda i: (0, 0))]
    args = [x, g.reshape(1, d)]
    for s in segs:
        block = s.w.block
        in_specs.append(pl.BlockSpec((d, s.w.n), lambda i, block=block: (0, block), pipeline_mode=pl.Buffered(1)))
        in_specs += s.aux_specs
        args += [s.w.array] + s.aux
    out = pl.pallas_call(
        functools.partial(_proj_kernel, segs=segs), grid=(m // tm,), in_specs=in_specs, name=name,
        out_specs=[sp for s in segs for sp in s.out_specs],
        out_shape=[sh for s in segs for sh in s.out_shape],
        compiler_params=_params(("parallel",)))(*args)
    res, o = [], 0
    for s in segs:
        res.append(out[o:o + len(s.out_shape)])
        o += len(s.out_shape)
    return res


def _seg_qk(rows, w, head_gain, *, scale, use_rope, head_major, emit_f32):
    n = w.n
    n_heads = n // HEAD_DIM
    cos, sin = rows.tables(HEAD_DIM)
    out_shape, out_specs = [], []
    if emit_f32:
        out_shape.append(jax.ShapeDtypeStruct((rows.m, n), F32))
        out_specs.append(_row_spec(rows.tm, n))
    if head_major:
        out_shape.append(jax.ShapeDtypeStruct((n_heads, rows.m, HEAD_DIM), BF16))
        out_specs.append(pl.BlockSpec((n_heads, rows.tm, HEAD_DIM), lambda i: (0, i, 0)))
    else:
        out_shape.append(jax.ShapeDtypeStruct((rows.m, n), BF16))
        out_specs.append(_row_spec(rows.tm, n))
    emit = functools.partial(_emit_qk, n_heads=n_heads, scale=scale, use_rope=use_rope,
                             head_major=head_major, emit_f32=emit_f32)
    return _Seg(emit, w, (cos, sin, head_gain.reshape(1, HEAD_DIM)),
                (rows.tab_spec(), rows.tab_spec(), _vec_spec(HEAD_DIM)), out_shape, out_specs)


def _seg_plain(rows, w):
    n = w.n
    return _Seg(functools.partial(_emit_plain, n_cols=n), w, (), (),
                [jax.ShapeDtypeStruct((rows.m, n), F32), jax.ShapeDtypeStruct((rows.m, n), BF16)],
                [_row_spec(rows.tm, n), _row_spec(rows.tm, n)])


def _seg_gate(rows, w):
    n = w.n
    return _Seg(functools.partial(_emit_gate, n_cols=n), w, (), (),
                [jax.ShapeDtypeStruct((rows.m, n), BF16)], [_row_spec(rows.tm, n)])


def _seg_rope64(rows, w):
    n = w.n
    cos, sin = rows.tables(IDX_DIM)
    return _Seg(functools.partial(_emit_rope64, n_cols=n), w, (cos, sin), (rows.tab_spec(), rows.tab_spec()),
                [jax.ShapeDtypeStruct((rows.m, n), BF16)], [_row_spec(rows.tm, n)])


def _seg_idx_small(rows, w, key_gain):
    cos, sin = rows.tables(IDX_DIM)
    gain = jnp.concatenate([key_gain, jnp.zeros((LANES - IDX_DIM,), F32)]).reshape(1, LANES)
    return _Seg(_emit_idx_small, w, (cos, sin, gain), (rows.tab_spec(), rows.tab_spec(), _vec_spec(LANES)),
                [jax.ShapeDtypeStruct((rows.m, LANES), F32)], [_row_spec(rows.tm, LANES)])


def _seg_logf(rows, w, fb):
    fbp = jnp.concatenate([fb, jnp.zeros((LANES - C_HEADS,), F32)]).reshape(1, LANES)
    return _Seg(_emit_logf, w, (fbp,), (_vec_spec(LANES),),
                [jax.ShapeDtypeStruct((rows.m, LANES), F32)], [_row_spec(rows.tm, LANES)])


def _out_kernel(x_ref, o_ref, sg_ref, w_ref, y_ref, *, n_cols):
    a = o_ref[...] * sg_ref[...]
    for c in range(n_cols // MXU_N):
        sl = slice(c * MXU_N, (c + 1) * MXU_N)
        y_ref[:, sl] = x_ref[:, sl] + jnp.dot(a, w_ref[:, sl], preferred_element_type=F32)


def _out_proj(rows, x, o, sg, w):
    m, d = x.shape
    tm = rows.tm
    kern = functools.partial(_out_kernel, n_cols=d)
    return pl.pallas_call(
        kern, grid=(m // tm,), name="out_proj",
        in_specs=[_row_spec(tm, d), _row_spec(tm, d), _row_spec(tm, d),
                  pl.BlockSpec((d, d), lambda i: (0, 0))],
        out_specs=_row_spec(tm, d), out_shape=jax.ShapeDtypeStruct((m, d), F32),
        compiler_params=_params(("parallel",)))(x, o, sg, w)


class _Attn:
    def __init__(self, batch, t_q, s_pad, s_valid, q_off, tq, tk):
        self.batch, self.t_q, self.s_pad, self.s_valid = batch, t_q, s_pad, s_valid
        self.q_off, self.tq, self.tk = q_off, tq, tk
        self.nq = t_q // tq


def _chunk_limits(i, cfg):
    q0 = cfg.q_off + i * cfg.tq
    qpos = q0 + lax.broadcasted_iota(I32, (cfg.tq, 1), 0)
    qend = jnp.minimum(((qpos >> CHUNK_SHIFT) + 1) << CHUNK_SHIFT, cfg.s_valid)
    kmin = jnp.minimum(((q0 >> CHUNK_SHIFT) + 1) << CHUNK_SHIFT, cfg.s_valid)
    kend = jnp.minimum((((q0 + cfg.tq - 1) >> CHUNK_SHIFT) + 1) << CHUNK_SHIFT, cfg.s_valid)
    return qend, kmin // cfg.tk, (kend + cfg.tk - 1) // cfg.tk


def _causal_limits(i, cfg):
    q0 = cfg.q_off + i * cfg.tq
    qpos = q0 + lax.broadcasted_iota(I32, (cfg.tq, 1), 0)
    return qpos + 1, (q0 + 1) // cfg.tk, (q0 + cfg.tq + cfg.tk - 1) // cfg.tk


def _qk(q, k):
    return lax.dot_general(q, k, (((1,), (1,)), ((), ())), preferred_element_type=F32)


def _visibility_bias(ks, qend, shape):
    kpos = ks + lax.broadcasted_iota(I32, shape, 1)
    return jnp.where(kpos < qend, 0.0, MASKED)


def _flash_step(qs, ks, vs, bias_fn, state, *, rows, strip):
    s_ref, p_ref, m_ref, l_ref, alpha_ref, acc_ref = state
    n = len(qs)
    strips = [slice(r * strip, (r + 1) * strip) for r in range(rows // strip)]
    chunks = [slice(a, a + LANES) for a in range(0, s_ref.shape[-1], LANES)]
    for c in range(n):
        s_ref[c] = _qk(qs[c], ks[c])
    for c in range(n):
        for r, rs in enumerate(strips):
            mx = None
            for ch in chunks:
                s = s_ref[c, rs, ch]
                b = bias_fn(c, r, ch)
                if b is not None:
                    s = s + b
                    s_ref[c, rs, ch] = s
                mx = s if mx is None else jnp.maximum(mx, s)
            m_old = m_ref[c, rs, :]
            m_new = jnp.maximum(m_old, jnp.broadcast_to(jnp.max(mx, axis=-1, keepdims=True), mx.shape))
            alpha_ref[c, rs, :] = jnp.exp2(m_old - m_new)
            m_ref[c, rs, :] = m_new
        for rs in strips:
            m = m_ref[c, rs, :]
            psum = None
            for ch in chunks:
                p = jnp.exp2(s_ref[c, rs, ch] - m)
                if l_ref is not None:
                    psum = p if psum is None else psum + p
                p_ref[c, rs, ch] = p.astype(BF16)
            if l_ref is not None:
                l_ref[c, rs, :] = alpha_ref[c, rs, :] * l_ref[c, rs, :] + psum
    for c in range(n):
        alpha = jnp.concatenate([alpha_ref[c]] * (acc_ref.shape[-1] // LANES), axis=1)
        acc_ref[c] = alpha * acc_ref[c] + jnp.dot(p_ref[c], vs[c], preferred_element_type=F32)


def _init_state(state):
    _, _, m_ref, l_ref, _, acc_ref = state
    m_ref[...] = jnp.full(m_ref.shape, MASKED, F32)
    acc_ref[...] = jnp.zeros(acc_ref.shape, F32)
    if l_ref is not None:
        l_ref[...] = jnp.zeros(l_ref.shape, F32)


def _flash_scratch(chains, rows, tk, dv, with_l):
    stat = pltpu.VMEM((chains, rows, LANES), F32)
    return ([pltpu.VMEM((chains, rows, tk), F32), pltpu.VMEM((chains, rows, tk), BF16), stat]
            + ([stat] if with_l else []) + [stat, pltpu.VMEM((chains, rows, dv), F32)])


def _with_ones(v):
    return jnp.concatenate([v, jnp.ones(v.shape, v.dtype)], axis=1)


def _dsa_kernel(q_ref, qi_ref, sm_ref, k_ref, v_ref, kit_ref, o_ref,
                key_ref, bias_ref, stat_ref, look_ref, s_ref, p_ref, m_ref, alpha_ref, acc_ref, *, cfg, topk):
    tq, tk = cfg.tq, cfg.tk
    i = pl.program_id(1)
    qend, _, n_kv = _chunk_limits(i, cfg)
    sub = min(tq, SCORE_ROWS)
    stat_ref[0] = jnp.full((tq, LANES), INT_MIN, I32)
    stat_ref[1] = jnp.zeros((tq, LANES), I32)

    def score_block(j, carry):
        ks = pl.multiple_of(j * tk, tk)
        for r0 in range(0, tq, sub):
            rs = slice(r0, r0 + sub)
            w = sm_ref[rs, IDX_DIM:IDX_DIM + IDX_HEADS] * (IDX_DIM ** -0.5 * IDX_HEADS ** -0.5)
            acc = jnp.zeros((sub, tk), F32)
            for pair in range(IDX_HEADS // 2):
                qp = qi_ref[rs, pair * LANES:(pair + 1) * LANES]
                for e in range(2):
                    hd = 2 * pair + e
                    sc = jnp.dot(qp, kit_ref[e, :, pl.ds(ks, tk)], preferred_element_type=F32)
                    acc = acc + w[:, hd:hd + 1] * jnp.maximum(sc, 0.0)
            bits = lax.bitcast_convert_type(acc, I32)
            key = bits ^ ((bits >> 31) & 0x7FFFFFFF)
            key = jnp.where(acc == 0.0, 0, key)
            kpos = ks + lax.broadcasted_iota(I32, (sub, tk), 1)
            key = jnp.where(kpos < qend[rs], key, NEG_INF_KEY)
            key_ref[rs, pl.ds(ks, tk)] = key
            kmax, nfin = stat_ref[0, rs, :], stat_ref[1, rs, :]
            for c in range(tk // LANES):
                kc = key[:, c * LANES:(c + 1) * LANES]
                kmax = jnp.maximum(kmax, kc)
                nfin = nfin + jnp.where(kc > NEG_INF_KEY, 1, 0)
            stat_ref[0, rs, :], stat_ref[1, rs, :] = kmax, nfin
        return carry

    lax.fori_loop(0, n_kv, score_block, 0)

    def key_chunk(j, c):
        return key_ref[:, pl.ds(pl.multiple_of(j * tk + c * LANES, LANES), LANES)]

    def lane_total(x):
        return jnp.broadcast_to(jnp.sum(x, axis=-1, keepdims=True), x.shape)

    def count_ge(cand):
        def body(j, part):
            ks = pl.multiple_of(j * tk, tk)
            ge = jnp.where(key_ref[:, pl.ds(ks, tk)] >= cand, 1, 0)
            for c in range(tk // LANES):
                part = part + ge[:, c * LANES:(c + 1) * LANES]
            return part
        part = lax.fori_loop(0, n_kv, body, jnp.zeros((tq, LANES), I32))
        return jnp.sum(part, axis=-1, keepdims=True)

    n_fin = lane_total(stat_ref[1])
    small = n_fin <= topk
    look_ref[0] = jnp.full((tq, LANES), NEG_INF_KEY + 1, I32)
    look_ref[1] = jnp.broadcast_to(jnp.max(stat_ref[0], axis=-1, keepdims=True), (tq, LANES)) + 1
    look_ref[2] = n_fin

    def pending():
        lo, hi, cnt = look_ref[0], look_ref[1], look_ref[2]
        open_row = (cnt != topk) & (hi != lo + 1) & jnp.logical_not(small)
        return jnp.max(jnp.where(open_row, 1, 0))

    groups = [slice(r, r + min(tq, SEARCH_ROWS)) for r in range(0, tq, min(tq, SEARCH_ROWS))]

    def halve(n_blocks):
        state = [[look_ref[v, gs, :] for v in range(3)] for gs in groups]
        part = jnp.zeros((groups[0].stop, LANES), I32)
        for _ in range(BISECT_UNROLL):
            for g, gs in enumerate(groups):
                lo, hi, cnt = state[g]
                mid = (lo >> 1) + (hi >> 1) + (lo & hi & 1)
                part = part >> 31
                for a in range(0, n_blocks * tk, LANES):
                    part = part + jnp.where(key_ref[gs, a:a + LANES] >= mid, 1, 0)
                c = lane_total(part)
                up = c >= topk
                state[g] = [jnp.where(up, mid, lo), jnp.where(up, hi, mid), jnp.where(up, c, cnt)]
        for g, gs in enumerate(groups):
            for v in range(3):
                look_ref[v, gs, :] = state[g][v]

    def search_round(_):
        for n_blocks in range(1, cfg.s_pad // tk + 1):
            pl.when(n_kv == n_blocks)(functools.partial(halve, n_blocks))
        return pending()

    lax.while_loop(lambda flag: flag > 0, search_round, pending())
    thr, n_ge = look_ref[0], look_ref[2]
    tie_rows = (n_ge > topk) & jnp.logical_not(small)

    @pl.when(jnp.max(jnp.where(tie_rows, 1, 0)) > 0)
    def _drop_late_ties():
        need_f = jnp.broadcast_to((topk - count_ge(thr[:, :1] + 1)).astype(F32), (tq, LANES))
        upper = jnp.where(lax.broadcasted_iota(I32, (LANES, LANES), 0) <= lax.broadcasted_iota(I32, (LANES, LANES), 1),
                          1.0, 0.0).astype(BF16)

        def body(j, seen):
            for c in range(tk // LANES):
                kb = key_chunk(j, c)
                eq = kb == thr
                eq_f = jnp.where(eq, 1.0, 0.0)
                rank = seen + jnp.dot(eq_f.astype(BF16), upper, preferred_element_type=F32)
                drop = eq & (rank > need_f) & tie_rows
                key_ref[:, pl.ds(pl.multiple_of(j * tk + c * LANES, LANES), LANES)] = jnp.where(drop, thr - 1, kb)
                seen = seen + lane_total(eq_f)
            return seen

        lax.fori_loop(0, n_kv, body, jnp.zeros((tq, LANES), F32))

    thr_keep = jnp.where(small, NEG_INF_KEY + 1, jnp.maximum(thr, NEG_INF_KEY + 1))
    state = (s_ref, p_ref, m_ref, None, alpha_ref, acc_ref)
    _init_state(state)
    strip = min(tq, STRIP_ROWS)
    strips_per_head = tq // strip

    def attn_block(j, carry):
        ks = pl.multiple_of(j * tk, tk)
        for c in range(tk // LANES):
            bias_ref[:, c * LANES:(c + 1) * LANES] = jnp.where(key_chunk(j, c) >= thr_keep, 0.0, MASKED)
        cols =[slice(g * HEAD_DIM, (g + 1) * HEAD_DIM) for g in range(A_KV_HEADS)]
        _flash_step([q_ref[g * A_REP:(g + 1) * A_REP].reshape(A_REP * tq, HEAD_DIM) for g in range(A_KV_HEADS)],
                    [k_ref[pl.ds(ks, tk), c] for c in cols],
                    [_with_ones(v_ref[pl.ds(ks, tk), c]) for c in cols],
                    lambda c, r, ch: bias_ref[(r % strips_per_head) * strip:(r % strips_per_head + 1) * strip, ch],
                    state, rows=A_REP * tq, strip=strip)
        return carry

    lax.fori_loop(0, n_kv, attn_block, 0)
    for g in range(A_KV_HEADS):
        o = acc_ref[g, :, :HEAD_DIM] / acc_ref[g, :, HEAD_DIM:]
        for r in range(A_REP):
            hd = g * A_REP + r
            o_ref[:, hd * HEAD_DIM:(hd + 1) * HEAD_DIM] = o[r * tq:(r + 1) * tq].astype(BF16)


def _dsa_attention(cfg, q_hm, q_row0, qi, sm, k, v, kit, topk):
    tq, nq, s = cfg.tq, cfg.nq, cfg.s_pad
    qb0 = q_row0 // tq
    kern = functools.partial(_dsa_kernel, cfg=cfg, topk=topk)
    return pl.pallas_call(
        kern, grid=(cfg.batch, nq), name="dsa_attention",
        in_specs=[pl.BlockSpec((A_HEADS, tq, HEAD_DIM), lambda b, i: (0, qb0 + b * nq + i, 0)),
                  pl.BlockSpec((tq, IDX_HEADS * IDX_DIM), lambda b, i: (qb0 + b * nq + i, 0)),
                  pl.BlockSpec((tq, LANES), lambda b, i: (qb0 + b * nq + i, 0)),
                  pl.BlockSpec((s, A_KV_HEADS * HEAD_DIM), lambda b, i: (b, 0), pipeline_mode=pl.Buffered(1)),
                  pl.BlockSpec((s, A_KV_HEADS * HEAD_DIM), lambda b, i: (b, 0), pipeline_mode=pl.Buffered(1)),
                  pl.BlockSpec((None, 2, LANES, s), lambda b, i: (b, 0, 0, 0), pipeline_mode=pl.Buffered(1))],
        out_specs=pl.BlockSpec((tq, MIX_WIDTH), lambda b, i: (b * nq + i, 0)),
        out_shape=jax.ShapeDtypeStruct((cfg.batch * cfg.t_q, MIX_WIDTH), BF16),
        scratch_shapes=[pltpu.VMEM((tq, s), I32), pltpu.VMEM((tq, cfg.tk), F32), pltpu.VMEM((2, tq, LANES), I32),
                        pltpu.VMEM((3, tq, LANES), I32)]
        + _flash_scratch(A_KV_HEADS, A_REP * tq, cfg.tk, 2 * HEAD_DIM, with_l=False),
        compiler_params=_params(("parallel", "arbitrary")))(q_hm, qi, sm, k, v, kit)


B_STEP_HEADS = 2


def _run_blocks(n_full, n_kv, step):
    def plain(j, carry):
        step(j, False)
        return carry

    def masked(j, carry):
        step(j, True)
        return carry

    lax.fori_loop(0, n_full, plain, 0)
    lax.fori_loop(n_full, n_kv, masked, 0)


def _diff_kernel(q_ref, k_ref, v_ref, lq1_ref, lk1_ref, lq2_ref, lk2_ref, sub_ref, o_ref,
                 bias_ref, s_ref, p_ref, m_ref, l_ref, alpha_ref, acc_ref, *, cfg, lam_init):
    tq, tk = cfg.tq, cfg.tk
    qend, n_full, n_kv = _chunk_limits(pl.program_id(2), cfg)
    state = (s_ref, p_ref, m_ref, l_ref, alpha_ref, acc_ref)
    _init_state(state)
    strip = min(tq, STRIP_ROWS)
    n_chains = 2 * B_STEP_HEADS

    def step(j, masked):
        ks = pl.multiple_of(j * tk, tk)
        if masked:
            bias_ref[...] = _visibility_bias(ks, qend, (tq, tk))
        qk_cols = [slice(c * HEAD_DIM, (c + 1) * HEAD_DIM) for c in range(n_chains)]
        v_cols = [slice((c // 2) * B_V_DIM, (c // 2 + 1) * B_V_DIM) for c in range(n_chains)]
        _flash_step([q_ref[:, c] for c in qk_cols],
                    [k_ref[pl.ds(ks, tk), c] for c in qk_cols],
                    [v_ref[pl.ds(ks, tk), c] for c in v_cols],
                    (lambda c, r, ch: bias_ref[r * strip:(r + 1) * strip, ch]) if masked
                    else (lambda c, r, ch: None),
                    state, rows=tq, strip=strip)

    _run_blocks(n_full, n_kv, step)
    lam = (jnp.exp(jnp.sum(lq1_ref[...] * lk1_ref[...], axis=-1, keepdims=True))
           - jnp.exp(jnp.sum(lq2_ref[...] * lk2_ref[...], axis=-1, keepdims=True)) + lam_init)
    def normalised(c):
        return acc_ref[c] / jnp.sum(l_ref[c], axis=-1, keepdims=True)

    for h in range(B_STEP_HEADS):
        o = normalised(2 * h) - lam * normalised(2 * h + 1)
        ms = jnp.mean(o * o, axis=-1, keepdims=True)
        o_ref[:, h * B_V_DIM:(h + 1) * B_V_DIM] = (
            o * lax.rsqrt(ms + EPS) * sub_ref[...] * (1.0 - lam_init)).astype(BF16)


def _diff_attention(cfg, q, q_row0, k, v, lam_vecs, subln, lam_init):
    tq, nq, s = cfg.tq, cfg.nq, cfg.s_pad
    qb0 = q_row0 // tq
    width = B_STEP_HEADS * B_V_DIM
    kern = functools.partial(_diff_kernel, cfg=cfg, lam_init=lam_init)
    vec = pl.BlockSpec((1, HEAD_DIM), lambda b, h, i: (0, 0))
    return pl.pallas_call(
        kern, grid=(cfg.batch, B_HEADS // B_STEP_HEADS, nq), name="diff_attention",
        in_specs=[pl.BlockSpec((tq, width), lambda b, h, i: (qb0 + b * nq + i, h)),
                  pl.BlockSpec((s, width), lambda b, h, i: (b, h)),
                  pl.BlockSpec((s, width), lambda b, h, i: (b, h)),
                  vec, vec, vec, vec,
                  pl.BlockSpec((1, B_V_DIM), lambda b, h, i: (0, 0))],
        out_specs=pl.BlockSpec((tq, width), lambda b, h, i: (b * nq + i, h)),
        out_shape=jax.ShapeDtypeStruct((cfg.batch * cfg.t_q, MIX_WIDTH), BF16),
        scratch_shapes=[pltpu.VMEM((tq, cfg.tk), F32)]
        + _flash_scratch(2 * B_STEP_HEADS, tq, cfg.tk, B_V_DIM, with_l=True),
        compiler_params=_params(("parallel", "parallel", "arbitrary")))(
            q, k, v, *[a.reshape(1, HEAD_DIM) for a in lam_vecs], subln.reshape(1, B_V_DIM))


C_STEP_HEADS = 4


def _fox_kernel(q_ref, k_ref, v_ref, cq_ref, ck_ref, o_ref,
                bias_ref, cqh_ref, s_ref, p_ref, m_ref, alpha_ref, acc_ref, *, cfg):
    tq, tk = cfg.tq, cfg.tk
    hg = pl.program_id(1)
    qend, n_full, n_kv = _causal_limits(pl.program_id(2), cfg)
    state = (s_ref, p_ref, m_ref, None, alpha_ref, acc_ref)
    _init_state(state)
    strip = min(tq, STRIP_ROWS)
    lane = lax.broadcasted_iota(I32, (tq, LANES), 1)
    for c in range(C_STEP_HEADS):
        head_lane = lane == hg * C_STEP_HEADS + c
        cqh_ref[c] = jnp.broadcast_to(
            jnp.sum(jnp.where(head_lane, cq_ref[...], 0.0), axis=-1, keepdims=True), (tq, LANES))

    def step(j, masked):
        ks = pl.multiple_of(j * tk, tk)
        if masked:
            bias_ref[...] = _visibility_bias(ks, qend, (tq, tk))

        def bias(c, r, ch):
            rs = slice(r * strip, (r + 1) * strip)
            b = cqh_ref[c, rs, :] - ck_ref[c:c + 1, pl.ds(pl.multiple_of(ks + ch.start, LANES), LANES)]
            return b + bias_ref[rs, ch] if masked else b

        cols = [slice(c * HEAD_DIM, (c + 1) * HEAD_DIM) for c in range(C_STEP_HEADS)]
        _flash_step([q_ref[:, c] for c in cols],
                    [k_ref[pl.ds(ks, tk), c] for c in cols],
                    [_with_ones(v_ref[pl.ds(ks, tk), c]) for c in cols],
                    bias, state, rows=tq, strip=strip)

    _run_blocks(n_full, n_kv, step)
    for c in range(C_STEP_HEADS):
        o_ref[:, c * HEAD_DIM:(c + 1) * HEAD_DIM] = (
            acc_ref[c, :, :HEAD_DIM] / acc_ref[c, :, HEAD_DIM:]).astype(BF16)


def _fox_attention(cfg, q, q_row0, k, v, cq, ck_rows):
    tq, nq, s = cfg.tq, cfg.nq, cfg.s_pad
    qb0 = q_row0 // tq
    width = C_STEP_HEADS * HEAD_DIM
    groups = C_HEADS // C_STEP_HEADS
    kern = functools.partial(_fox_kernel, cfg=cfg)
    return pl.pallas_call(
        kern, grid=(cfg.batch, groups, nq), name="fox_attention",
        in_specs=[pl.BlockSpec((tq, width), lambda b, h, i: (qb0 + b * nq + i, h)),
                  pl.BlockSpec((s, width), lambda b, h, i: (b, h)),
                  pl.BlockSpec((s, width), lambda b, h, i: (b, h)),
                  pl.BlockSpec((tq, LANES), lambda b, h, i: (qb0 + b * nq + i, 0)),
                  pl.BlockSpec((None, C_STEP_HEADS, s), lambda b, h, i: (b * groups + h, 0, 0))],
        out_specs=pl.BlockSpec((tq, width), lambda b, h, i: (b * nq + i, h)),
        out_shape=jax.ShapeDtypeStruct((cfg.batch * cfg.t_q, MIX_WIDTH), BF16),
        scratch_shapes=[pltpu.VMEM((tq, cfg.tk), F32), pltpu.VMEM((C_STEP_HEADS, tq, LANES), F32)]
        + _flash_scratch(C_STEP_HEADS, tq, cfg.tk, 2 * HEAD_DIM, with_l=False),
        compiler_params=_params(("parallel", "parallel", "arbitrary")))(q, k, v, cq, ck_rows)


def _cumsum_kernel(x_ref, o_ref, carry_ref, *, tb):
    @pl.when(pl.program_id(1) == 0)
    def _():
        carry_ref[...] = jnp.zeros(carry_ref.shape, F32)

    x = x_ref[...]
    hi = x.astype(BF16)
    r1 = x - hi.astype(F32)
    mid = r1.astype(BF16)
    low = (r1 - mid.astype(F32)).astype(BF16)
    tri = jnp.where(lax.broadcasted_iota(I32, (tb, tb), 0) >= lax.broadcasted_iota(I32, (tb, tb), 1),
                    1.0, 0.0).astype(BF16)
    c = (jnp.dot(tri, hi, preferred_element_type=F32) + jnp.dot(tri, mid, preferred_element_type=F32)
         + jnp.dot(tri, low, preferred_element_type=F32)) + carry_ref[...]
    o_ref[...] = c * LOG2E
    carry_ref[...] = c[tb - 1:tb, :]


def _cumsum_rows(x, batch, t, tb):
    nt = t // tb
    return pl.pallas_call(
        functools.partial(_cumsum_kernel, tb=tb), grid=(batch, nt), name="cumsum_rows",
        in_specs=[pl.BlockSpec((tb, LANES), lambda b, j: (b * nt + j, 0))],
        out_specs=pl.BlockSpec((tb, LANES), lambda b, j: (b * nt + j, 0)),
        out_shape=jax.ShapeDtypeStruct((batch * t, LANES), F32),
        scratch_shapes=[pltpu.VMEM((1, LANES), F32)],
        compiler_params=_params(("parallel", "arbitrary")))(x)


def _split_w(w_in, sizes):
    offs = np.cumsum((0,) + tuple(sizes))
    return [w_in[:, int(offs[i]):int(offs[i + 1])] for i in range(len(sizes))]


def _pad_rows(a, rows):
    return jnp.pad(a, ((0, 0), (0, rows - a.shape[1])) + ((0, 0),) * (a.ndim - 2))


def _sample_keys(cache, new, s_pad):
    b = cache.shape[0]
    full = jnp.concatenate([cache.reshape(b, cache.shape[1], -1), new.reshape(b, new.shape[1], -1)], axis=1)
    return _pad_rows(full, s_pad).astype(BF16).reshape(b * s_pad, -1)


def _key_transposed_pairs(ki, batch, s):
    kt = jnp.swapaxes(ki.reshape(batch, s, IDX_DIM), 1, 2).astype(BF16)
    z = jnp.zeros_like(kt)
    return jnp.stack([jnp.concatenate([kt, z], axis=1), jnp.concatenate([z, kt], axis=1)], axis=1)


def _layer_a(pr, sr, cfg_p, cfg_s, xp, xs, ck, cv, cki, norm, w_in, w_out, qn, kn, ikn):
    w_all = w_in.astype(BF16)
    wq, wk, wv, wqi = _Cols(w_all, 2048, 0), _Cols(w_all, 512, 4), _Cols(w_all, 512, 5), _Cols(w_all, 1024, 3)
    wsm = _Cols(w_all, LANES, sum(A_SIZES[:4]) // LANES)
    wg = _Cols(_split_w(w_in, A_SIZES)[6].astype(BF16), MIX_WIDTH)
    wo = w_out.astype(BF16)

    def project(rows, x):
        (q,), (qi,), (sm,) = _project(rows, x, norm, [
            _seg_qk(rows, wq, qn, scale=HEAD_DIM ** -0.5 * LOG2E, use_rope=True, head_major=True, emit_f32=False),
            _seg_rope64(rows, wqi), _seg_idx_small(rows, wsm, ikn)], "proj_a_query")
        (k_f, k_b), (v_f, v_b), (sg,) = _project(rows, x, norm, [
            _seg_qk(rows, wk, kn, scale=1.0, use_rope=True, head_major=False, emit_f32=True),
            _seg_plain(rows, wv), _seg_gate(rows, wg)], "proj_a_kv_gate")
        return q, k_f, k_b, v_f, v_b, qi, sm, sg

    q, k_f, k_b, v_f, v_b, qi, sm, sg = project(pr, xp)
    ki_f = sm[:, :IDX_DIM]
    o = _dsa_attention(cfg_p, q, 0, qi, sm, k_b, v_b,
                       _key_transposed_pairs(ki_f, pr.batch, pr.t_q), min(TOPK_MAX, pr.t_q // 4))
    yp = _out_proj(pr, xp, o, sg, wo)

    q2, k2_f, _, v2_f, _, qi2, sm2, sg2 = project(sr, xs)
    ki2_f = sm2[:, :IDX_DIM]
    b, s = sr.batch, sr.t_q
    k_all = _sample_keys(ck, k2_f.reshape(b, s, -1), cfg_s.s_pad)
    v_all = _sample_keys(cv, v2_f.reshape(b, s, -1), cfg_s.s_pad)
    ki_all = _pad_rows(jnp.concatenate([cki, ki2_f.reshape(b, s, IDX_DIM)], axis=1), cfg_s.s_pad)
    o2 = _dsa_attention(cfg_s, q2, 0, qi2, sm2, k_all, v_all,
                        _key_transposed_pairs(ki_all.reshape(-1, IDX_DIM), b, cfg_s.s_pad),
                        min(TOPK_MAX, cfg_s.s_valid // 4))
    ys = _out_proj(sr, xs, o2, sg2, wo)
    kv = (A_KV_HEADS, HEAD_DIM)
    state = (k_f.reshape(pr.batch, pr.t_q, *kv), v_f.reshape(pr.batch, pr.t_q, *kv),
             ki_f.reshape(pr.batch, pr.t_q, IDX_DIM),
             k2_f.reshape(b, s, *kv), v2_f.reshape(b, s, *kv), ki2_f.reshape(b, s, IDX_DIM))
    return yp, ys, state


def _layer_b(layer, pr, sr, cfg_p, cfg_s, xp, xs, ck, cv, norm, w_in, w_out, qn, kn,
             lq1, lk1, lq2, lk2, subln):
    lam_init = 0.8 - 0.6 * float(np.exp(-0.3 * layer))
    w_all = w_in.astype(BF16)
    wq, wk, wv, wg = [_Cols(w_all, 2048, j) for j in range(4)]
    wo = w_out.astype(BF16)

    def project(rows, x):
        (q,), (sg,) = _project(rows, x, norm, [
            _seg_qk(rows, wq, qn, scale=HEAD_DIM ** -0.5 * LOG2E, use_rope=True, head_major=False, emit_f32=False),
            _seg_gate(rows, wg)], "proj_b_query_gate")
        (k_f, k_b), (v_f, v_b) = _project(rows, x, norm, [
            _seg_qk(rows, wk, kn, scale=1.0, use_rope=True, head_major=False, emit_f32=True),
            _seg_plain(rows, wv)], "proj_b_kv")
        return q, k_f, k_b, v_f, v_b, sg

    lam_vecs = (lq1, lk1, lq2, lk2)
    q, k_f, k_b, v_f, v_b, sg = project(pr, xp)
    o = _diff_attention(cfg_p, q, 0, k_b, v_b, lam_vecs, subln, lam_init)
    yp = _out_proj(pr, xp, o, sg, wo)

    q2, k2_f, _, v2_f, _, sg2 = project(sr, xs)
    b, s = sr.batch, sr.t_q
    k_all = _sample_keys(ck, k2_f.reshape(b, s, -1), cfg_s.s_pad)
    v_all = _sample_keys(cv, v2_f.reshape(b, s, -1), cfg_s.s_pad)
    o2 = _diff_attention(cfg_s, q2, 0, k_all, v_all, lam_vecs, subln, lam_init)
    ys = _out_proj(sr, xs, o2, sg2, wo)
    state = (k_f.reshape(pr.batch, pr.t_q, 2 * B_HEADS, HEAD_DIM), v_f.reshape(pr.batch, pr.t_q, B_HEADS, B_V_DIM),
             k2_f.reshape(b, s, 2 * B_HEADS, HEAD_DIM), v2_f.reshape(b, s, B_HEADS, B_V_DIM))
    return yp, ys, state


def _head_rows(c, batch, s):
    rows = jnp.swapaxes(c.reshape(batch, s, LANES)[:, :, :C_HEADS], 1, 2)
    return rows.reshape(batch * C_HEADS // C_STEP_HEADS, C_STEP_HEADS, s)


def _layer_c(pr, sr, cfg_p, cfg_s, xp, xs, ck, cv, clogf, norm, w_in, w_out, qn, kn, fb):
    w_all = w_in.astype(BF16)
    wq, wk, wv = [_Cols(w_all, 2048, j) for j in range(3)]
    wf = _Cols(w_all, LANES, sum(C_SIZES[:3]) // LANES)
    wg = _Cols(_split_w(w_in, C_SIZES)[4].astype(BF16), MIX_WIDTH)
    wo = w_out.astype(BF16)

    def project(rows, x):
        (q,), (logf,), (sg,) = _project(rows, x, norm, [
            _seg_qk(rows, wq, qn, scale=HEAD_DIM ** -0.5 * LOG2E, use_rope=False, head_major=False, emit_f32=False),
            _seg_logf(rows, wf, fb), _seg_gate(rows, wg)], "proj_c_query_gate")
        (k_f, k_b), (v_f, v_b) = _project(rows, x, norm, [
            _seg_qk(rows, wk, kn, scale=1.0, use_rope=False, head_major=False, emit_f32=True),
            _seg_plain(rows, wv)], "proj_c_kv")
        return q, k_f, k_b, v_f, v_b, logf, sg

    q, k_f, k_b, v_f, v_b, logf, sg = project(pr, xp)
    c = _cumsum_rows(logf, pr.batch, pr.t_q, 512)
    o = _fox_attention(cfg_p, q, 0, k_b, v_b, c, _head_rows(c, pr.batch, pr.t_q))
    yp = _out_proj(pr, xp, o, sg, wo)

    q2, k2_f, _, v2_f, _, logf2, sg2 = project(sr, xs)
    b, s = sr.batch, sr.t_q
    k_all = _sample_keys(ck, k2_f.reshape(b, s, -1), cfg_s.s_pad)
    v_all = _sample_keys(cv, v2_f.reshape(b, s, -1), cfg_s.s_pad)
    logf_all = jnp.concatenate([jnp.pad(clogf.astype(F32), ((0, 0), (0, 0), (0, LANES - C_HEADS))),
                                logf2.reshape(b, s, LANES)], axis=1)
    c2 = _cumsum_rows(_pad_rows(logf_all, cfg_s.s_pad).reshape(b * cfg_s.s_pad, LANES), b, cfg_s.s_pad, LANES)
    cq2 = c2.reshape(b, cfg_s.s_pad, LANES)[:, PAST_LEN:PAST_LEN + s].reshape(b * s, LANES)
    o2 = _fox_attention(cfg_s, q2, 0, k_all, v_all, cq2, _head_rows(c2, b, cfg_s.s_pad))
    ys = _out_proj(sr, xs, o2, sg2, wo)
    hd = (C_HEADS, HEAD_DIM)
    state = (k_f.reshape(pr.batch, pr.t_q, *hd), v_f.reshape(pr.batch, pr.t_q, *hd),
             logf[:, :C_HEADS].reshape(pr.batch, pr.t_q, C_HEADS),
             k2_f.reshape(b, s, *hd), v2_f.reshape(b, s, *hd), logf2[:, :C_HEADS].reshape(b, s, C_HEADS))
    return yp, ys, state


def kernel(x_prompt, x_sample, cache_l0_k, cache_l0_v, cache_l0_kidx, cache_l1_k, cache_l1_v, cache_l2_k, cache_l2_v, cache_l2_logf, cache_l3_k, cache_l3_v, cache_l3_kidx, l0_norm, l0_w_in, l0_w_out, l0_q_norm, l0_k_norm, l0_idx_k_norm, l1_norm, l1_w_in, l1_w_out, l1_q_norm, l1_k_norm, l1_lambda_q1, l1_lambda_k1, l1_lambda_q2, l1_lambda_k2, l1_subln, l2_norm, l2_w_in, l2_w_out, l2_q_norm, l2_k_norm, l2_forget_bias, l3_norm, l3_w_in, l3_w_out, l3_q_norm, l3_k_norm, l3_idx_k_norm):
    bp, tp, d = x_prompt.shape
    bs, ts, _ = x_sample.shape
    s_valid = PAST_LEN + ts
    s_pad = -(-s_valid // LANES) * LANES
    pr = _Rows(bp, tp, 0, 512)
    sr = _Rows(bs, ts, PAST_LEN, bs * ts)
    a_p = _Attn(bp, tp, tp, tp, 0, 128, 512)
    a_s = _Attn(bs, ts, s_pad, s_valid, PAST_LEN, ts, LANES)
    bc_p = _Attn(bp, tp, tp, tp, 0, 512, 512)
    bc_s = a_s

    xp = x_prompt.reshape(bp * tp, d)
    xs = x_sample.reshape(bs * ts, d)
    xp, xs, st0 = _layer_a(pr, sr, a_p, a_s, xp, xs, cache_l0_k, cache_l0_v, cache_l0_kidx,
                           l0_norm, l0_w_in, l0_w_out, l0_q_norm, l0_k_norm, l0_idx_k_norm)
    xp, xs, st1 = _layer_b(1, pr, sr, bc_p, bc_s, xp, xs, cache_l1_k, cache_l1_v,
                           l1_norm, l1_w_in, l1_w_out, l1_q_norm, l1_k_norm,
                           l1_lambda_q1, l1_lambda_k1, l1_lambda_q2, l1_lambda_k2, l1_subln)
    xp, xs, st2 = _layer_c(pr, sr, bc_p, bc_s, xp, xs, cache_l2_k, cache_l2_v, cache_l2_logf,
                           l2_norm, l2_w_in, l2_w_out, l2_q_norm, l2_k_norm, l2_forget_bias)
    xp, xs, st3 = _layer_a(pr, sr, a_p, a_s, xp, xs, cache_l3_k, cache_l3_v, cache_l3_kidx,
                           l3_norm, l3_w_in, l3_w_out, l3_q_norm, l3_k_norm, l3_idx_k_norm)
    return (xp.reshape(bp, tp, d), xs.reshape(bs, ts, d)) + st0 + st1 + st2 + st3
```

```python
import functools

import numpy as np
import jax
import jax.numpy as jnp
from jax import lax
from jax.experimental import pallas as pl
from jax.experimental.pallas import tpu as pltpu

F32 = jnp.float32
BF16 = jnp.bfloat16
I32 = jnp.int32

D_MODEL = 2048
PAST_LEN = 1024
CHUNK_SHIFT = 6
ROPE_THETA = 10000.0
EPS = 1e-6
HEAD_DIM = 128
A_HEADS = 16
A_KV_HEADS = 4
A_REP = A_HEADS // A_KV_HEADS
IDX_HEADS = 16
IDX_DIM = 64
TOPK_MAX = 256
B_HEADS = 8
B_V_DIM = 256
C_HEADS = 16
MIX_WIDTH = D_MODEL

A_SIZES = (2048, 512, 512, 1024, 64, 16, 2048)
B_SIZES = (2048, 2048, 2048, 2048)
C_SIZES = (2048, 2048, 2048, 16, 2048)

LANES = 128
MXU_N = 256
VMEM_LIMIT = 56 * 1024 * 1024
MASKED = -1e30
LOG2E = 1.4426950408889634
STRIP_ROWS = 32
BISECT_UNROLL = 3
SEARCH_ROWS = 32
INT_MIN = -2 ** 31
SCORE_ROWS = 128
NEG_INF_KEY = -2139095041


def _params(sem):
    return pltpu.CompilerParams(dimension_semantics=sem, vmem_limit_bytes=VMEM_LIMIT)


def _norm_rows(x_ref, g_ref):
    x = x_ref[...]
    ms = jnp.mean(x * x, axis=-1, keepdims=True)
    return (x * lax.rsqrt(ms + EPS) * g_ref[...]).astype(BF16)


def _rope128(y, cos, sin):
    return y * cos + pltpu.roll(y, 64, 1) * sin


def _rope64(y, cos, sin):
    lane = lax.broadcasted_iota(I32, y.shape, 1)
    rot = jnp.where((lane & 63) < 32, pltpu.roll(y, 96, 1), pltpu.roll(y, 32, 1))
    return y * cos + rot * sin


class _Cols:
    def __init__(self, array, n, block=0):
        assert array.shape[1] >= (block + 1) * n
        self.array, self.n, self.block = array, n, block


class _Rows:
    def __init__(self, batch, t_q, pos0, tm):
        self.batch, self.t_q, self.pos0, self.tm = batch, t_q, pos0, tm
        self.m = batch * t_q
        self.tab_blocks = max(t_q // tm, 1)

    def tab_spec(self):
        nb = self.tab_blocks
        return pl.BlockSpec((self.tm, LANES), lambda i: (i % nb, 0))

    def tables(self, d):
        period = max(self.t_q, self.tm)
        pos = self.pos0 + (jnp.arange(period) % self.t_q)
        half = d // 2
        inv = ROPE_THETA ** (-2.0 * jnp.arange(half, dtype=F32) / d)
        ang = pos.astype(F32)[:, None] * inv[None, :]
        cos, sin = jnp.cos(ang), jnp.sin(ang)
        reps = LANES // d
        cos_f = jnp.tile(jnp.concatenate([cos, cos], axis=-1), (1, reps))
        sin_f = jnp.tile(jnp.concatenate([-sin, sin], axis=-1), (1, reps))
        return cos_f, sin_f


def _row_spec(tm, n):
    return pl.BlockSpec((tm, n), lambda i: (i, 0))


def _vec_spec(n):
    return pl.BlockSpec((1, n), lambda i: (0, 0))


class _Seg:
    def __init__(self, emit, w, aux, aux_specs, out_shape, out_specs):
        self.emit, self.w = emit, w
        self.aux, self.aux_specs = list(aux), list(aux_specs)
        self.out_shape, self.out_specs = list(out_shape), list(out_specs)


def _emit_qk(h, w_ref, aux, outs, *, n_heads, scale, use_rope, head_major, emit_f32):
    cos_ref, sin_ref, hg_ref = aux
    hg = hg_ref[...]
    of_ref = outs[0] if emit_f32 else None
    ob_ref = outs[-1]
    for c in range(n_heads // 2):
        y = jnp.dot(h, w_ref[:, c * MXU_N:(c + 1) * MXU_N], preferred_element_type=F32)
        for j in range(2):
            hd = 2 * c + j
            yh = y[:, j * LANES:(j + 1) * LANES]
            ms = jnp.mean(yh * yh, axis=-1, keepdims=True)
            yr = yh * lax.rsqrt(ms + EPS) * hg
            if use_rope:
                yr = _rope128(yr, cos_ref[...], sin_ref[...])
            if emit_f32:
                of_ref[:, hd * LANES:(hd + 1) * LANES] = yr
            yb = (yr * scale).astype(BF16)
            if head_major:
                ob_ref[hd] = yb
            else:
                ob_ref[:, hd * LANES:(hd + 1) * LANES] = yb


def _emit_plain(h, w_ref, aux, outs, *, n_cols):
    of_ref, ob_ref = outs
    for c in range(n_cols // MXU_N):
        sl = slice(c * MXU_N, (c + 1) * MXU_N)
        y = jnp.dot(h, w_ref[:, sl], preferred_element_type=F32)
        of_ref[:, sl] = y
        ob_ref[:, sl] = y.astype(BF16)


def _emit_gate(h, w_ref, aux, outs, *, n_cols):
    for c in range(n_cols // MXU_N):
        sl = slice(c * MXU_N, (c + 1) * MXU_N)
        y = jnp.dot(h, w_ref[:, sl], preferred_element_type=F32)
        outs[0][:, sl] = (y * (1.0 / (1.0 + jnp.exp(-y)))).astype(BF16)


def _emit_rope64(h, w_ref, aux, outs, *, n_cols):
    cos, sin = aux[0][...], aux[1][...]
    for c in range(n_cols // MXU_N):
        y = jnp.dot(h, w_ref[:, c * MXU_N:(c + 1) * MXU_N], preferred_element_type=F32)
        for j in range(2):
            col = c * MXU_N + j * LANES
            outs[0][:, col:col + LANES] = _rope64(y[:, j * LANES:(j + 1) * LANES], cos, sin).astype(BF16)


def _emit_idx_small(h, w_ref, aux, outs):
    cos_ref, sin_ref, hg_ref = aux
    y = jnp.dot(h, w_ref[...], preferred_element_type=F32)
    is_key = lax.broadcasted_iota(I32, y.shape, 1) < IDX_DIM
    ms = jnp.sum(jnp.where(is_key, y * y, 0.0), axis=-1, keepdims=True) * (1.0 / IDX_DIM)
    kn = _rope64(y * lax.rsqrt(ms + EPS) * hg_ref[...], cos_ref[...], sin_ref[...])
    outs[0][...] = jnp.where(is_key, kn, y)


def _emit_logf(h, w_ref, aux, outs):
    z = jnp.dot(h, w_ref[...], preferred_element_type=F32) + aux[0][...]
    outs[0][...] = jnp.minimum(z, 0.0) - jnp.log1p(jnp.exp(-jnp.abs(z)))


def _proj_kernel(x_ref, g_ref, *refs, segs):
    h = _norm_rows(x_ref, g_ref)
    n_in = sum(1 + len(s.aux) for s in segs)
    ins, outs = refs[:n_in], refs[n_in:]
    i = o = 0
    for s in segs:
        s.emit(h, ins[i], ins[i + 1:i + 1 + len(s.aux)], outs[o:o + len(s.out_shape)])
        i += 1 + len(s.aux)
        o += len(s.out_shape)


def _project(rows, x, g, segs, name):
    m, d = x.shape
    tm = rows.tm
    in_specs = [pl.BlockSpec((tm, d), lambda i: (i, 0)), pl.BlockSpec((1, d), lambda i: (0, 0))]
    args = [x, g.reshape(1, d)]
    for s in segs:
        block = s.w.block
        in_specs.append(pl.BlockSpec((d, s.w.n), lambda i, block=block: (0, block), pipeline_mode=pl.Buffered(1)))
        in_specs += s.aux_specs
        args += [s.w.array] + s.aux
    out = pl.pallas_call(
        functools.partial(_proj_kernel, segs=segs), grid=(m // tm,), in_specs=in_specs, name=name,
        out_specs=[sp for s in segs for sp in s.out_specs],
        out_shape=[sh for s in segs for sh in s.out_shape],
        compiler_params=_params(("parallel",)))(*args)
    res, o = [], 0
    for s in segs:
        res.append(out[o:o + len(s.out_shape)])
        o += len(s.out_shape)
    return res


def _seg_qk(rows, w, head_gain, *, scale, use_rope, head_major, emit_f32):
    n = w.n
    n_heads = n // HEAD_DIM
    cos, sin = rows.tables(HEAD_DIM)
    out_shape, out_specs = [], []
    if emit_f32:
        out_shape.append(jax.ShapeDtypeStruct((rows.m, n), F32))
        out_specs.append(_row_spec(rows.tm, n))
    if head_major:
        out_shape.append(jax.ShapeDtypeStruct((n_heads, rows.m, HEAD_DIM), BF16))
        out_specs.append(pl.BlockSpec((n_heads, rows.tm, HEAD_DIM), lambda i: (0, i, 0)))
    else:
        out_shape.append(jax.ShapeDtypeStruct((rows.m, n), BF16))
        out_specs.append(_row_spec(rows.tm, n))
    emit = functools.partial(_emit_qk, n_heads=n_heads, scale=scale, use_rope=use_rope,
                             head_major=head_major, emit_f32=emit_f32)
    return _Seg(emit, w, (cos, sin, head_gain.reshape(1, HEAD_DIM)),
                (rows.tab_spec(), rows.tab_spec(), _vec_spec(HEAD_DIM)), out_shape, out_specs)


def _seg_plain(rows, w):
    n = w.n
    return _Seg(functools.partial(_emit_plain, n_cols=n), w, (), (),
                [jax.ShapeDtypeStruct((rows.m, n), F32), jax.ShapeDtypeStruct((rows.m, n), BF16)],
                [_row_spec(rows.tm, n), _row_spec(rows.tm, n)])


def _seg_gate(rows, w):
    n = w.n
    return _Seg(functools.partial(_emit_gate, n_cols=n), w, (), (),
                [jax.ShapeDtypeStruct((rows.m, n), BF16)], [_row_spec(rows.tm, n)])


def _seg_rope64(rows, w):
    n = w.n
    cos, sin = rows.tables(IDX_DIM)
    return _Seg(functools.partial(_emit_rope64, n_cols=n), w, (cos, sin), (rows.tab_spec(), rows.tab_spec()),
                [jax.ShapeDtypeStruct((rows.m, n), BF16)], [_row_spec(rows.tm, n)])


def _seg_idx_small(rows, w, key_gain):
    cos, sin = rows.tables(IDX_DIM)
    gain = jnp.concatenate([key_gain, jnp.zeros((LANES - IDX_DIM,), F32)]).reshape(1, LANES)
    return _Seg(_emit_idx_small, w, (cos, sin, gain), (rows.tab_spec(), rows.tab_spec(), _vec_spec(LANES)),
                [jax.ShapeDtypeStruct((rows.m, LANES), F32)], [_row_spec(rows.tm, LANES)])


def _seg_logf(rows, w, fb):
    fbp = jnp.concatenate([fb, jnp.zeros((LANES - C_HEADS,), F32)]).reshape(1, LANES)
    return _Seg(_emit_logf, w, (fbp,), (_vec_spec(LANES),),
                [jax.ShapeDtypeStruct((rows.m, LANES), F32)], [_row_spec(rows.tm, LANES)])


def _out_kernel(x_ref, o_ref, sg_ref, w_ref, y_ref, *, n_cols):
    a = o_ref[...] * sg_ref[...]
    for c in range(n_cols // MXU_N):
        sl = slice(c * MXU_N, (c + 1) * MXU_N)
        y_ref[:, sl] = x_ref[:, sl] + jnp.dot(a, w_ref[:, sl], preferred_element_type=F32)


def _out_proj(rows, x, o, sg, w):
    m, d = x.shape
    tm = rows.tm
    kern = functools.partial(_out_kernel, n_cols=d)
    return pl.pallas_call(
        kern, grid=(m // tm,), name="out_proj",
        in_specs=[_row_spec(tm, d), _row_spec(tm, d), _row_spec(tm, d),
                  pl.BlockSpec((d, d), lambda i: (0, 0))],
        out_specs=_row_spec(tm, d), out_shape=jax.ShapeDtypeStruct((m, d), F32),
        compiler_params=_params(("parallel",)))(x, o, sg, w)


class _Attn:
    def __init__(self, batch, t_q, s_pad, s_valid, q_off, tq, tk):
        self.batch, self.t_q, self.s_pad, self.s_valid = batch, t_q, s_pad, s_valid
        self.q_off, self.tq, self.tk = q_off, tq, tk
        self.nq = t_q // tq


def _chunk_limits(i, cfg):
    q0 = cfg.q_off + i * cfg.tq
    qpos = q0 + lax.broadcasted_iota(I32, (cfg.tq, 1), 0)
    qend = jnp.minimum(((qpos >> CHUNK_SHIFT) + 1) << CHUNK_SHIFT, cfg.s_valid)
    kmin = jnp.minimum(((q0 >> CHUNK_SHIFT) + 1) << CHUNK_SHIFT, cfg.s_valid)
    kend = jnp.minimum((((q0 + cfg.tq - 1) >> CHUNK_SHIFT) + 1) << CHUNK_SHIFT, cfg.s_valid)
    return qend, kmin // cfg.tk, (kend + cfg.tk - 1) // cfg.tk


def _causal_limits(i, cfg):
    q0 = cfg.q_off + i * cfg.tq
    qpos = q0 + lax.broadcasted_iota(I32, (cfg.tq, 1), 0)
    return qpos + 1, (q0 + 1) // cfg.tk, (q0 + cfg.tq + cfg.tk - 1) // cfg.tk


def _qk(q, k):
    return lax.dot_general(q, k, (((1,), (1,)), ((), ())), preferred_element_type=F32)


def _visibility_bias(ks, qend, shape):
    kpos = ks + lax.broadcasted_iota(I32, shape, 1)
    return jnp.where(kpos < qend, 0.0, MASKED)


def _flash_step(qs, ks, vs, bias_fn, state, *, rows, strip):
    s_ref, p_ref, m_ref, l_ref, alpha_ref, acc_ref = state
    n = len(qs)
    strips = [slice(r * strip, (r + 1) * strip) for r in range(rows // strip)]
    chunks = [slice(a, a + LANES) for a in range(0, s_ref.shape[-1], LANES)]
    for c in range(n):
        s_ref[c] = _qk(qs[c], ks[c])
    for c in range(n):
        for r, rs in enumerate(strips):
            mx = None
            for ch in chunks:
                s = s_ref[c, rs, ch]
                b = bias_fn(c, r, ch)
                if b is not None:
                    s = s + b
                    s_ref[c, rs, ch] = s
                mx = s if mx is None else jnp.maximum(mx, s)
            m_old = m_ref[c, rs, :]
            m_new = jnp.maximum(m_old, jnp.broadcast_to(jnp.max(mx, axis=-1, keepdims=True), mx.shape))
            alpha_ref[c, rs, :] = jnp.exp2(m_old - m_new)
            m_ref[c, rs, :] = m_new
        for rs in strips:
            m = m_ref[c, rs, :]
            psum = None
            for ch in chunks:
                p = jnp.exp2(s_ref[c, rs, ch] - m)
                if l_ref is not None:
                    psum = p if psum is None else psum + p
                p_ref[c, rs, ch] = p.astype(BF16)
            if l_ref is not None:
                l_ref[c, rs, :] = alpha_ref[c, rs, :] * l_ref[c, rs, :] + psum
    for c in range(n):
        alpha = jnp.concatenate([alpha_ref[c]] * (acc_ref.shape[-1] // LANES), axis=1)
        acc_ref[c] = alpha * acc_ref[c] + jnp.dot(p_ref[c], vs[c], preferred_element_type=F32)


def _init_state(state):
    _, _, m_ref, l_ref, _, acc_ref = state
    m_ref[...] = jnp.full(m_ref.shape, MASKED, F32)
    acc_ref[...] = jnp.zeros(acc_ref.shape, F32)
    if l_ref is not None:
        l_ref[...] = jnp.zeros(l_ref.shape, F32)


def _flash_scratch(chains, rows, tk, dv, with_l):
    stat = pltpu.VMEM((chains, rows, LANES), F32)
    return ([pltpu.VMEM((chains, rows, tk), F32), pltpu.VMEM((chains, rows, tk), BF16), stat]
            + ([stat] if with_l else []) + [stat, pltpu.VMEM((chains, rows, dv), F32)])


def _with_ones(v):
    return jnp.concatenate([v, jnp.ones(v.shape, v.dtype)], axis=1)


def _dsa_kernel(q_ref, qi_ref, sm_ref, k_ref, v_ref, kit_ref, o_ref,
                key_ref, bias_ref, stat_ref, look_ref, s_ref, p_ref, m_ref, alpha_ref, acc_ref, *, cfg, topk):
    tq, tk = cfg.tq, cfg.tk
    i = pl.program_id(1)
    qend, _, n_kv = _chunk_limits(i, cfg)
    sub = min(tq, SCORE_ROWS)
    stat_ref[0] = jnp.full((tq, LANES), INT_MIN, I32)
    stat_ref[1] = jnp.zeros((tq, LANES), I32)

    def score_block(j, carry):
        ks = pl.multiple_of(j * tk, tk)
        for r0 in range(0, tq, sub):
            rs = slice(r0, r0 + sub)
            w = sm_ref[rs, IDX_DIM:IDX_DIM + IDX_HEADS] * (IDX_DIM ** -0.5 * IDX_HEADS ** -0.5)
            acc = jnp.zeros((sub, tk), F32)
            for pair in range(IDX_HEADS // 2):
                qp = qi_ref[rs, pair * LANES:(pair + 1) * LANES]
                for e in range(2):
                    hd = 2 * pair + e
                    sc = jnp.dot(qp, kit_ref[e, :, pl.ds(ks, tk)], preferred_element_type=F32)
                    acc = acc + w[:, hd:hd + 1] * jnp.maximum(sc, 0.0)
            bits = lax.bitcast_convert_type(acc, I32)
            key = bits ^ ((bits >> 31) & 0x7FFFFFFF)
            key = jnp.where(acc == 0.0, 0, key)
            kpos = ks + lax.broadcasted_iota(I32, (sub, tk), 1)
            key = jnp.where(kpos < qend[rs], key, NEG_INF_KEY)
            key_ref[rs, pl.ds(ks, tk)] = key
            kmax, nfin = stat_ref[0, rs, :], stat_ref[1, rs, :]
            for c in range(tk // LANES):
                kc = key[:, c * LANES:(c + 1) * LANES]
                kmax = jnp.maximum(kmax, kc)
                nfin = nfin + jnp.where(kc > NEG_INF_KEY, 1, 0)
            stat_ref[0, rs, :], stat_ref[1, rs, :] = kmax, nfin
        return carry

    lax.fori_loop(0, n_kv, score_block, 0)

    def key_chunk(j, c):
        return key_ref[:, pl.ds(pl.multiple_of(j * tk + c * LANES, LANES), LANES)]

    def lane_total(x):
        return jnp.broadcast_to(jnp.sum(x, axis=-1, keepdims=True), x.shape)

    def count_ge(cand):
        def body(j, part):
            ks = pl.multiple_of(j * tk, tk)
            ge = jnp.where(key_ref[:, pl.ds(ks, tk)] >= cand, 1, 0)
            for c in range(tk // LANES):
                part = part + ge[:, c * LANES:(c + 1) * LANES]
            return part
        part = lax.fori_loop(0, n_kv, body, jnp.zeros((tq, LANES), I32))
        return jnp.sum(part, axis=-1, keepdims=True)

    n_fin = lane_total(stat_ref[1])
    small = n_fin <= topk
    look_ref[0] = jnp.full((tq, LANES), NEG_INF_KEY + 1, I32)
    look_ref[1] = jnp.broadcast_to(jnp.max(stat_ref[0], axis=-1, keepdims=True), (tq, LANES)) + 1
    look_ref[2] = n_fin

    def pending():
        lo, hi, cnt = look_ref[0], look_ref[1], look_ref[2]
        open_row = (cnt != topk) & (hi != lo + 1) & jnp.logical_not(small)
        return jnp.max(jnp.where(open_row, 1, 0))

    groups = [slice(r, r + min(tq, SEARCH_ROWS)) for r in range(0, tq, min(tq, SEARCH_ROWS))]

    def halve(n_blocks):
        state = [[look_ref[v, gs, :] for v in range(3)] for gs in groups]
        part = jnp.zeros((groups[0].stop, LANES), I32)
        for _ in range(BISECT_UNROLL):
            for g, gs in enumerate(groups):
                lo, hi, cnt = state[g]
                mid = (lo >> 1) + (hi >> 1) + (lo & hi & 1)
                part = part >> 31
                for a in range(0, n_blocks * tk, LANES):
                    part = part + jnp.where(key_ref[gs, a:a + LANES] >= mid, 1, 0)
                c = lane_total(part)
                up = c >= topk
                state[g] = [jnp.where(up, mid, lo), jnp.where(up, hi, mid), jnp.where(up, c, cnt)]
        for g, gs in enumerate(groups):
            for v in range(3):
                look_ref[v, gs, :] = state[g][v]

    def search_round(_):
        for n_blocks in range(1, cfg.s_pad // tk + 1):
            pl.when(n_kv == n_blocks)(functools.partial(halve, n_blocks))
        return pending()

    lax.while_loop(lambda flag: flag > 0, search_round, pending())
    thr, n_ge = look_ref[0], look_ref[2]
    tie_rows = (n_ge > topk) & jnp.logical_not(small)

    @pl.when(jnp.max(jnp.where(tie_rows, 1, 0)) > 0)
    def _drop_late_ties():
        need_f = jnp.broadcast_to((topk - count_ge(thr[:, :1] + 1)).astype(F32), (tq, LANES))
        upper = jnp.where(lax.broadcasted_iota(I32, (LANES, LANES), 0) <= lax.broadcasted_iota(I32, (LANES, LANES), 1),
                          1.0, 0.0).astype(BF16)

        def body(j, seen):
            for c in range(tk // LANES):
                kb = key_chunk(j, c)
                eq = kb == thr
                eq_f = jnp.where(eq, 1.0, 0.0)
                rank = seen + jnp.dot(eq_f.astype(BF16), upper, preferred_element_type=F32)
                drop = eq & (rank > need_f) & tie_rows
                key_ref[:, pl.ds(pl.multiple_of(j * tk + c * LANES, LANES), LANES)] = jnp.where(drop, thr - 1, kb)
                seen = seen + lane_total(eq_f)
            return seen

        lax.fori_loop(0, n_kv, body, jnp.zeros((tq, LANES), F32))

    thr_keep = jnp.where(small, NEG_INF_KEY + 1, jnp.maximum(thr, NEG_INF_KEY + 1))
    state = (s_ref, p_ref, m_ref, None, alpha_ref, acc_ref)
    _init_state(state)
    strip = min(tq, STRIP_ROWS)
    strips_per_head = tq // strip

    def attn_block(j, carry):
        ks = pl.multiple_of(j * tk, tk)
        for c in range(tk // LANES):
            bias_ref[:, c * LANES:(c + 1) * LANES] = jnp.where(key_chunk(j, c) >= thr_keep, 0.0, MASKED)
        cols =[slice(g * HEAD_DIM, (g + 1) * HEAD_DIM) for g in range(A_KV_HEADS)]
        _flash_step([q_ref[g * A_REP:(g + 1) * A_REP].reshape(A_REP * tq, HEAD_DIM) for g in range(A_KV_HEADS)],
                    [k_ref[pl.ds(ks, tk), c] for c in cols],
                    [_with_ones(v_ref[pl.ds(ks, tk), c]) for c in cols],
                    lambda c, r, ch: bias_ref[(r % strips_per_head) * strip:(r % strips_per_head + 1) * strip, ch],
                    state, rows=A_REP * tq, strip=strip)
        return carry

    lax.fori_loop(0, n_kv, attn_block, 0)
    for g in range(A_KV_HEADS):
        o = acc_ref[g, :, :HEAD_DIM] / acc_ref[g, :, HEAD_DIM:]
        for r in range(A_REP):
            hd = g * A_REP + r
            o_ref[:, hd * HEAD_DIM:(hd + 1) * HEAD_DIM] = o[r * tq:(r + 1) * tq].astype(BF16)


def _dsa_attention(cfg, q_hm, q_row0, qi, sm, k, v, kit, topk):
    tq, nq, s = cfg.tq, cfg.nq, cfg.s_pad
    qb0 = q_row0 // tq
    kern = functools.partial(_dsa_kernel, cfg=cfg, topk=topk)
    return pl.pallas_call(
        kern, grid=(cfg.batch, nq), name="dsa_attention",
        in_specs=[pl.BlockSpec((A_HEADS, tq, HEAD_DIM), lambda b, i: (0, qb0 + b * nq + i, 0)),
                  pl.BlockSpec((tq, IDX_HEADS * IDX_DIM), lambda b, i: (qb0 + b * nq + i, 0)),
                  pl.BlockSpec((tq, LANES), lambda b, i: (qb0 + b * nq + i, 0)),
                  pl.BlockSpec((s, A_KV_HEADS * HEAD_DIM), lambda b, i: (b, 0), pipeline_mode=pl.Buffered(1)),
                  pl.BlockSpec((s, A_KV_HEADS * HEAD_DIM), lambda b, i: (b, 0), pipeline_mode=pl.Buffered(1)),
                  pl.BlockSpec((None, 2, LANES, s), lambda b, i: (b, 0, 0, 0), pipeline_mode=pl.Buffered(1))],
        out_specs=pl.BlockSpec((tq, MIX_WIDTH), lambda b, i: (b * nq + i, 0)),
        out_shape=jax.ShapeDtypeStruct((cfg.batch * cfg.t_q, MIX_WIDTH), BF16),
        scratch_shapes=[pltpu.VMEM((tq, s), I32), pltpu.VMEM((tq, cfg.tk), F32), pltpu.VMEM((2, tq, LANES), I32),
                        pltpu.VMEM((3, tq, LANES), I32)]
        + _flash_scratch(A_KV_HEADS, A_REP * tq, cfg.tk, 2 * HEAD_DIM, with_l=False),
        compiler_params=_params(("parallel", "arbitrary")))(q_hm, qi, sm, k, v, kit)


B_STEP_HEADS = 2


def _run_blocks(n_full, n_kv, step):
    def plain(j, carry):
        step(j, False)
        return carry

    def masked(j, carry):
        step(j, True)
        return carry

    lax.fori_loop(0, n_full, plain, 0)
    lax.fori_loop(n_full, n_kv, masked, 0)


def _diff_kernel(q_ref, k_ref, v_ref, lq1_ref, lk1_ref, lq2_ref, lk2_ref, sub_ref, o_ref,
                 bias_ref, s_ref, p_ref, m_ref, l_ref, alpha_ref, acc_ref, *, cfg, lam_init):
    tq, tk = cfg.tq, cfg.tk
    qend, n_full, n_kv = _chunk_limits(pl.program_id(2), cfg)
    state = (s_ref, p_ref, m_ref, l_ref, alpha_ref, acc_ref)
    _init_state(state)
    strip = min(tq, STRIP_ROWS)
    n_chains = 2 * B_STEP_HEADS

    def step(j, masked):
        ks = pl.multiple_of(j * tk, tk)
        if masked:
            bias_ref[...] = _visibility_bias(ks, qend, (tq, tk))
        qk_cols = [slice(c * HEAD_DIM, (c + 1) * HEAD_DIM) for c in range(n_chains)]
        v_cols = [slice((c // 2) * B_V_DIM, (c // 2 + 1) * B_V_DIM) for c in range(n_chains)]
        _flash_step([q_ref[:, c] for c in qk_cols],
                    [k_ref[pl.ds(ks, tk), c] for c in qk_cols],
                    [v_ref[pl.ds(ks, tk), c] for c in v_cols],
                    (lambda c, r, ch: bias_ref[r * strip:(r + 1) * strip, ch]) if masked
                    else (lambda c, r, ch: None),
                    state, rows=tq, strip=strip)

    _run_blocks(n_full, n_kv, step)
    lam = (jnp.exp(jnp.sum(lq1_ref[...] * lk1_ref[...], axis=-1, keepdims=True))
           - jnp.exp(jnp.sum(lq2_ref[...] * lk2_ref[...], axis=-1, keepdims=True)) + lam_init)
    def normalised(c):
        return acc_ref[c] / jnp.sum(l_ref[c], axis=-1, keepdims=True)

    for h in range(B_STEP_HEADS):
        o = normalised(2 * h) - lam * normalised(2 * h + 1)
        ms = jnp.mean(o * o, axis=-1, keepdims=True)
        o_ref[:, h * B_V_DIM:(h + 1) * B_V_DIM] = (
            o * lax.rsqrt(ms + EPS) * sub_ref[...] * (1.0 - lam_init)).astype(BF16)


def _diff_attention(cfg, q, q_row0, k, v, lam_vecs, subln, lam_init):
    tq, nq, s = cfg.tq, cfg.nq, cfg.s_pad
    qb0 = q_row0 // tq
    width = B_STEP_HEADS * B_V_DIM
    kern = functools.partial(_diff_kernel, cfg=cfg, lam_init=lam_init)
    vec = pl.BlockSpec((1, HEAD_DIM), lambda b, h, i: (0, 0))
    return pl.pallas_call(
        kern, grid=(cfg.batch, B_HEADS // B_STEP_HEADS, nq), name="diff_attention",
        in_specs=[pl.BlockSpec((tq, width), lambda b, h, i: (qb0 + b * nq + i, h)),
                  pl.BlockSpec((s, width), lambda b, h, i: (b, h)),
                  pl.BlockSpec((s, width), lambda b, h, i: (b, h)),
                  vec, vec, vec, vec,
                  pl.BlockSpec((1, B_V_DIM), lambda b, h, i: (0, 0))],
        out_specs=pl.BlockSpec((tq, width), lambda b, h, i: (b * nq + i, h)),
        out_shape=jax.ShapeDtypeStruct((cfg.batch * cfg.t_q, MIX_WIDTH), BF16),
        scratch_shapes=[pltpu.VMEM((tq, cfg.tk), F32)]
        + _flash_scratch(2 * B_STEP_HEADS, tq, cfg.tk, B_V_DIM, with_l=True),
        compiler_params=_params(("parallel", "parallel", "arbitrary")))(
            q, k, v, *[a.reshape(1, HEAD_DIM) for a in lam_vecs], subln.reshape(1, B_V_DIM))


C_STEP_HEADS = 4


def _fox_kernel(q_ref, k_ref, v_ref, cq_ref, ck_ref, o_ref,
                bias_ref, cqh_ref, s_ref, p_ref, m_ref, alpha_ref, acc_ref, *, cfg):
    tq, tk = cfg.tq, cfg.tk
    hg = pl.program_id(1)
    qend, n_full, n_kv = _causal_limits(pl.program_id(2), cfg)
    state = (s_ref, p_ref, m_ref, None, alpha_ref, acc_ref)
    _init_state(state)
    strip = min(tq, STRIP_ROWS)
    lane = lax.broadcasted_iota(I32, (tq, LANES), 1)
    for c in range(C_STEP_HEADS):
        head_lane = lane == hg * C_STEP_HEADS + c
        cqh_ref[c] = jnp.broadcast_to(
            jnp.sum(jnp.where(head_lane, cq_ref[...], 0.0), axis=-1, keepdims=True), (tq, LANES))

    def step(j, masked):
        ks = pl.multiple_of(j * tk, tk)
        if masked:
            bias_ref[...] = _visibility_bias(ks, qend, (tq, tk))

        def bias(c, r, ch):
            rs = slice(r * strip, (r + 1) * strip)
            b = cqh_ref[c, rs, :] - ck_ref[c:c + 1, pl.ds(pl.multiple_of(ks + ch.start, LANES), LANES)]
            return b + bias_ref[rs, ch] if masked else b

        cols = [slice(c * HEAD_DIM, (c + 1) * HEAD_DIM) for c in range(C_STEP_HEADS)]
        _flash_step([q_ref[:, c] for c in cols],
                    [k_ref[pl.ds(ks, tk), c] for c in cols],
                    [_with_ones(v_ref[pl.ds(ks, tk), c]) for c in cols],
                    bias, state, rows=tq, strip=strip)

    _run_blocks(n_full, n_kv, step)
    for c in range(C_STEP_HEADS):
        o_ref[:, c * HEAD_DIM:(c + 1) * HEAD_DIM] = (
            acc_ref[c, :, :HEAD_DIM] / acc_ref[c, :, HEAD_DIM:]).astype(BF16)


def _fox_attention(cfg, q, q_row0, k, v, cq, ck_rows):
    tq, nq, s = cfg.tq, cfg.nq, cfg.s_pad
    qb0 = q_row0 // tq
    width = C_STEP_HEADS * HEAD_DIM
    groups = C_HEADS // C_STEP_HEADS
    kern = functools.partial(_fox_kernel, cfg=cfg)
    return pl.pallas_call(
        kern, grid=(cfg.batch, groups, nq), name="fox_attention",
        in_specs=[pl.BlockSpec((tq, width), lambda b, h, i: (qb0 + b * nq + i, h)),
                  pl.BlockSpec((s, width), lambda b, h, i: (b, h)),
                  pl.BlockSpec((s, width), lambda b, h, i: (b, h)),
                  pl.BlockSpec((tq, LANES), lambda b, h, i: (qb0 + b * nq + i, 0)),
                  pl.BlockSpec((None, C_STEP_HEADS, s), lambda b, h, i: (b * groups + h, 0, 0))],
        out_specs=pl.BlockSpec((tq, width), lambda b, h, i: (b * nq + i, h)),
        out_shape=jax.ShapeDtypeStruct((cfg.batch * cfg.t_q, MIX_WIDTH), BF16),
        scratch_shapes=[pltpu.VMEM((tq, cfg.tk), F32), pltpu.VMEM((C_STEP_HEADS, tq, LANES), F32)]
        + _flash_scratch(C_STEP_HEADS, tq, cfg.tk, 2 * HEAD_DIM, with_l=False),
        compiler_params=_params(("parallel", "parallel", "arbitrary")))(q, k, v, cq, ck_rows)


def _cumsum_kernel(x_ref, o_ref, carry_ref, *, tb):
    @pl.when(pl.program_id(1) == 0)
    def _():
        carry_ref[...] = jnp.zeros(carry_ref.shape, F32)

    x = x_ref[...]
    hi = x.astype(BF16)
    r1 = x - hi.astype(F32)
    mid = r1.astype(BF16)
    low = (r1 - mid.astype(F32)).astype(BF16)
    tri = jnp.where(lax.broadcasted_iota(I32, (tb, tb), 0) >= lax.broadcasted_iota(I32, (tb, tb), 1),
                    1.0, 0.0).astype(BF16)
    c = (jnp.dot(tri, hi, preferred_element_type=F32) + jnp.dot(tri, mid, preferred_element_type=F32)
         + jnp.dot(tri, low, preferred_element_type=F32)) + carry_ref[...]
    o_ref[...] = c * LOG2E
    carry_ref[...] = c[tb - 1:tb, :]


def _cumsum_rows(x, batch, t, tb):
    nt = t // tb
    return pl.pallas_call(
        functools.partial(_cumsum_kernel, tb=tb), grid=(batch, nt), name="cumsum_rows",
        in_specs=[pl.BlockSpec((tb, LANES), lambda b, j: (b * nt + j, 0))],
        out_specs=pl.BlockSpec((tb, LANES), lambda b, j: (b * nt + j, 0)),
        out_shape=jax.ShapeDtypeStruct((batch * t, LANES), F32),
        scratch_shapes=[pltpu.VMEM((1, LANES), F32)],
        compiler_params=_params(("parallel", "arbitrary")))(x)


def _split_w(w_in, sizes):
    offs = np.cumsum((0,) + tuple(sizes))
    return [w_in[:, int(offs[i]):int(offs[i + 1])] for i in range(len(sizes))]


def _pad_rows(a, rows):
    return jnp.pad(a, ((0, 0), (0, rows - a.shape[1])) + ((0, 0),) * (a.ndim - 2))


def _sample_keys(cache, new, s_pad):
    b = cache.shape[0]
    full = jnp.concatenate([cache.reshape(b, cache.shape[1], -1), new.reshape(b, new.shape[1], -1)], axis=1)
    return _pad_rows(full, s_pad).astype(BF16).reshape(b * s_pad, -1)


def _key_transposed_pairs(ki, batch, s):
    kt = jnp.swapaxes(ki.reshape(batch, s, IDX_DIM), 1, 2).astype(BF16)
    z = jnp.zeros_like(kt)
    return jnp.stack([jnp.concatenate([kt, z], axis=1), jnp.concatenate([z, kt], axis=1)], axis=1)


def _layer_a(pr, sr, cfg_p, cfg_s, xp, xs, ck, cv, cki, norm, w_in, w_out, qn, kn, ikn):
    w_all = w_in.astype(BF16)
    wq, wk, wv, wqi = _Cols(w_all, 2048, 0), _Cols(w_all, 512, 4), _Cols(w_all, 512, 5), _Cols(w_all, 1024, 3)
    wsm = _Cols(w_all, LANES, sum(A_SIZES[:4]) // LANES)
    wg = _Cols(_split_w(w_in, A_SIZES)[6].astype(BF16), MIX_WIDTH)
    wo = w_out.astype(BF16)

    def project(rows, x):
        (q,), (qi,), (sm,) = _project(rows, x, norm, [
            _seg_qk(rows, wq, qn, scale=HEAD_DIM ** -0.5 * LOG2E, use_rope=True, head_major=True, emit_f32=False),
            _seg_rope64(rows, wqi), _seg_idx_small(rows, wsm, ikn)], "proj_a_query")
        (k_f, k_b), (v_f, v_b), (sg,) = _project(rows, x, norm, [
            _seg_qk(rows, wk, kn, scale=1.0, use_rope=True, head_major=False, emit_f32=True),
            _seg_plain(rows, wv), _seg_gate(rows, wg)], "proj_a_kv_gate")
        return q, k_f, k_b, v_f, v_b, qi, sm, sg

    q, k_f, k_b, v_f, v_b, qi, sm, sg = project(pr, xp)
    ki_f = sm[:, :IDX_DIM]
    o = _dsa_attention(cfg_p, q, 0, qi, sm, k_b, v_b,
                       _key_transposed_pairs(ki_f, pr.batch, pr.t_q), min(TOPK_MAX, pr.t_q // 4))
    yp = _out_proj(pr, xp, o, sg, wo)

    q2, k2_f, _, v2_f, _, qi2, sm2, sg2 = project(sr, xs)
    ki2_f = sm2[:, :IDX_DIM]
    b, s = sr.batch, sr.t_q
    k_all = _sample_keys(ck, k2_f.reshape(b, s, -1), cfg_s.s_pad)
    v_all = _sample_keys(cv, v2_f.reshape(b, s, -1), cfg_s.s_pad)
    ki_all = _pad_rows(jnp.concatenate([cki, ki2_f.reshape(b, s, IDX_DIM)], axis=1), cfg_s.s_pad)
    o2 = _dsa_attention(cfg_s, q2, 0, qi2, sm2, k_all, v_all,
                        _key_transposed_pairs(ki_all.reshape(-1, IDX_DIM), b, cfg_s.s_pad),
                        min(TOPK_MAX, cfg_s.s_valid // 4))
    ys = _out_proj(sr, xs, o2, sg2, wo)
    kv = (A_KV_HEADS, HEAD_DIM)
    state = (k_f.reshape(pr.batch, pr.t_q, *kv), v_f.reshape(pr.batch, pr.t_q, *kv),
             ki_f.reshape(pr.batch, pr.t_q, IDX_DIM),
             k2_f.reshape(b, s, *kv), v2_f.reshape(b, s, *kv), ki2_f.reshape(b, s, IDX_DIM))
    return yp, ys, state


def _layer_b(layer, pr, sr, cfg_p, cfg_s, xp, xs, ck, cv, norm, w_in, w_out, qn, kn,
             lq1, lk1, lq2, lk2, subln):
    lam_init = 0.8 - 0.6 * float(np.exp(-0.3 * layer))
    w_all = w_in.astype(BF16)
    wq, wk, wv, wg = [_Cols(w_all, 2048, j) for j in range(4)]
    wo = w_out.astype(BF16)

    def project(rows, x):
        (q,), (sg,) = _project(rows, x, norm, [
            _seg_qk(rows, wq, qn, scale=HEAD_DIM ** -0.5 * LOG2E, use_rope=True, head_major=False, emit_f32=False),
            _seg_gate(rows, wg)], "proj_b_query_gate")
        (k_f, k_b), (v_f, v_b) = _project(rows, x, norm, [
            _seg_qk(rows, wk, kn, scale=1.0, use_rope=True, head_major=False, emit_f32=True),
            _seg_plain(rows, wv)], "proj_b_kv")
        return q, k_f, k_b, v_f, v_b, sg

    lam_vecs = (lq1, lk1, lq2, lk2)
    q, k_f, k_b, v_f, v_b, sg = project(pr, xp)
    o = _diff_attention(cfg_p, q, 0, k_b, v_b, lam_vecs, subln, lam_init)
    yp = _out_proj(pr, xp, o, sg, wo)

    q2, k2_f, _, v2_f, _, sg2 = project(sr, xs)
    b, s = sr.batch, sr.t_q
    k_all = _sample_keys(ck, k2_f.reshape(b, s, -1), cfg_s.s_pad)
    v_all = _sample_keys(cv, v2_f.reshape(b, s, -1), cfg_s.s_pad)
    o2 = _diff_attention(cfg_s, q2, 0, k_all, v_all, lam_vecs, subln, lam_init)
    ys = _out_proj(sr, xs, o2, sg2, wo)
    state = (k_f.reshape(pr.batch, pr.t_q, 2 * B_HEADS, HEAD_DIM), v_f.reshape(pr.batch, pr.t_q, B_HEADS, B_V_DIM),
             k2_f.reshape(b, s, 2 * B_HEADS, HEAD_DIM), v2_f.reshape(b, s, B_HEADS, B_V_DIM))
    return yp, ys, state


def _head_rows(c, batch, s):
    rows = jnp.swapaxes(c.reshape(batch, s, LANES)[:, :, :C_HEADS], 1, 2)
    return rows.reshape(batch * C_HEADS // C_STEP_HEADS, C_STEP_HEADS, s)


def _layer_c(pr, sr, cfg_p, cfg_s, xp, xs, ck, cv, clogf, norm, w_in, w_out, qn, kn, fb):
    w_all = w_in.astype(BF16)
    wq, wk, wv = [_Cols(w_all, 2048, j) for j in range(3)]
    wf = _Cols(w_all, LANES, sum(C_SIZES[:3]) // LANES)
    wg = _Cols(_split_w(w_in, C_SIZES)[4].astype(BF16), MIX_WIDTH)
    wo = w_out.astype(BF16)

    def project(rows, x):
        (q,), (logf,), (sg,) = _project(rows, x, norm, [
            _seg_qk(rows, wq, qn, scale=HEAD_DIM ** -0.5 * LOG2E, use_rope=False, head_major=False, emit_f32=False),
            _seg_logf(rows, wf, fb), _seg_gate(rows, wg)], "proj_c_query_gate")
        (k_f, k_b), (v_f, v_b) = _project(rows, x, norm, [
            _seg_qk(rows, wk, kn, scale=1.0, use_rope=False, head_major=False, emit_f32=True),
            _seg_plain(rows, wv)], "proj_c_kv")
        return q, k_f, k_b, v_f, v_b, logf, sg

    q, k_f, k_b, v_f, v_b, logf, sg = project(pr, xp)
    c = _cumsum_rows(logf, pr.batch, pr.t_q, 512)
    o = _fox_attention(cfg_p, q, 0, k_b, v_b, c, _head_rows(c, pr.batch, pr.t_q))
    yp = _out_proj(pr, xp, o, sg, wo)

    q2, k2_f, _, v2_f, _, logf2, sg2 = project(sr, xs)
    b, s = sr.batch, sr.t_q
    k_all = _sample_keys(ck, k2_f.reshape(b, s, -1), cfg_s.s_pad)
    v_all = _sample_keys(cv, v2_f.reshape(b, s, -1), cfg_s.s_pad)
    logf_all = jnp.concatenate([jnp.pad(clogf.astype(F32), ((0, 0), (0, 0), (0, LANES - C_HEADS))),
                                logf2.reshape(b, s, LANES)], axis=1)
    c2 = _cumsum_rows(_pad_rows(logf_all, cfg_s.s_pad).reshape(b * cfg_s.s_pad, LANES), b, cfg_s.s_pad, LANES)
    cq2 = c2.reshape(b, cfg_s.s_pad, LANES)[:, PAST_LEN:PAST_LEN + s].reshape(b * s, LANES)
    o2 = _fox_attention(cfg_s, q2, 0, k_all, v_all, cq2, _head_rows(c2, b, cfg_s.s_pad))
    ys = _out_proj(sr, xs, o2, sg2, wo)
    hd = (C_HEADS, HEAD_DIM)
    state = (k_f.reshape(pr.batch, pr.t_q, *hd), v_f.reshape(pr.batch, pr.t_q, *hd),
             logf[:, :C_HEADS].reshape(pr.batch, pr.t_q, C_HEADS),
             k2_f.reshape(b, s, *hd), v2_f.reshape(b, s, *hd), logf2[:, :C_HEADS].reshape(b, s, C_HEADS))
    return yp, ys, state


def kernel(x_prompt, x_sample, cache_l0_k, cache_l0_v, cache_l0_kidx, cache_l1_k, cache_l1_v, cache_l2_k, cache_l2_v, cache_l2_logf, cache_l3_k, cache_l3_v, cache_l3_kidx, l0_norm, l0_w_in, l0_w_out, l0_q_norm, l0_k_norm, l0_idx_k_norm, l1_norm, l1_w_in, l1_w_out, l1_q_norm, l1_k_norm, l1_lambda_q1, l1_lambda_k1, l1_lambda_q2, l1_lambda_k2, l1_subln, l2_norm, l2_w_in, l2_w_out, l2_q_norm, l2_k_norm, l2_forget_bias, l3_norm, l3_w_in, l3_w_out, l3_q_norm, l3_k_norm, l3_idx_k_norm):
    bp, tp, d = x_prompt.shape
    bs, ts, _ = x_sample.shape
    s_valid = PAST_LEN + ts
    s_pad = -(-s_valid // LANES) * LANES
    pr = _Rows(bp, tp, 0, 512)
    sr = _Rows(bs, ts, PAST_LEN, bs * ts)
    a_p = _Attn(bp, tp, tp, tp, 0, 128, 512)
    a_s = _Attn(bs, ts, s_pad, s_valid, PAST_LEN, ts, s_pad)
    bc_p = _Attn(bp, tp, tp, tp, 0, 512, 512)
    bc_s = a_s

    xp = x_prompt.reshape(bp * tp, d)
    xs = x_sample.reshape(bs * ts, d)
    xp, xs, st0 = _layer_a(pr, sr, a_p, a_s, xp, xs, cache_l0_k, cache_l0_v, cache_l0_kidx,
                           l0_norm, l0_w_in, l0_w_out, l0_q_norm, l0_k_norm, l0_idx_k_norm)
    xp, xs, st1 = _layer_b(1, pr, sr, bc_p, bc_s, xp, xs, cache_l1_k, cache_l1_v,
                           l1_norm, l1_w_in, l1_w_out, l1_q_norm, l1_k_norm,
                           l1_lambda_q1, l1_lambda_k1, l1_lambda_q2, l1_lambda_k2, l1_subln)
    xp, xs, st2 = _layer_c(pr, sr, bc_p, bc_s, xp, xs, cache_l2_k, cache_l2_v, cache_l2_logf,
                           l2_norm, l2_w_in, l2_w_out, l2_q_norm, l2_k_norm, l2_forget_bias)
    xp, xs, st3 = _layer_a(pr, sr, a_p, a_s, xp, xs, cache_l3_k, cache_l3_v, cache_l3_kidx,
                           l3_norm, l3_w_in, l3_w_out, l3_q_norm, l3_k_norm, l3_idx_k_norm)
    return (xp.reshape(bp, tp, d), xs.reshape(bs, ts, d)) + st0 + st1 + st2 + st3
```

```python
import functools

import numpy as np
import jax
import jax.numpy as jnp
from jax import lax
from jax.experimental import pallas as pl
from jax.experimental.pallas import tpu as pltpu

F32 = jnp.float32
BF16 = jnp.bfloat16
I32 = jnp.int32

D_MODEL = 2048
PAST_LEN = 1024
CHUNK_SHIFT = 6
ROPE_THETA = 10000.0
EPS = 1e-6
HEAD_DIM = 128
A_HEADS = 16
A_KV_HEADS = 4
A_REP = A_HEADS // A_KV_HEADS
IDX_HEADS = 16
IDX_DIM = 64
TOPK_MAX = 256
B_HEADS = 8
B_V_DIM = 256
C_HEADS = 16
MIX_WIDTH = D_MODEL

A_SIZES = (2048, 512, 512, 1024, 64, 16, 2048)
B_SIZES = (2048, 2048, 2048, 2048)
C_SIZES = (2048, 2048, 2048, 16, 2048)

LANES = 128
MXU_N = 256
VMEM_LIMIT = 56 * 1024 * 1024
MASKED = -1e30
LOG2E = 1.4426950408889634
STRIP_ROWS = 32
BISECT_UNROLL = 3
SEARCH_ROWS = 32
INT_MIN = -2 ** 31
SCORE_ROWS = 128
NEG_INF_KEY = -2139095041


def _params(sem):
    return pltpu.CompilerParams(dimension_semantics=sem, vmem_limit_bytes=VMEM_LIMIT)


def _norm_rows(x_ref, g_ref):
    x = x_ref[...]
    ms = jnp.mean(x * x, axis=-1, keepdims=True)
    return (x * lax.rsqrt(ms + EPS) * g_ref[...]).astype(BF16)


def _rope128(y, cos, sin):
    return y * cos + pltpu.roll(y, 64, 1) * sin


def _rope64(y, cos, sin):
    lane = lax.broadcasted_iota(I32, y.shape, 1)
    rot = jnp.where((lane & 63) < 32, pltpu.roll(y, 96, 1), pltpu.roll(y, 32, 1))
    return y * cos + rot * sin


class _Cols:
    def __init__(self, array, n, block=0):
        assert array.shape[1] >= (block + 1) * n
        self.array, self.n, self.block = array, n, block


class _Rows:
    def __init__(self, batch, t_q, pos0, tm):
        self.batch, self.t_q, self.pos0, self.tm = batch, t_q, pos0, tm
        self.m = batch * t_q
        self.tab_blocks = max(t_q // tm, 1)

    def tab_spec(self):
        nb = self.tab_blocks
        return pl.BlockSpec((self.tm, LANES), lambda i: (i % nb, 0))

    def tables(self, d):
        period = max(self.t_q, self.tm)
        pos = self.pos0 + (jnp.arange(period) % self.t_q)
        half = d // 2
        inv = ROPE_THETA ** (-2.0 * jnp.arange(half, dtype=F32) / d)
        ang = pos.astype(F32)[:, None] * inv[None, :]
        cos, sin = jnp.cos(ang), jnp.sin(ang)
        reps = LANES // d
        cos_f = jnp.tile(jnp.concatenate([cos, cos], axis=-1), (1, reps))
        sin_f = jnp.tile(jnp.concatenate([-sin, sin], axis=-1), (1, reps))
        return cos_f, sin_f


def _row_spec(tm, n):
    return pl.BlockSpec((tm, n), lambda i: (i, 0))


def _vec_spec(n):
    return pl.BlockSpec((1, n), lambda i: (0, 0))


class _Seg:
    def __init__(self, emit, w, aux, aux_specs, out_shape, out_specs):
        self.emit, self.w = emit, w
        self.aux, self.aux_specs = list(aux), list(aux_specs)
        self.out_shape, self.out_specs = list(out_shape), list(out_specs)


def _emit_qk(h, w_ref, aux, outs, *, n_heads, scale, use_rope, head_major, emit_f32):
    cos_ref, sin_ref, hg_ref = aux
    hg = hg_ref[...]
    of_ref = outs[0] if emit_f32 else None
    ob_ref = outs[-1]
    for c in range(n_heads // 2):
        y = jnp.dot(h, w_ref[:, c * MXU_N:(c + 1) * MXU_N], preferred_element_type=F32)
        for j in range(2):
            hd = 2 * c + j
            yh = y[:, j * LANES:(j + 1) * LANES]
            ms = jnp.mean(yh * yh, axis=-1, keepdims=True)
            yr = yh * lax.rsqrt(ms + EPS) * hg
            if use_rope:
                yr = _rope128(yr, cos_ref[...], sin_ref[...])
            if emit_f32:
                of_ref[:, hd * LANES:(hd + 1) * LANES] = yr
            yb = (yr * scale).astype(BF16)
            if head_major:
                ob_ref[hd] = yb
            else:
                ob_ref[:, hd * LANES:(hd + 1) * LANES] = yb


def _emit_plain(h, w_ref, aux, outs, *, n_cols):
    of_ref, ob_ref = outs
    for c in range(n_cols // MXU_N):
        sl = slice(c * MXU_N, (c + 1) * MXU_N)
        y = jnp.dot(h, w_ref[:, sl], preferred_element_type=F32)
        of_ref[:, sl] = y
        ob_ref[:, sl] = y.astype(BF16)


def _emit_gate(h, w_ref, aux, outs, *, n_cols):
    for c in range(n_cols // MXU_N):
        sl = slice(c * MXU_N, (c + 1) * MXU_N)
        y = jnp.dot(h, w_ref[:, sl], preferred_element_type=F32)
        outs[0][:, sl] = (y * (1.0 / (1.0 + jnp.exp(-y)))).astype(BF16)


def _emit_rope64(h, w_ref, aux, outs, *, n_cols):
    cos, sin = aux[0][...], aux[1][...]
    for c in range(n_cols // MXU_N):
        y = jnp.dot(h, w_ref[:, c * MXU_N:(c + 1) * MXU_N], preferred_element_type=F32)
        for j in range(2):
            col = c * MXU_N + j * LANES
            outs[0][:, col:col + LANES] = _rope64(y[:, j * LANES:(j + 1) * LANES], cos, sin).astype(BF16)


def _emit_idx_small(h, w_ref, aux, outs):
    cos_ref, sin_ref, hg_ref = aux
    y = jnp.dot(h, w_ref[...], preferred_element_type=F32)
    is_key = lax.broadcasted_iota(I32, y.shape, 1) < IDX_DIM
    ms = jnp.sum(jnp.where(is_key, y * y, 0.0), axis=-1, keepdims=True) * (1.0 / IDX_DIM)
    kn = _rope64(y * lax.rsqrt(ms + EPS) * hg_ref[...], cos_ref[...], sin_ref[...])
    outs[0][...] = jnp.where(is_key, kn, y)


def _emit_logf(h, w_ref, aux, outs):
    z = jnp.dot(h, w_ref[...], preferred_element_type=F32) + aux[0][...]
    outs[0][...] = jnp.minimum(z, 0.0) - jnp.log1p(jnp.exp(-jnp.abs(z)))


def _proj_kernel(x_ref, g_ref, *refs, segs):
    h = _norm_rows(x_ref, g_ref)
    n_in = sum(1 + len(s.aux) for s in segs)
    ins, outs = refs[:n_in], refs[n_in:]
    i = o = 0
    for s in segs:
        s.emit(h, ins[i], ins[i + 1:i + 1 + len(s.aux)], outs[o:o + len(s.out_shape)])
        i += 1 + len(s.aux)
        o += len(s.out_shape)


def _project(rows, x, g, segs, name):
    m, d = x.shape
    tm = rows.tm
    in_specs = [pl.BlockSpec((tm, d), lambda i: (i, 0)), pl.BlockSpec((1, d), lambda i: (0, 0))]
    args = [x, g.reshape(1, d)]
    for s in segs:
        block = s.w.block
        in_specs.append(pl.BlockSpec((d, s.w.n), lambda i, block=block: (0, block), pipeline_mode=pl.Buffered(1)))
        in_specs += s.aux_specs
        args += [s.w.array] + s.aux
    out = pl.pallas_call(
        functools.partial(_proj_kernel, segs=segs), grid=(m // tm,), in_specs=in_specs, name=name,
        out_specs=[sp for s in segs for sp in s.out_specs],
        out_shape=[sh for s in segs for sh in s.out_shape],
        compiler_params=_params(("parallel",)))(*args)
    res, o = [], 0
    for s in segs:
        res.append(out[o:o + len(s.out_shape)])
        o += len(s.out_shape)
    return res


def _seg_qk(rows, w, head_gain, *, scale, use_rope, head_major, emit_f32):
    n = w.n
    n_heads = n // HEAD_DIM
    cos, sin = rows.tables(HEAD_DIM)
    out_shape, out_specs = [], []
    if emit_f32:
        out_shape.append(jax.ShapeDtypeStruct((rows.m, n), F32))
        out_specs.append(_row_spec(rows.tm, n))
    if head_major:
        out_shape.append(jax.ShapeDtypeStruct((n_heads, rows.m, HEAD_DIM), BF16))
        out_specs.append(pl.BlockSpec((n_heads, rows.tm, HEAD_DIM), lambda i: (0, i, 0)))
    else:
        out_shape.append(jax.ShapeDtypeStruct((rows.m, n), BF16))
        out_specs.append(_row_spec(rows.tm, n))
    emit = functools.partial(_emit_qk, n_heads=n_heads, scale=scale, use_rope=use_rope,
                             head_major=head_major, emit_f32=emit_f32)
    return _Seg(emit, w, (cos, sin, head_gain.reshape(1, HEAD_DIM)),
                (rows.tab_spec(), rows.tab_spec(), _vec_spec(HEAD_DIM)), out_shape, out_specs)


def _seg_plain(rows, w):
    n = w.n
    return _Seg(functools.partial(_emit_plain, n_cols=n), w, (), (),
                [jax.ShapeDtypeStruct((rows.m, n), F32), jax.ShapeDtypeStruct((rows.m, n), BF16)],
                [_row_spec(rows.tm, n), _row_spec(rows.tm, n)])


def _seg_gate(rows, w):
    n = w.n
    return _Seg(functools.partial(_emit_gate, n_cols=n), w, (), (),
                [jax.ShapeDtypeStruct((rows.m, n), BF16)], [_row_spec(rows.tm, n)])


def _seg_rope64(rows, w):
    n = w.n
    cos, sin = rows.tables(IDX_DIM)
    return _Seg(functools.partial(_emit_rope64, n_cols=n), w, (cos, sin), (rows.tab_spec(), rows.tab_spec()),
                [jax.ShapeDtypeStruct((rows.m, n), BF16)], [_row_spec(rows.tm, n)])


def _seg_idx_small(rows, w, key_gain):
    cos, sin = rows.tables(IDX_DIM)
    gain = jnp.concatenate([key_gain, jnp.zeros((LANES - IDX_DIM,), F32)]).reshape(1, LANES)
    return _Seg(_emit_idx_small, w, (cos, sin, gain), (rows.tab_spec(), rows.tab_spec(), _vec_spec(LANES)),
                [jax.ShapeDtypeStruct((rows.m, LANES), F32)], [_row_spec(rows.tm, LANES)])


def _seg_logf(rows, w, fb):
    fbp = jnp.concatenate([fb, jnp.zeros((LANES - C_HEADS,), F32)]).reshape(1, LANES)
    return _Seg(_emit_logf, w, (fbp,), (_vec_spec(LANES),),
                [jax.ShapeDtypeStruct((rows.m, LANES), F32)], [_row_spec(rows.tm, LANES)])


def _out_kernel(x_ref, o_ref, sg_ref, w_ref, y_ref, *, n_cols):
    a = o_ref[...] * sg_ref[...]
    for c in range(n_cols // MXU_N):
        sl = slice(c * MXU_N, (c + 1) * MXU_N)
        y_ref[:, sl] = x_ref[:, sl] + jnp.dot(a, w_ref[:, sl], preferred_element_type=F32)


def _out_proj(rows, x, o, sg, w):
    m, d = x.shape
    tm = rows.tm
    kern = functools.partial(_out_kernel, n_cols=d)
    return pl.pallas_call(
        kern, grid=(m // tm,), name="out_proj",
        in_specs=[_row_spec(tm, d), _row_spec(tm, d), _row_spec(tm, d),
                  pl.BlockSpec((d, d), lambda i: (0, 0))],
        out_specs=_row_spec(tm, d), out_shape=jax.ShapeDtypeStruct((m, d), F32),
        compiler_params=_params(("parallel",)))(x, o, sg, w)


class _Attn:
    def __init__(self, batch, t_q, s_pad, s_valid, q_off, tq, tk):
        self.batch, self.t_q, self.s_pad, self.s_valid = batch, t_q, s_pad, s_valid
        self.q_off, self.tq, self.tk = q_off, tq, tk
        self.nq = t_q // tq


def _chunk_limits(i, cfg):
    q0 = cfg.q_off + i * cfg.tq
    qpos = q0 + lax.broadcasted_iota(I32, (cfg.tq, 1), 0)
    qend = jnp.minimum(((qpos >> CHUNK_SHIFT) + 1) << CHUNK_SHIFT, cfg.s_valid)
    kmin = jnp.minimum(((q0 >> CHUNK_SHIFT) + 1) << CHUNK_SHIFT, cfg.s_valid)
    kend = jnp.minimum((((q0 + cfg.tq - 1) >> CHUNK_SHIFT) + 1) << CHUNK_SHIFT, cfg.s_valid)
    return qend, kmin // cfg.tk, (kend + cfg.tk - 1) // cfg.tk


def _causal_limits(i, cfg):
    q0 = cfg.q_off + i * cfg.tq
    qpos = q0 + lax.broadcasted_iota(I32, (cfg.tq, 1), 0)
    return qpos + 1, (q0 + 1) // cfg.tk, (q0 + cfg.tq + cfg.tk - 1) // cfg.tk


def _qk(q, k):
    return lax.dot_general(q, k, (((1,), (1,)), ((), ())), preferred_element_type=F32)


def _visibility_bias(ks, qend, shape):
    kpos = ks + lax.broadcasted_iota(I32, shape, 1)
    return jnp.where(kpos < qend, 0.0, MASKED)


def _flash_step(qs, ks, vs, bias_fn, state, *, rows, strip):
    s_ref, p_ref, m_ref, l_ref, alpha_ref, acc_ref = state
    n = len(qs)
    strips = [slice(r * strip, (r + 1) * strip) for r in range(rows // strip)]
    chunks = [slice(a, a + LANES) for a in range(0, s_ref.shape[-1], LANES)]
    for c in range(n):
        s_ref[c] = _qk(qs[c], ks[c])
    for c in range(n):
        for r, rs in enumerate(strips):
            mx = None
            for ch in chunks:
                s = s_ref[c, rs, ch]
                b = bias_fn(c, r, ch)
                if b is not None:
                    s = s + b
                    s_ref[c, rs, ch] = s
                mx = s if mx is None else jnp.maximum(mx, s)
            m_old = m_ref[c, rs, :]
            m_new = jnp.maximum(m_old, jnp.broadcast_to(jnp.max(mx, axis=-1, keepdims=True), mx.shape))
            alpha_ref[c, rs, :] = jnp.exp2(m_old - m_new)
            m_ref[c, rs, :] = m_new
        for rs in strips:
            m = m_ref[c, rs, :]
            psum = None
            for ch in chunks:
                p = jnp.exp2(s_ref[c, rs, ch] - m)
                if l_ref is not None:
                    psum = p if psum is None else psum + p
                p_ref[c, rs, ch] = p.astype(BF16)
            if l_ref is not None:
                l_ref[c, rs, :] = alpha_ref[c, rs, :] * l_ref[c, rs, :] + psum
    for c in range(n):
        alpha = jnp.concatenate([alpha_ref[c]] * (acc_ref.shape[-1] // LANES), axis=1)
        acc_ref[c] = alpha * acc_ref[c] + jnp.dot(p_ref[c], vs[c], preferred_element_type=F32)


def _init_state(state):
    _, _, m_ref, l_ref, _, acc_ref = state
    m_ref[...] = jnp.full(m_ref.shape, MASKED, F32)
    acc_ref[...] = jnp.zeros(acc_ref.shape, F32)
    if l_ref is not None:
        l_ref[...] = jnp.zeros(l_ref.shape, F32)


def _flash_scratch(chains, rows, tk, dv, with_l):
    stat = pltpu.VMEM((chains, rows, LANES), F32)
    return ([pltpu.VMEM((chains, rows, tk), F32), pltpu.VMEM((chains, rows, tk), BF16), stat]
            + ([stat] if with_l else []) + [stat, pltpu.VMEM((chains, rows, dv), F32)])


def _with_ones(v):
    return jnp.concatenate([v, jnp.ones(v.shape, v.dtype)], axis=1)


def _dsa_kernel(q_ref, qi_ref, sm_ref, k_ref, v_ref, kit_ref, o_ref,
                key_ref, bias_ref, stat_ref, look_ref, s_ref, p_ref, m_ref, alpha_ref, acc_ref, *, cfg, topk):
    tq, tk = cfg.tq, cfg.tk
    i = pl.program_id(1)
    qend, _, n_kv = _chunk_limits(i, cfg)
    sub = min(tq, SCORE_ROWS)
    stat_ref[0] = jnp.full((tq, LANES), INT_MIN, I32)
    stat_ref[1] = jnp.zeros((tq, LANES), I32)

    def score_block(j, carry):
        ks = pl.multiple_of(j * tk, tk)
        for r0 in range(0, tq, sub):
            rs = slice(r0, r0 + sub)
            w = sm_ref[rs, IDX_DIM:IDX_DIM + IDX_HEADS] * (IDX_DIM ** -0.5 * IDX_HEADS ** -0.5)
            acc = jnp.zeros((sub, tk), F32)
            for pair in range(IDX_HEADS // 2):
                qp = qi_ref[rs, pair * LANES:(pair + 1) * LANES]
                for e in range(2):
                    hd = 2 * pair + e
                    sc = jnp.dot(qp, kit_ref[e, :, pl.ds(ks, tk)], preferred_element_type=F32)
                    acc = acc + w[:, hd:hd + 1] * jnp.maximum(sc, 0.0)
            bits = lax.bitcast_convert_type(acc, I32)
            key = bits ^ ((bits >> 31) & 0x7FFFFFFF)
            key = jnp.where(acc == 0.0, 0, key)
            kpos = ks + lax.broadcasted_iota(I32, (sub, tk), 1)
            key = jnp.where(kpos < qend[rs], key, NEG_INF_KEY)
            key_ref[rs, pl.ds(ks, tk)] = key
            kmax, nfin = stat_ref[0, rs, :], stat_ref[1, rs, :]
            for c in range(tk // LANES):
                kc = key[:, c * LANES:(c + 1) * LANES]
                kmax = jnp.maximum(kmax, kc)
                nfin = nfin + jnp.where(kc > NEG_INF_KEY, 1, 0)
            stat_ref[0, rs, :], stat_ref[1, rs, :] = kmax, nfin
        return carry

    lax.fori_loop(0, n_kv, score_block, 0)

    def key_chunk(j, c):
        return key_ref[:, pl.ds(pl.multiple_of(j * tk + c * LANES, LANES), LANES)]

    def lane_total(x):
        return jnp.broadcast_to(jnp.sum(x, axis=-1, keepdims=True), x.shape)

    def count_ge(cand):
        def body(j, part):
            ks = pl.multiple_of(j * tk, tk)
            ge = jnp.where(key_ref[:, pl.ds(ks, tk)] >= cand, 1, 0)
            for c in range(tk // LANES):
                part = part + ge[:, c * LANES:(c + 1) * LANES]
            return part
        part = lax.fori_loop(0, n_kv, body, jnp.zeros((tq, LANES), I32))
        return jnp.sum(part, axis=-1, keepdims=True)

    n_fin = lane_total(stat_ref[1])
    small = n_fin <= topk
    look_ref[0] = jnp.full((tq, LANES), NEG_INF_KEY + 1, I32)
    look_ref[1] = jnp.broadcast_to(jnp.max(stat_ref[0], axis=-1, keepdims=True), (tq, LANES)) + 1
    look_ref[2] = n_fin

    def pending():
        lo, hi, cnt = look_ref[0], look_ref[1], look_ref[2]
        open_row = (cnt != topk) & (hi != lo + 1) & jnp.logical_not(small)
        return jnp.max(jnp.where(open_row, 1, 0))

    groups = [slice(r, r + min(tq, SEARCH_ROWS)) for r in range(0, tq, min(tq, SEARCH_ROWS))]

    def halve(n_blocks):
        state = [[look_ref[v, gs, :] for v in range(3)] for gs in groups]
        part = jnp.zeros((groups[0].stop, LANES), I32)
        for _ in range(BISECT_UNROLL):
            for g, gs in enumerate(groups):
                lo, hi, cnt = state[g]
                mid = (lo >> 1) + (hi >> 1) + (lo & hi & 1)
                part = part >> 31
                for a in range(0, n_blocks * tk, LANES):
                    part = part + jnp.where(key_ref[gs, a:a + LANES] >= mid, 1, 0)
                c = lane_total(part)
                up = c >= topk
                state[g] = [jnp.where(up, mid, lo), jnp.where(up, hi, mid), jnp.where(up, c, cnt)]
        for g, gs in enumerate(groups):
            for v in range(3):
                look_ref[v, gs, :] = state[g][v]

    def search_round(_):
        for n_blocks in range(1, cfg.s_pad // tk + 1):
            pl.when(n_kv == n_blocks)(functools.partial(halve, n_blocks))
        return pending()

    lax.while_loop(lambda flag: flag > 0, search_round, pending())
    thr, n_ge = look_ref[0], look_ref[2]
    tie_rows = (n_ge > topk) & jnp.logical_not(small)

    @pl.when(jnp.max(jnp.where(tie_rows, 1, 0)) > 0)
    def _drop_late_ties():
        need_f = jnp.broadcast_to((topk - count_ge(thr[:, :1] + 1)).astype(F32), (tq, LANES))
        upper = jnp.where(lax.broadcasted_iota(I32, (LANES, LANES), 0) <= lax.broadcasted_iota(I32, (LANES, LANES), 1),
                          1.0, 0.0).astype(BF16)

        def body(j, seen):
            for c in range(tk // LANES):
                kb = key_chunk(j, c)
                eq = kb == thr
                eq_f = jnp.where(eq, 1.0, 0.0)
                rank = seen + jnp.dot(eq_f.astype(BF16), upper, preferred_element_type=F32)
                drop = eq & (rank > need_f) & tie_rows
                key_ref[:, pl.ds(pl.multiple_of(j * tk + c * LANES, LANES), LANES)] = jnp.where(drop, thr - 1, kb)
                seen = seen + lane_total(eq_f)
            return seen

        lax.fori_loop(0, n_kv, body, jnp.zeros((tq, LANES), F32))

    thr_keep = jnp.where(small, NEG_INF_KEY + 1, jnp.maximum(thr, NEG_INF_KEY + 1))
    state = (s_ref, p_ref, m_ref, None, alpha_ref, acc_ref)
    _init_state(state)
    strip = min(tq, STRIP_ROWS)
    strips_per_head = tq // strip

    def attn_block(j, carry):
        ks = pl.multiple_of(j * tk, tk)
        for c in range(tk // LANES):
            bias_ref[:, c * LANES:(c + 1) * LANES] = jnp.where(key_chunk(j, c) >= thr_keep, 0.0, MASKED)
        cols =[slice(g * HEAD_DIM, (g + 1) * HEAD_DIM) for g in range(A_KV_HEADS)]
        _flash_step([q_ref[g * A_REP:(g + 1) * A_REP].reshape(A_REP * tq, HEAD_DIM) for g in range(A_KV_HEADS)],
                    [k_ref[pl.ds(ks, tk), c] for c in cols],
                    [_with_ones(v_ref[pl.ds(ks, tk), c]) for c in cols],
                    lambda c, r, ch: bias_ref[(r % strips_per_head) * strip:(r % strips_per_head + 1) * strip, ch],
                    state, rows=A_REP * tq, strip=strip)
        return carry

    lax.fori_loop(0, n_kv, attn_block, 0)
    for g in range(A_KV_HEADS):
        o = acc_ref[g, :, :HEAD_DIM] / acc_ref[g, :, HEAD_DIM:]
        for r in range(A_REP):
            hd = g * A_REP + r
            o_ref[:, hd * HEAD_DIM:(hd + 1) * HEAD_DIM] = o[r * tq:(r + 1) * tq].astype(BF16)


def _dsa_attention(cfg, q_hm, q_row0, qi, sm, k, v, kit, topk):
    tq, nq, s = cfg.tq, cfg.nq, cfg.s_pad
    qb0 = q_row0 // tq
    kern = functools.partial(_dsa_kernel, cfg=cfg, topk=topk)
    return pl.pallas_call(
        kern, grid=(cfg.batch, nq), name="dsa_attention",
        in_specs=[pl.BlockSpec((A_HEADS, tq, HEAD_DIM), lambda b, i: (0, qb0 + b * nq + i, 0)),
                  pl.BlockSpec((tq, IDX_HEADS * IDX_DIM), lambda b, i: (qb0 + b * nq + i, 0)),
                  pl.BlockSpec((tq, LANES), lambda b, i: (qb0 + b * nq + i, 0)),
                  pl.BlockSpec((s, A_KV_HEADS * HEAD_DIM), lambda b, i: (b, 0), pipeline_mode=pl.Buffered(1)),
                  pl.BlockSpec((s, A_KV_HEADS * HEAD_DIM), lambda b, i: (b, 0), pipeline_mode=pl.Buffered(1)),
                  pl.BlockSpec((None, 2, LANES, s), lambda b, i: (b, 0, 0, 0), pipeline_mode=pl.Buffered(1))],
        out_specs=pl.BlockSpec((tq, MIX_WIDTH), lambda b, i: (b * nq + i, 0)),
        out_shape=jax.ShapeDtypeStruct((cfg.batch * cfg.t_q, MIX_WIDTH), BF16),
        scratch_shapes=[pltpu.VMEM((tq, s), I32), pltpu.VMEM((tq, cfg.tk), F32), pltpu.VMEM((2, tq, LANES), I32),
                        pltpu.VMEM((3, tq, LANES), I32)]
        + _flash_scratch(A_KV_HEADS, A_REP * tq, cfg.tk, 2 * HEAD_DIM, with_l=False),
        compiler_params=_params(("parallel", "arbitrary")))(q_hm, qi, sm, k, v, kit)


B_STEP_HEADS = 2


def _run_blocks(n_full, n_kv, step):
    def plain(j, carry):
        step(j, False)
        return carry

    def masked(j, carry):
        step(j, True)
        return carry

    lax.fori_loop(0, n_full, plain, 0)
    lax.fori_loop(n_full, n_kv, masked, 0)


def _diff_kernel(q_ref, k_ref, v_ref, lq1_ref, lk1_ref, lq2_ref, lk2_ref, sub_ref, o_ref,
                 bias_ref, s_ref, p_ref, m_ref, l_ref, alpha_ref, acc_ref, *, cfg, lam_init):
    tq, tk = cfg.tq, cfg.tk
    qend, n_full, n_kv = _chunk_limits(pl.program_id(2), cfg)
    state = (s_ref, p_ref, m_ref, l_ref, alpha_ref, acc_ref)
    _init_state(state)
    strip = min(tq, STRIP_ROWS)
    n_chains = 2 * B_STEP_HEADS

    def step(j, masked):
        ks = pl.multiple_of(j * tk, tk)
        if masked:
            bias_ref[...] = _visibility_bias(ks, qend, (tq, tk))
        qk_cols = [slice(c * HEAD_DIM, (c + 1) * HEAD_DIM) for c in range(n_chains)]
        v_cols = [slice((c // 2) * B_V_DIM, (c // 2 + 1) * B_V_DIM) for c in range(n_chains)]
        _flash_step([q_ref[:, c] for c in qk_cols],
                    [k_ref[pl.ds(ks, tk), c] for c in qk_cols],
                    [v_ref[pl.ds(ks, tk), c] for c in v_cols],
                    (lambda c, r, ch: bias_ref[r * strip:(r + 1) * strip, ch]) if masked
                    else (lambda c, r, ch: None),
                    state, rows=tq, strip=strip)

    _run_blocks(n_full, n_kv, step)
    lam = (jnp.exp(jnp.sum(lq1_ref[...] * lk1_ref[...], axis=-1, keepdims=True))
           - jnp.exp(jnp.sum(lq2_ref[...] * lk2_ref[...], axis=-1, keepdims=True)) + lam_init)
    def normalised(c):
        return acc_ref[c] / jnp.sum(l_ref[c], axis=-1, keepdims=True)

    for h in range(B_STEP_HEADS):
        o = normalised(2 * h) - lam * normalised(2 * h + 1)
        ms = jnp.mean(o * o, axis=-1, keepdims=True)
        o_ref[:, h * B_V_DIM:(h + 1) * B_V_DIM] = (
            o * lax.rsqrt(ms + EPS) * sub_ref[...] * (1.0 - lam_init)).astype(BF16)


def _diff_attention(cfg, q, q_row0, k, v, lam_vecs, subln, lam_init):
    tq, nq, s = cfg.tq, cfg.nq, cfg.s_pad
    qb0 = q_row0 // tq
    width = B_STEP_HEADS * B_V_DIM
    kern = functools.partial(_diff_kernel, cfg=cfg, lam_init=lam_init)
    vec = pl.BlockSpec((1, HEAD_DIM), lambda b, h, i: (0, 0))
    return pl.pallas_call(
        kern, grid=(cfg.batch, B_HEADS // B_STEP_HEADS, nq), name="diff_attention",
        in_specs=[pl.BlockSpec((tq, width), lambda b, h, i: (qb0 + b * nq + i, h)),
                  pl.BlockSpec((s, width), lambda b, h, i: (b, h)),
                  pl.BlockSpec((s, width), lambda b, h, i: (b, h)),
                  vec, vec, vec, vec,
                  pl.BlockSpec((1, B_V_DIM), lambda b, h, i: (0, 0))],
        out_specs=pl.BlockSpec((tq, width), lambda b, h, i: (b * nq + i, h)),
        out_shape=jax.ShapeDtypeStruct((cfg.batch * cfg.t_q, MIX_WIDTH), BF16),
        scratch_shapes=[pltpu.VMEM((tq, cfg.tk), F32)]
        + _flash_scratch(2 * B_STEP_HEADS, tq, cfg.tk, B_V_DIM, with_l=True),
        compiler_params=_params(("parallel", "parallel", "arbitrary")))(
            q, k, v, *[a.reshape(1, HEAD_DIM) for a in lam_vecs], subln.reshape(1, B_V_DIM))


C_STEP_HEADS = 4


def _fox_kernel(q_ref, k_ref, v_ref, cq_ref, ck_ref, o_ref,
                bias_ref, cqh_ref, s_ref, p_ref, m_ref, alpha_ref, acc_ref, *, cfg):
    tq, tk = cfg.tq, cfg.tk
    hg = pl.program_id(1)
    qend, n_full, n_kv = _causal_limits(pl.program_id(2), cfg)
    state = (s_ref, p_ref, m_ref, None, alpha_ref, acc_ref)
    _init_state(state)
    strip = min(tq, STRIP_ROWS)
    lane = lax.broadcasted_iota(I32, (tq, LANES), 1)
    for c in range(C_STEP_HEADS):
        head_lane = lane == hg * C_STEP_HEADS + c
        cqh_ref[c] = jnp.broadcast_to(
            jnp.sum(jnp.where(head_lane, cq_ref[...], 0.0), axis=-1, keepdims=True), (tq, LANES))

    def step(j, masked):
        ks = pl.multiple_of(j * tk, tk)
        if masked:
            bias_ref[...] = _visibility_bias(ks, qend, (tq, tk))

        def bias(c, r, ch):
            rs = slice(r * strip, (r + 1) * strip)
            b = cqh_ref[c, rs, :] - ck_ref[c:c + 1, pl.ds(pl.multiple_of(ks + ch.start, LANES), LANES)]
            return b + bias_ref[rs, ch] if masked else b

        cols = [slice(c * HEAD_DIM, (c + 1) * HEAD_DIM) for c in range(C_STEP_HEADS)]
        _flash_step([q_ref[:, c] for c in cols],
                    [k_ref[pl.ds(ks, tk), c] for c in cols],
                    [_with_ones(v_ref[pl.ds(ks, tk), c]) for c in cols],
                    bias, state, rows=tq, strip=strip)

    _run_blocks(n_full, n_kv, step)
    for c in range(C_STEP_HEADS):
        o_ref[:, c * HEAD_DIM:(c + 1) * HEAD_DIM] = (
            acc_ref[c, :, :HEAD_DIM] / acc_ref[c, :, HEAD_DIM:]).astype(BF16)


def _fox_attention(cfg, q, q_row0, k, v, cq, ck_rows):
    tq, nq, s = cfg.tq, cfg.nq, cfg.s_pad
    qb0 = q_row0 // tq
    width = C_STEP_HEADS * HEAD_DIM
    groups = C_HEADS // C_STEP_HEADS
    kern = functools.partial(_fox_kernel, cfg=cfg)
    return pl.pallas_call(
        kern, grid=(cfg.batch, groups, nq), name="fox_attention",
        in_specs=[pl.BlockSpec((tq, width), lambda b, h, i: (qb0 + b * nq + i, h)),
                  pl.BlockSpec((s, width), lambda b, h, i: (b, h)),
                  pl.BlockSpec((s, width), lambda b, h, i: (b, h)),
                  pl.BlockSpec((tq, LANES), lambda b, h, i: (qb0 + b * nq + i, 0)),
                  pl.BlockSpec((None, C_STEP_HEADS, s), lambda b, h, i: (b * groups + h, 0, 0))],
        out_specs=pl.BlockSpec((tq, width), lambda b, h, i: (b * nq + i, h)),
        out_shape=jax.ShapeDtypeStruct((cfg.batch * cfg.t_q, MIX_WIDTH), BF16),
        scratch_shapes=[pltpu.VMEM((tq, cfg.tk), F32), pltpu.VMEM((C_STEP_HEADS, tq, LANES), F32)]
        + _flash_scratch(C_STEP_HEADS, tq, cfg.tk, 2 * HEAD_DIM, with_l=False),
        compiler_params=_params(("parallel", "parallel", "arbitrary")))(q, k, v, cq, ck_rows)


def _cumsum_kernel(x_ref, o_ref, carry_ref, *, tb):
    @pl.when(pl.program_id(1) == 0)
    def _():
        carry_ref[...] = jnp.zeros(carry_ref.shape, F32)

    x = x_ref[...]
    hi = x.astype(BF16)
    r1 = x - hi.astype(F32)
    mid = r1.astype(BF16)
    low = (r1 - mid.astype(F32)).astype(BF16)
    tri = jnp.where(lax.broadcasted_iota(I32, (tb, tb), 0) >= lax.broadcasted_iota(I32, (tb, tb), 1),
                    1.0, 0.0).astype(BF16)
    c = (jnp.dot(tri, hi, preferred_element_type=F32) + jnp.dot(tri, mid, preferred_element_type=F32)
         + jnp.dot(tri, low, preferred_element_type=F32)) + carry_ref[...]
    o_ref[...] = c * LOG2E
    carry_ref[...] = c[tb - 1:tb, :]


def _cumsum_rows(x, batch, t, tb):
    nt = t // tb
    return pl.pallas_call(
        functools.partial(_cumsum_kernel, tb=tb), grid=(batch, nt), name="cumsum_rows",
        in_specs=[pl.BlockSpec((tb, LANES), lambda b, j: (b * nt + j, 0))],
        out_specs=pl.BlockSpec((tb, LANES), lambda b, j: (b * nt + j, 0)),
        out_shape=jax.ShapeDtypeStruct((batch * t, LANES), F32),
        scratch_shapes=[pltpu.VMEM((1, LANES), F32)],
        compiler_params=_params(("parallel", "arbitrary")))(x)


def _split_w(w_in, sizes):
    offs = np.cumsum((0,) + tuple(sizes))
    return [w_in[:, int(offs[i]):int(offs[i + 1])] for i in range(len(sizes))]


def _pad_rows(a, rows):
    return jnp.pad(a, ((0, 0), (0, rows - a.shape[1])) + ((0, 0),) * (a.ndim - 2))


def _sample_keys(cache, new, s_pad):
    b = cache.shape[0]
    full = jnp.concatenate([cache.reshape(b, cache.shape[1], -1), new.reshape(b, new.shape[1], -1)], axis=1)
    return _pad_rows(full, s_pad).astype(BF16).reshape(b * s_pad, -1)


def _key_transposed_pairs(ki, batch, s):
    kt = jnp.swapaxes(ki.reshape(batch, s, IDX_DIM), 1, 2).astype(BF16)
    z = jnp.zeros_like(kt)
    return jnp.stack([jnp.concatenate([kt, z], axis=1), jnp.concatenate([z, kt], axis=1)], axis=1)


def _layer_a(pr, sr, cfg_p, cfg_s, xp, xs, ck, cv, cki, norm, w_in, w_out, qn, kn, ikn):
    w_all = w_in.astype(BF16)
    wq, wk, wv, wqi = _Cols(w_all, 2048, 0), _Cols(w_all, 512, 4), _Cols(w_all, 512, 5), _Cols(w_all, 1024, 3)
    wsm = _Cols(w_all, LANES, sum(A_SIZES[:4]) // LANES)
    wg = _Cols(_split_w(w_in, A_SIZES)[6].astype(BF16), MIX_WIDTH)
    wo = w_out.astype(BF16)

    def project(rows, x):
        (q,), (qi,), (sm,) = _project(rows, x, norm, [
            _seg_qk(rows, wq, qn, scale=HEAD_DIM ** -0.5 * LOG2E, use_rope=True, head_major=True, emit_f32=False),
            _seg_rope64(rows, wqi), _seg_idx_small(rows, wsm, ikn)], "proj_a_query")
        (k_f, k_b), (v_f, v_b), (sg,) = _project(rows, x, norm, [
            _seg_qk(rows, wk, kn, scale=1.0, use_rope=True, head_major=False, emit_f32=True),
            _seg_plain(rows, wv), _seg_gate(rows, wg)], "proj_a_kv_gate")
        return q, k_f, k_b, v_f, v_b, qi, sm, sg

    q, k_f, k_b, v_f, v_b, qi, sm, sg = project(pr, xp)
    ki_f = sm[:, :IDX_DIM]
    o = _dsa_attention(cfg_p, q, 0, qi, sm, k_b, v_b,
                       _key_transposed_pairs(ki_f, pr.batch, pr.t_q), min(TOPK_MAX, pr.t_q // 4))
    yp = _out_proj(pr, xp, o, sg, wo)

    q2, k2_f, _, v2_f, _, qi2, sm2, sg2 = project(sr, xs)
    ki2_f = sm2[:, :IDX_DIM]
    b, s = sr.batch, sr.t_q
    k_all = _sample_keys(ck, k2_f.reshape(b, s, -1), cfg_s.s_pad)
    v_all = _sample_keys(cv, v2_f.reshape(b, s, -1), cfg_s.s_pad)
    ki_all = _pad_rows(jnp.concatenate([cki, ki2_f.reshape(b, s, IDX_DIM)], axis=1), cfg_s.s_pad)
    o2 = _dsa_attention(cfg_s, q2, 0, qi2, sm2, k_all, v_all,
                        _key_transposed_pairs(ki_all.reshape(-1, IDX_DIM), b, cfg_s.s_pad),
                        min(TOPK_MAX, cfg_s.s_valid // 4))
    ys = _out_proj(sr, xs, o2, sg2, wo)
    kv = (A_KV_HEADS, HEAD_DIM)
    state = (k_f.reshape(pr.batch, pr.t_q, *kv), v_f.reshape(pr.batch, pr.t_q, *kv),
             ki_f.reshape(pr.batch, pr.t_q, IDX_DIM),
             k2_f.reshape(b, s, *kv), v2_f.reshape(b, s, *kv), ki2_f.reshape(b, s, IDX_DIM))
    return yp, ys, state


def _layer_b(layer, pr, sr, cfg_p, cfg_s, xp, xs, ck, cv, norm, w_in, w_out, qn, kn,
             lq1, lk1, lq2, lk2, subln):
    lam_init = 0.8 - 0.6 * float(np.exp(-0.3 * layer))
    w_all = w_in.astype(BF16)
    wq, wk, wv, wg = [_Cols(w_all, 2048, j) for j in range(4)]
    wo = w_out.astype(BF16)

    def project(rows, x):
        (q,), (sg,) = _project(rows, x, norm, [
            _seg_qk(rows, wq, qn, scale=HEAD_DIM ** -0.5 * LOG2E, use_rope=True, head_major=False, emit_f32=False),
            _seg_gate(rows, wg)], "proj_b_query_gate")
        (k_f, k_b), (v_f, v_b) = _project(rows, x, norm, [
            _seg_qk(rows, wk, kn, scale=1.0, use_rope=True, head_major=False, emit_f32=True),
            _seg_plain(rows, wv)], "proj_b_kv")
        return q, k_f, k_b, v_f, v_b, sg

    lam_vecs = (lq1, lk1, lq2, lk2)
    q, k_f, k_b, v_f, v_b, sg = project(pr, xp)
    o = _diff_attention(cfg_p, q, 0, k_b, v_b, lam_vecs, subln, lam_init)
    yp = _out_proj(pr, xp, o, sg, wo)

    q2, k2_f, _, v2_f, _, sg2 = project(sr, xs)
    b, s = sr.batch, sr.t_q
    k_all = _sample_keys(ck, k2_f.reshape(b, s, -1), cfg_s.s_pad)
    v_all = _sample_keys(cv, v2_f.reshape(b, s, -1), cfg_s.s_pad)
    o2 = _diff_attention(cfg_s, q2, 0, k_all, v_all, lam_vecs, subln, lam_init)
    ys = _out_proj(sr, xs, o2, sg2, wo)
    state = (k_f.reshape(pr.batch, pr.t_q, 2 * B_HEADS, HEAD_DIM), v_f.reshape(pr.batch, pr.t_q, B_HEADS, B_V_DIM),
             k2_f.reshape(b, s, 2 * B_HEADS, HEAD_DIM), v2_f.reshape(b, s, B_HEADS, B_V_DIM))
    return yp, ys, state


def _head_rows(c, batch, s):
    rows = jnp.swapaxes(c.reshape(batch, s, LANES)[:, :, :C_HEADS], 1, 2)
    return rows.reshape(batch * C_HEADS // C_STEP_HEADS, C_STEP_HEADS, s)


def _layer_c(pr, sr, cfg_p, cfg_s, xp, xs, ck, cv, clogf, norm, w_in, w_out, qn, kn, fb):
    w_all = w_in.astype(BF16)
    wq, wk, wv = [_Cols(w_all, 2048, j) for j in range(3)]
    wf = _Cols(w_all, LANES, sum(C_SIZES[:3]) // LANES)
    wg = _Cols(_split_w(w_in, C_SIZES)[4].astype(BF16), MIX_WIDTH)
    wo = w_out.astype(BF16)

    def project(rows, x):
        (q,), (logf,), (sg,) = _project(rows, x, norm, [
            _seg_qk(rows, wq, qn, scale=HEAD_DIM ** -0.5 * LOG2E, use_rope=False, head_major=False, emit_f32=False),
            _seg_logf(rows, wf, fb), _seg_gate(rows, wg)], "proj_c_query_gate")
        (k_f, k_b), (v_f, v_b) = _project(rows, x, norm, [
            _seg_qk(rows, wk, kn, scale=1.0, use_rope=False, head_major=False, emit_f32=True),
            _seg_plain(rows, wv)], "proj_c_kv")
        return q, k_f, k_b, v_f, v_b, logf, sg

    q, k_f, k_b, v_f, v_b, logf, sg = project(pr, xp)
    c = _cumsum_rows(logf, pr.batch, pr.t_q, 512)
    o = _fox_attention(cfg_p, q, 0, k_b, v_b, c, _head_rows(c, pr.batch, pr.t_q))
    yp = _out_proj(pr, xp, o, sg, wo)

    q2, k2_f, _, v2_f, _, logf2, sg2 = project(sr, xs)
    b, s = sr.batch, sr.t_q
    k_all = _sample_keys(ck, k2_f.reshape(b, s, -1), cfg_s.s_pad)
    v_all = _sample_keys(cv, v2_f.reshape(b, s, -1), cfg_s.s_pad)
    logf_all = jnp.concatenate([jnp.pad(clogf.astype(F32), ((0, 0), (0, 0), (0, LANES - C_HEADS))),
                                logf2.reshape(b, s, LANES)], axis=1)
    c2 = _cumsum_rows(_pad_rows(logf_all, cfg_s.s_pad).reshape(b * cfg_s.s_pad, LANES), b, cfg_s.s_pad, LANES)
    cq2 = c2.reshape(b, cfg_s.s_pad, LANES)[:, PAST_LEN:PAST_LEN + s].reshape(b * s, LANES)
    o2 = _fox_attention(cfg_s, q2, 0, k_all, v_all, cq2, _head_rows(c2, b, cfg_s.s_pad))
    ys = _out_proj(sr, xs, o2, sg2, wo)
    hd = (C_HEADS, HEAD_DIM)
    state = (k_f.reshape(pr.batch, pr.t_q, *hd), v_f.reshape(pr.batch, pr.t_q, *hd),
             logf[:, :C_HEADS].reshape(pr.batch, pr.t_q, C_HEADS),
             k2_f.reshape(b, s, *hd), v2_f.reshape(b, s, *hd), logf2[:, :C_HEADS].reshape(b, s, C_HEADS))
    return yp, ys, state


def kernel(x_prompt, x_sample, cache_l0_k, cache_l0_v, cache_l0_kidx, cache_l1_k, cache_l1_v, cache_l2_k, cache_l2_v, cache_l2_logf, cache_l3_k, cache_l3_v, cache_l3_kidx, l0_norm, l0_w_in, l0_w_out, l0_q_norm, l0_k_norm, l0_idx_k_norm, l1_norm, l1_w_in, l1_w_out, l1_q_norm, l1_k_norm, l1_lambda_q1, l1_lambda_k1, l1_lambda_q2, l1_lambda_k2, l1_subln, l2_norm, l2_w_in, l2_w_out, l2_q_norm, l2_k_norm, l2_forget_bias, l3_norm, l3_w_in, l3_w_out, l3_q_norm, l3_k_norm, l3_idx_k_norm):
    bp, tp, d = x_prompt.shape
    bs, ts, _ = x_sample.shape
    s_valid = PAST_LEN + ts
    s_pad = -(-s_valid // LANES) * LANES
    pr = _Rows(bp, tp, 0, 512)
    sr = _Rows(bs, ts, PAST_LEN, bs * ts)
    a_p = _Attn(bp, tp, tp, tp, 0, 256, 512)
    a_s = _Attn(bs, ts, s_pad, s_valid, PAST_LEN, ts, s_pad)
    bc_p = _Attn(bp, tp, tp, tp, 0, 512, 512)
    bc_s = a_s

    xp = x_prompt.reshape(bp * tp, d)
    xs = x_sample.reshape(bs * ts, d)
    xp, xs, st0 = _layer_a(pr, sr, a_p, a_s, xp, xs, cache_l0_k, cache_l0_v, cache_l0_kidx,
                           l0_norm, l0_w_in, l0_w_out, l0_q_norm, l0_k_norm, l0_idx_k_norm)
    xp, xs, st1 = _layer_b(1, pr, sr, bc_p, bc_s, xp, xs, cache_l1_k, cache_l1_v,
                           l1_norm, l1_w_in, l1_w_out, l1_q_norm, l1_k_norm,
                           l1_lambda_q1, l1_lambda_k1, l1_lambda_q2, l1_lambda_k2, l1_subln)
    xp, xs, st2 = _layer_c(pr, sr, bc_p, bc_s, xp, xs, cache_l2_k, cache_l2_v, cache_l2_logf,
                           l2_norm, l2_w_in, l2_w_out, l2_q_norm, l2_k_norm, l2_forget_bias)
    xp, xs, st3 = _layer_a(pr, sr, a_p, a_s, xp, xs, cache_l3_k, cache_l3_v, cache_l3_kidx,
                           l3_norm, l3_w_in, l3_w_out, l3_q_norm, l3_k_norm, l3_idx_k_norm)
    return (xp.reshape(bp, tp, d), xs.reshape(bs, ts, d)) + st0 + st1 + st2 + st3
```

```python
import functools

import numpy as np
import jax
import jax.numpy as jnp
from jax import lax
from jax.experimental import pallas as pl
from jax.experimental.pallas import tpu as pltpu

F32 = jnp.float32
BF16 = jnp.bfloat16
I32 = jnp.int32

D_MODEL = 2048
PAST_LEN = 1024
CHUNK_SHIFT = 6
ROPE_THETA = 10000.0
EPS = 1e-6
HEAD_DIM = 128
A_HEADS = 16
A_KV_HEADS = 4
A_REP = A_HEADS // A_KV_HEADS
IDX_HEADS = 16
IDX_DIM = 64
TOPK_MAX = 256
B_HEADS = 8
B_V_DIM = 256
C_HEADS = 16
MIX_WIDTH = D_MODEL

A_SIZES = (2048, 512, 512, 1024, 64, 16, 2048)
B_SIZES = (2048, 2048, 2048, 2048)
C_SIZES = (2048, 2048, 2048, 16, 2048)

LANES = 128
MXU_N = 256
VMEM_LIMIT = 56 * 1024 * 1024
MASKED = -1e30
LOG2E = 1.4426950408889634
STRIP_ROWS = 32
BISECT_UNROLL = 3
SEARCH_ROWS = 32
INT_MIN = -2 ** 31
SCORE_ROWS = 128
NEG_INF_KEY = -2139095041


def _params(sem):
    return pltpu.CompilerParams(dimension_semantics=sem, vmem_limit_bytes=VMEM_LIMIT)


def _norm_rows(x_ref, g_ref):
    x = x_ref[...]
    ms = jnp.mean(x * x, axis=-1, keepdims=True)
    return (x * lax.rsqrt(ms + EPS) * g_ref[...]).astype(BF16)


def _rope128(y, cos, sin):
    return y * cos + pltpu.roll(y, 64, 1) * sin


def _rope64(y, cos, sin):
    lane = lax.broadcasted_iota(I32, y.shape, 1)
    rot = jnp.where((lane & 63) < 32, pltpu.roll(y, 96, 1), pltpu.roll(y, 32, 1))
    return y * cos + rot * sin


class _Cols:
    def __init__(self, array, n, block=0):
        assert array.shape[1] >= (block + 1) * n
        self.array, self.n, self.block = array, n, block


class _Rows:
    def __init__(self, batch, t_q, pos0, tm):
        self.batch, self.t_q, self.pos0, self.tm = batch, t_q, pos0, tm
        self.m = batch * t_q
        self.tab_blocks = max(t_q // tm, 1)

    def tab_spec(self):
        nb = self.tab_blocks
        return pl.BlockSpec((self.tm, LANES), lambda i: (i % nb, 0))

    def tables(self, d):
        period = max(self.t_q, self.tm)
        pos = self.pos0 + (jnp.arange(period) % self.t_q)
        half = d // 2
        inv = ROPE_THETA ** (-2.0 * jnp.arange(half, dtype=F32) / d)
        ang = pos.astype(F32)[:, None] * inv[None, :]
        cos, sin = jnp.cos(ang), jnp.sin(ang)
        reps = LANES // d
        cos_f = jnp.tile(jnp.concatenate([cos, cos], axis=-1), (1, reps))
        sin_f = jnp.tile(jnp.concatenate([-sin, sin], axis=-1), (1, reps))
        return cos_f, sin_f


def _row_spec(tm, n):
    return pl.BlockSpec((tm, n), lambda i: (i, 0))


def _vec_spec(n):
    return pl.BlockSpec((1, n), lambda i: (0, 0))


class _Seg:
    def __init__(self, emit, w, aux, aux_specs, out_shape, out_specs):
        self.emit, self.w = emit, w
        self.aux, self.aux_specs = list(aux), list(aux_specs)
        self.out_shape, self.out_specs = list(out_shape), list(out_specs)


def _emit_qk(h, w_ref, aux, outs, *, n_heads, scale, use_rope, head_major, emit_f32):
    cos_ref, sin_ref, hg_ref = aux
    hg = hg_ref[...]
    of_ref = outs[0] if emit_f32 else None
    ob_ref = outs[-1]
    for c in range(n_heads // 2):
        y = jnp.dot(h, w_ref[:, c * MXU_N:(c + 1) * MXU_N], preferred_element_type=F32)
        for j in range(2):
            hd = 2 * c + j
            yh = y[:, j * LANES:(j + 1) * LANES]
            ms = jnp.mean(yh * yh, axis=-1, keepdims=True)
            yr = yh * lax.rsqrt(ms + EPS) * hg
            if use_rope:
                yr = _rope128(yr, cos_ref[...], sin_ref[...])
            if emit_f32:
                of_ref[:, hd * LANES:(hd + 1) * LANES] = yr
            yb = (yr * scale).astype(BF16)
            if head_major:
                ob_ref[hd] = yb
            else:
                ob_ref[:, hd * LANES:(hd + 1) * LANES] = yb


def _emit_plain(h, w_ref, aux, outs, *, n_cols):
    of_ref, ob_ref = outs
    for c in range(n_cols // MXU_N):
        sl = slice(c * MXU_N, (c + 1) * MXU_N)
        y = jnp.dot(h, w_ref[:, sl], preferred_element_type=F32)
        of_ref[:, sl] = y
        ob_ref[:, sl] = y.astype(BF16)


def _emit_gate(h, w_ref, aux, outs, *, n_cols):
    for c in range(n_cols // MXU_N):
        sl = slice(c * MXU_N, (c + 1) * MXU_N)
        y = jnp.dot(h, w_ref[:, sl], preferred_element_type=F32)
        outs[0][:, sl] = (y * (1.0 / (1.0 + jnp.exp(-y)))).astype(BF16)


def _emit_rope64(h, w_ref, aux, outs, *, n_cols):
    cos, sin = aux[0][...], aux[1][...]
    for c in range(n_cols // MXU_N):
        y = jnp.dot(h, w_ref[:, c * MXU_N:(c + 1) * MXU_N], preferred_element_type=F32)
        for j in range(2):
            col = c * MXU_N + j * LANES
            outs[0][:, col:col + LANES] = _rope64(y[:, j * LANES:(j + 1) * LANES], cos, sin).astype(BF16)


def _emit_idx_small(h, w_ref, aux, outs):
    cos_ref, sin_ref, hg_ref = aux
    y = jnp.dot(h, w_ref[...], preferred_element_type=F32)
    is_key = lax.broadcasted_iota(I32, y.shape, 1) < IDX_DIM
    ms = jnp.sum(jnp.where(is_key, y * y, 0.0), axis=-1, keepdims=True) * (1.0 / IDX_DIM)
    kn = _rope64(y * lax.rsqrt(ms + EPS) * hg_ref[...], cos_ref[...], sin_ref[...])
    outs[0][...] = jnp.where(is_key, kn, y)


def _emit_logf(h, w_ref, aux, outs):
    z = jnp.dot(h, w_ref[...], preferred_element_type=F32) + aux[0][...]
    outs[0][...] = jnp.minimum(z, 0.0) - jnp.log1p(jnp.exp(-jnp.abs(z)))


def _proj_kernel(x_ref, g_ref, *refs, segs):
    h = _norm_rows(x_ref, g_ref)
    n_in = sum(1 + len(s.aux) for s in segs)
    ins, outs = refs[:n_in], refs[n_in:]
    i = o = 0
    for s in segs:
        s.emit(h, ins[i], ins[i + 1:i + 1 + len(s.aux)], outs[o:o + len(s.out_shape)])
        i += 1 + len(s.aux)
        o += len(s.out_shape)


def _project(rows, x, g, segs, name):
    m, d = x.shape
    tm = rows.tm
    in_specs = [pl.BlockSpec((tm, d), lambda i: (i, 0)), pl.BlockSpec((1, d), lambda i: (0, 0))]
    args = [x, g.reshape(1, d)]
    for s in segs:
        block = s.w.block
        in_specs.append(pl.BlockSpec((d, s.w.n), lambda i, block=block: (0, block), pipeline_mode=pl.Buffered(1)))
        in_specs += s.aux_specs
        args += [s.w.array] + s.aux
    out = pl.pallas_call(
        functools.partial(_proj_kernel, segs=segs), grid=(m // tm,), in_specs=in_specs, name=name,
        out_specs=[sp for s in segs for sp in s.out_specs],
        out_shape=[sh for s in segs for sh in s.out_shape],
        compiler_params=_params(("parallel",)))(*args)
    res, o = [], 0
    for s in segs:
        res.append(out[o:o + len(s.out_shape)])
        o += len(s.out_shape)
    return res


def _seg_qk(rows, w, head_gain, *, scale, use_rope, head_major, emit_f32):
    n = w.n
    n_heads = n // HEAD_DIM
    cos, sin = rows.tables(HEAD_DIM)
    out_shape, out_specs = [], []
    if emit_f32:
        out_shape.append(jax.ShapeDtypeStruct((rows.m, n), F32))
        out_specs.append(_row_spec(rows.tm, n))
    if head_major:
        out_shape.append(jax.ShapeDtypeStruct((n_heads, rows.m, HEAD_DIM), BF16))
        out_specs.append(pl.BlockSpec((n_heads, rows.tm, HEAD_DIM), lambda i: (0, i, 0)))
    else:
        out_shape.append(jax.ShapeDtypeStruct((rows.m, n), BF16))
        out_specs.append(_row_spec(rows.tm, n))
    emit = functools.partial(_emit_qk, n_heads=n_heads, scale=scale, use_rope=use_rope,
                             head_major=head_major, emit_f32=emit_f32)
    return _Seg(emit, w, (cos, sin, head_gain.reshape(1, HEAD_DIM)),
                (rows.tab_spec(), rows.tab_spec(), _vec_spec(HEAD_DIM)), out_shape, out_specs)


def _seg_plain(rows, w):
    n = w.n
    return _Seg(functools.partial(_emit_plain, n_cols=n), w, (), (),
                [jax.ShapeDtypeStruct((rows.m, n), F32), jax.ShapeDtypeStruct((rows.m, n), BF16)],
                [_row_spec(rows.tm, n), _row_spec(rows.tm, n)])


def _seg_gate(rows, w):
    n = w.n
    return _Seg(functools.partial(_emit_gate, n_cols=n), w, (), (),
                [jax.ShapeDtypeStruct((rows.m, n), BF16)], [_row_spec(rows.tm, n)])


def _seg_rope64(rows, w):
    n = w.n
    cos, sin = rows.tables(IDX_DIM)
    return _Seg(functools.partial(_emit_rope64, n_cols=n), w, (cos, sin), (rows.tab_spec(), rows.tab_spec()),
                [jax.ShapeDtypeStruct((rows.m, n), BF16)], [_row_spec(rows.tm, n)])


def _seg_idx_small(rows, w, key_gain):
    cos, sin = rows.tables(IDX_DIM)
    gain = jnp.concatenate([key_gain, jnp.zeros((LANES - IDX_DIM,), F32)]).reshape(1, LANES)
    return _Seg(_emit_idx_small, w, (cos, sin, gain), (rows.tab_spec(), rows.tab_spec(), _vec_spec(LANES)),
                [jax.ShapeDtypeStruct((rows.m, LANES), F32)], [_row_spec(rows.tm, LANES)])


def _seg_logf(rows, w, fb):
    fbp = jnp.concatenate([fb, jnp.zeros((LANES - C_HEADS,), F32)]).reshape(1, LANES)
    return _Seg(_emit_logf, w, (fbp,), (_vec_spec(LANES),),
                [jax.ShapeDtypeStruct((rows.m, LANES), F32)], [_row_spec(rows.tm, LANES)])


def _out_kernel(x_ref, o_ref, sg_ref, w_ref, y_ref, *, n_cols):
    a = o_ref[...] * sg_ref[...]
    for c in range(n_cols // MXU_N):
        sl = slice(c * MXU_N, (c + 1) * MXU_N)
        y_ref[:, sl] = x_ref[:, sl] + jnp.dot(a, w_ref[:, sl].astype(BF16), preferred_element_type=F32)


def _out_proj(rows, x, o, sg, w):
    m, d = x.shape
    tm = rows.tm
    kern = functools.partial(_out_kernel, n_cols=d)
    return pl.pallas_call(
        kern, grid=(m // tm,), name="out_proj",
        in_specs=[_row_spec(tm, d), _row_spec(tm, d), _row_spec(tm, d),
                  pl.BlockSpec((d, d), lambda i: (0, 0), pipeline_mode=pl.Buffered(1))],
        out_specs=_row_spec(tm, d), out_shape=jax.ShapeDtypeStruct((m, d), F32),
        compiler_params=_params(("parallel",)))(x, o, sg, w)


class _Attn:
    def __init__(self, batch, t_q, s_pad, s_valid, q_off, tq, tk):
        self.batch, self.t_q, self.s_pad, self.s_valid = batch, t_q, s_pad, s_valid
        self.q_off, self.tq, self.tk = q_off, tq, tk
        self.nq = t_q // tq


def _chunk_limits(i, cfg):
    q0 = cfg.q_off + i * cfg.tq
    qpos = q0 + lax.broadcasted_iota(I32, (cfg.tq, 1), 0)
    qend = jnp.minimum(((qpos >> CHUNK_SHIFT) + 1) << CHUNK_SHIFT, cfg.s_valid)
    kmin = jnp.minimum(((q0 >> CHUNK_SHIFT) + 1) << CHUNK_SHIFT, cfg.s_valid)
    kend = jnp.minimum((((q0 + cfg.tq - 1) >> CHUNK_SHIFT) + 1) << CHUNK_SHIFT, cfg.s_valid)
    return qend, kmin // cfg.tk, (kend + cfg.tk - 1) // cfg.tk


def _causal_limits(i, cfg):
    q0 = cfg.q_off + i * cfg.tq
    qpos = q0 + lax.broadcasted_iota(I32, (cfg.tq, 1), 0)
    return qpos + 1, (q0 + 1) // cfg.tk, (q0 + cfg.tq + cfg.tk - 1) // cfg.tk


def _qk(q, k):
    return lax.dot_general(q, k, (((1,), (1,)), ((), ())), preferred_element_type=F32)


def _visibility_bias(ks, qend, shape):
    kpos = ks + lax.broadcasted_iota(I32, shape, 1)
    return jnp.where(kpos < qend, 0.0, MASKED)


def _flash_step(qs, ks, vs, bias_fn, state, *, rows, strip):
    s_ref, p_ref, m_ref, l_ref, alpha_ref, acc_ref = state
    n = len(qs)
    strips = [slice(r * strip, (r + 1) * strip) for r in range(rows // strip)]
    chunks = [slice(a, a + LANES) for a in range(0, s_ref.shape[-1], LANES)]
    for c in range(n):
        s_ref[c] = _qk(qs[c], ks[c])
    for c in range(n):
        for r, rs in enumerate(strips):
            mx = None
            for ch in chunks:
                s = s_ref[c, rs, ch]
                b = bias_fn(c, r, ch)
                if b is not None:
                    s = s + b
                    s_ref[c, rs, ch] = s
                mx = s if mx is None else jnp.maximum(mx, s)
            m_old = m_ref[c, rs, :]
            m_new = jnp.maximum(m_old, jnp.broadcast_to(jnp.max(mx, axis=-1, keepdims=True), mx.shape))
            alpha_ref[c, rs, :] = jnp.exp2(m_old - m_new)
            m_ref[c, rs, :] = m_new
        for rs in strips:
            m = m_ref[c, rs, :]
            psum = None
            for ch in chunks:
                p = jnp.exp2(s_ref[c, rs, ch] - m)
                if l_ref is not None:
                    psum = p if psum is None else psum + p
                p_ref[c, rs, ch] = p.astype(BF16)
            if l_ref is not None:
                l_ref[c, rs, :] = alpha_ref[c, rs, :] * l_ref[c, rs, :] + psum
    for c in range(n):
        alpha = jnp.concatenate([alpha_ref[c]] * (acc_ref.shape[-1] // LANES), axis=1)
        acc_ref[c] = alpha * acc_ref[c] + jnp.dot(p_ref[c], vs[c], preferred_element_type=F32)


def _init_state(state):
    _, _, m_ref, l_ref, _, acc_ref = state
    m_ref[...] = jnp.full(m_ref.shape, MASKED, F32)
    acc_ref[...] = jnp.zeros(acc_ref.shape, F32)
    if l_ref is not None:
        l_ref[...] = jnp.zeros(l_ref.shape, F32)


def _flash_scratch(chains, rows, tk, dv, with_l):
    stat = pltpu.VMEM((chains, rows, LANES), F32)
    return ([pltpu.VMEM((chains, rows, tk), F32), pltpu.VMEM((chains, rows, tk), BF16), stat]
            + ([stat] if with_l else []) + [stat, pltpu.VMEM((chains, rows, dv), F32)])


def _with_ones(v):
    return jnp.concatenate([v, jnp.ones(v.shape, v.dtype)], axis=1)


def _dsa_kernel(q_ref, qi_ref, sm_ref, k_ref, v_ref, kit_ref, o_ref,
                key_ref, bias_ref, stat_ref, look_ref, s_ref, p_ref, m_ref, alpha_ref, acc_ref, *, cfg, topk):
    tq, tk = cfg.tq, cfg.tk
    i = pl.program_id(1)
    qend, _, n_kv = _chunk_limits(i, cfg)
    sub = min(tq, SCORE_ROWS)
    stat_ref[0] = jnp.full((tq, LANES), INT_MIN, I32)
    stat_ref[1] = jnp.zeros((tq, LANES), I32)

    def score_block(j, carry):
        ks = pl.multiple_of(j * tk, tk)
        for r0 in range(0, tq, sub):
            rs = slice(r0, r0 + sub)
            w = sm_ref[rs, IDX_DIM:IDX_DIM + IDX_HEADS] * (IDX_DIM ** -0.5 * IDX_HEADS ** -0.5)
            acc = jnp.zeros((sub, tk), F32)
            for pair in range(IDX_HEADS // 2):
                qp = qi_ref[rs, pair * LANES:(pair + 1) * LANES]
                for e in range(2):
                    hd = 2 * pair + e
                    sc = jnp.dot(qp, kit_ref[e, :, pl.ds(ks, tk)], preferred_element_type=F32)
                    acc = acc + w[:, hd:hd + 1] * jnp.maximum(sc, 0.0)
            bits = lax.bitcast_convert_type(acc, I32)
            key = bits ^ ((bits >> 31) & 0x7FFFFFFF)
            key = jnp.where(acc == 0.0, 0, key)
            kpos = ks + lax.broadcasted_iota(I32, (sub, tk), 1)
            key = jnp.where(kpos < qend[rs], key, NEG_INF_KEY)
            key_ref[rs, pl.ds(ks, tk)] = key
            kmax, nfin = stat_ref[0, rs, :], stat_ref[1, rs, :]
            for c in range(tk // LANES):
                kc = key[:, c * LANES:(c + 1) * LANES]
                kmax = jnp.maximum(kmax, kc)
                nfin = nfin + jnp.where(kc > NEG_INF_KEY, 1, 0)
            stat_ref[0, rs, :], stat_ref[1, rs, :] = kmax, nfin
        return carry

    lax.fori_loop(0, n_kv, score_block, 0)

    def key_chunk(j, c):
        return key_ref[:, pl.ds(pl.multiple_of(j * tk + c * LANES, LANES), LANES)]

    def lane_total(x):
        return jnp.broadcast_to(jnp.sum(x, axis=-1, keepdims=True), x.shape)

    def count_ge(cand):
        def body(j, part):
            ks = pl.multiple_of(j * tk, tk)
            ge = jnp.where(key_ref[:, pl.ds(ks, tk)] >= cand, 1, 0)
            for c in range(tk // LANES):
                part = part + ge[:, c * LANES:(c + 1) * LANES]
            return part
        part = lax.fori_loop(0, n_kv, body, jnp.zeros((tq, LANES), I32))
        return jnp.sum(part, axis=-1, keepdims=True)

    n_fin = lane_total(stat_ref[1])
    small = n_fin <= topk
    look_ref[0] = jnp.full((tq, LANES), NEG_INF_KEY + 1, I32)
    look_ref[1] = jnp.broadcast_to(jnp.max(stat_ref[0], axis=-1, keepdims=True), (tq, LANES)) + 1
    look_ref[2] = n_fin

    def pending():
        lo, hi, cnt = look_ref[0], look_ref[1], look_ref[2]
        open_row = (cnt != topk) & (hi != lo + 1) & jnp.logical_not(small)
        return jnp.max(jnp.where(open_row, 1, 0))

    groups = [slice(r, r + min(tq, SEARCH_ROWS)) for r in range(0, tq, min(tq, SEARCH_ROWS))]

    def halve(n_blocks):
        state = [[look_ref[v, gs, :] for v in range(3)] for gs in groups]
        part = jnp.zeros((groups[0].stop, LANES), I32)
        for _ in range(BISECT_UNROLL):
            for g, gs in enumerate(groups):
                lo, hi, cnt = state[g]
                mid = (lo >> 1) + (hi >> 1) + (lo & hi & 1)
                part = part >> 31
                for a in range(0, n_blocks * tk, LANES):
                    part = part + jnp.where(key_ref[gs, a:a + LANES] >= mid, 1, 0)
                c = lane_total(part)
                up = c >= topk
                state[g] = [jnp.where(up, mid, lo), jnp.where(up, hi, mid), jnp.where(up, c, cnt)]
        for g, gs in enumerate(groups):
            for v in range(3):
                look_ref[v, gs, :] = state[g][v]

    def search_round(_):
        for n_blocks in range(1, cfg.s_pad // tk + 1):
            pl.when(n_kv == n_blocks)(functools.partial(halve, n_blocks))
        return pending()

    lax.while_loop(lambda flag: flag > 0, search_round, pending())
    thr, n_ge = look_ref[0], look_ref[2]
    tie_rows = (n_ge > topk) & jnp.logical_not(small)

    @pl.when(jnp.max(jnp.where(tie_rows, 1, 0)) > 0)
    def _drop_late_ties():
        need_f = jnp.broadcast_to((topk - count_ge(thr[:, :1] + 1)).astype(F32), (tq, LANES))
        upper = jnp.where(lax.broadcasted_iota(I32, (LANES, LANES), 0) <= lax.broadcasted_iota(I32, (LANES, LANES), 1),
                          1.0, 0.0).astype(BF16)

        def body(j, seen):
            for c in range(tk // LANES):
                kb = key_chunk(j, c)
                eq = kb == thr
                eq_f = jnp.where(eq, 1.0, 0.0)
                rank = seen + jnp.dot(eq_f.astype(BF16), upper, preferred_element_type=F32)
                drop = eq & (rank > need_f) & tie_rows
                key_ref[:, pl.ds(pl.multiple_of(j * tk + c * LANES, LANES), LANES)] = jnp.where(drop, thr - 1, kb)
                seen = seen + lane_total(eq_f)
            return seen

        lax.fori_loop(0, n_kv, body, jnp.zeros((tq, LANES), F32))

    thr_keep = jnp.where(small, NEG_INF_KEY + 1, jnp.maximum(thr, NEG_INF_KEY + 1))
    state = (s_ref, p_ref, m_ref, None, alpha_ref, acc_ref)
    _init_state(state)
    strip = min(tq, STRIP_ROWS)
    strips_per_head = tq // strip

    def attn_block(j, carry):
        ks = pl.multiple_of(j * tk, tk)
        for c in range(tk // LANES):
            bias_ref[:, c * LANES:(c + 1) * LANES] = jnp.where(key_chunk(j, c) >= thr_keep, 0.0, MASKED)
        cols =[slice(g * HEAD_DIM, (g + 1) * HEAD_DIM) for g in range(A_KV_HEADS)]
        _flash_step([q_ref[g * A_REP:(g + 1) * A_REP].reshape(A_REP * tq, HEAD_DIM) for g in range(A_KV_HEADS)],
                    [k_ref[pl.ds(ks, tk), c] for c in cols],
                    [_with_ones(v_ref[pl.ds(ks, tk), c]) for c in cols],
                    lambda c, r, ch: bias_ref[(r % strips_per_head) * strip:(r % strips_per_head + 1) * strip, ch],
                    state, rows=A_REP * tq, strip=strip)
        return carry

    lax.fori_loop(0, n_kv, attn_block, 0)
    for g in range(A_KV_HEADS):
        o = acc_ref[g, :, :HEAD_DIM] / acc_ref[g, :, HEAD_DIM:]
        for r in range(A_REP):
            hd = g * A_REP + r
            o_ref[:, hd * HEAD_DIM:(hd + 1) * HEAD_DIM] = o[r * tq:(r + 1) * tq].astype(BF16)


def _dsa_attention(cfg, q_hm, q_row0, qi, sm, k, v, kit, topk):
    tq, nq, s = cfg.tq, cfg.nq, cfg.s_pad
    qb0 = q_row0 // tq
    kern = functools.partial(_dsa_kernel, cfg=cfg, topk=topk)
    return pl.pallas_call(
        kern, grid=(cfg.batch, nq), name="dsa_attention",
        in_specs=[pl.BlockSpec((A_HEADS, tq, HEAD_DIM), lambda b, i: (0, qb0 + b * nq + i, 0)),
                  pl.BlockSpec((tq, IDX_HEADS * IDX_DIM), lambda b, i: (qb0 + b * nq + i, 0)),
                  pl.BlockSpec((tq, LANES), lambda b, i: (qb0 + b * nq + i, 0)),
                  pl.BlockSpec((s, A_KV_HEADS * HEAD_DIM), lambda b, i: (b, 0), pipeline_mode=pl.Buffered(1)),
                  pl.BlockSpec((s, A_KV_HEADS * HEAD_DIM), lambda b, i: (b, 0), pipeline_mode=pl.Buffered(1)),
                  pl.BlockSpec((None, 2, LANES, s), lambda b, i: (b, 0, 0, 0), pipeline_mode=pl.Buffered(1))],
        out_specs=pl.BlockSpec((tq, MIX_WIDTH), lambda b, i: (b * nq + i, 0)),
        out_shape=jax.ShapeDtypeStruct((cfg.batch * cfg.t_q, MIX_WIDTH), BF16),
        scratch_shapes=[pltpu.VMEM((tq, s), I32), pltpu.VMEM((tq, cfg.tk), F32), pltpu.VMEM((2, tq, LANES), I32),
                        pltpu.VMEM((3, tq, LANES), I32)]
        + _flash_scratch(A_KV_HEADS, A_REP * tq, cfg.tk, 2 * HEAD_DIM, with_l=False),
        compiler_params=_params(("parallel", "arbitrary")))(q_hm, qi, sm, k, v, kit)


B_STEP_HEADS = 2


def _run_blocks(n_full, n_kv, step):
    def plain(j, carry):
        step(j, False)
        return carry

    def masked(j, carry):
        step(j, True)
        return carry

    lax.fori_loop(0, n_full, plain, 0)
    lax.fori_loop(n_full, n_kv, masked, 0)


def _diff_kernel(q_ref, k_ref, v_ref, lq1_ref, lk1_ref, lq2_ref, lk2_ref, sub_ref, o_ref,
                 bias_ref, s_ref, p_ref, m_ref, l_ref, alpha_ref, acc_ref, *, cfg, lam_init):
    tq, tk = cfg.tq, cfg.tk
    qend, n_full, n_kv = _chunk_limits(pl.program_id(2), cfg)
    state = (s_ref, p_ref, m_ref, l_ref, alpha_ref, acc_ref)
    _init_state(state)
    strip = min(tq, STRIP_ROWS)
    n_chains = 2 * B_STEP_HEADS

    def step(j, masked):
        ks = pl.multiple_of(j * tk, tk)
        if masked:
            bias_ref[...] = _visibility_bias(ks, qend, (tq, tk))
        qk_cols = [slice(c * HEAD_DIM, (c + 1) * HEAD_DIM) for c in range(n_chains)]
        v_cols = [slice((c // 2) * B_V_DIM, (c // 2 + 1) * B_V_DIM) for c in range(n_chains)]
        _flash_step([q_ref[:, c] for c in qk_cols],
                    [k_ref[pl.ds(ks, tk), c] for c in qk_cols],
                    [v_ref[pl.ds(ks, tk), c] for c in v_cols],
                    (lambda c, r, ch: bias_ref[r * strip:(r + 1) * strip, ch]) if masked
                    else (lambda c, r, ch: None),
                    state, rows=tq, strip=strip)

    _run_blocks(n_full, n_kv, step)
    lam = (jnp.exp(jnp.sum(lq1_ref[...] * lk1_ref[...], axis=-1, keepdims=True))
           - jnp.exp(jnp.sum(lq2_ref[...] * lk2_ref[...], axis=-1, keepdims=True)) + lam_init)
    def normalised(c):
        return acc_ref[c] / jnp.sum(l_ref[c], axis=-1, keepdims=True)

    for h in range(B_STEP_HEADS):
        o = normalised(2 * h) - lam * normalised(2 * h + 1)
        ms = jnp.mean(o * o, axis=-1, keepdims=True)
        o_ref[:, h * B_V_DIM:(h + 1) * B_V_DIM] = (
            o * lax.rsqrt(ms + EPS) * sub_ref[...] * (1.0 - lam_init)).astype(BF16)


def _diff_attention(cfg, q, q_row0, k, v, lam_vecs, subln, lam_init):
    tq, nq, s = cfg.tq, cfg.nq, cfg.s_pad
    qb0 = q_row0 // tq
    width = B_STEP_HEADS * B_V_DIM
    kern = functools.partial(_diff_kernel, cfg=cfg, lam_init=lam_init)
    vec = pl.BlockSpec((1, HEAD_DIM), lambda b, h, i: (0, 0))
    return pl.pallas_call(
        kern, grid=(cfg.batch, B_HEADS // B_STEP_HEADS, nq), name="diff_attention",
        in_specs=[pl.BlockSpec((tq, width), lambda b, h, i: (qb0 + b * nq + i, h)),
                  pl.BlockSpec((s, width), lambda b, h, i: (b, h)),
                  pl.BlockSpec((s, width), lambda b, h, i: (b, h)),
                  vec, vec, vec, vec,
                  pl.BlockSpec((1, B_V_DIM), lambda b, h, i: (0, 0))],
        out_specs=pl.BlockSpec((tq, width), lambda b, h, i: (b * nq + i, h)),
        out_shape=jax.ShapeDtypeStruct((cfg.batch * cfg.t_q, MIX_WIDTH), BF16),
        scratch_shapes=[pltpu.VMEM((tq, cfg.tk), F32)]
        + _flash_scratch(2 * B_STEP_HEADS, tq, cfg.tk, B_V_DIM, with_l=True),
        compiler_params=_params(("parallel", "parallel", "arbitrary")))(
            q, k, v, *[a.reshape(1, HEAD_DIM) for a in lam_vecs], subln.reshape(1, B_V_DIM))


C_STEP_HEADS = 4


def _fox_kernel(q_ref, k_ref, v_ref, cq_ref, ck_ref, o_ref,
                bias_ref, cqh_ref, s_ref, p_ref, m_ref, alpha_ref, acc_ref, *, cfg):
    tq, tk = cfg.tq, cfg.tk
    hg = pl.program_id(1)
    qend, n_full, n_kv = _causal_limits(pl.program_id(2), cfg)
    state = (s_ref, p_ref, m_ref, None, alpha_ref, acc_ref)
    _init_state(state)
    strip = min(tq, STRIP_ROWS)
    lane = lax.broadcasted_iota(I32, (tq, LANES), 1)
    for c in range(C_STEP_HEADS):
        head_lane = lane == hg * C_STEP_HEADS + c
        cqh_ref[c] = jnp.broadcast_to(
            jnp.sum(jnp.where(head_lane, cq_ref[...], 0.0), axis=-1, keepdims=True), (tq, LANES))

    def step(j, masked):
        ks = pl.multiple_of(j * tk, tk)
        if masked:
            bias_ref[...] = _visibility_bias(ks, qend, (tq, tk))

        def bias(c, r, ch):
            rs = slice(r * strip, (r + 1) * strip)
            b = cqh_ref[c, rs, :] - ck_ref[c:c + 1, pl.ds(pl.multiple_of(ks + ch.start, LANES), LANES)]
            return b + bias_ref[rs, ch] if masked else b

        cols = [slice(c * HEAD_DIM, (c + 1) * HEAD_DIM) for c in range(C_STEP_HEADS)]
        _flash_step([q_ref[:, c] for c in cols],
                    [k_ref[pl.ds(ks, tk), c] for c in cols],
                    [_with_ones(v_ref[pl.ds(ks, tk), c]) for c in cols],
                    bias, state, rows=tq, strip=strip)

    _run_blocks(n_full, n_kv, step)
    for c in range(C_STEP_HEADS):
        o_ref[:, c * HEAD_DIM:(c + 1) * HEAD_DIM] = (
            acc_ref[c, :, :HEAD_DIM] / acc_ref[c, :, HEAD_DIM:]).astype(BF16)


def _fox_attention(cfg, q, q_row0, k, v, cq, ck_rows):
    tq, nq, s = cfg.tq, cfg.nq, cfg.s_pad
    qb0 = q_row0 // tq
    width = C_STEP_HEADS * HEAD_DIM
    groups = C_HEADS // C_STEP_HEADS
    kern = functools.partial(_fox_kernel, cfg=cfg)
    return pl.pallas_call(
        kern, grid=(cfg.batch, groups, nq), name="fox_attention",
        in_specs=[pl.BlockSpec((tq, width), lambda b, h, i: (qb0 + b * nq + i, h)),
                  pl.BlockSpec((s, width), lambda b, h, i: (b, h)),
                  pl.BlockSpec((s, width), lambda b, h, i: (b, h)),
                  pl.BlockSpec((tq, LANES), lambda b, h, i: (qb0 + b * nq + i, 0)),
                  pl.BlockSpec((None, C_STEP_HEADS, s), lambda b, h, i: (b * groups + h, 0, 0))],
        out_specs=pl.BlockSpec((tq, width), lambda b, h, i: (b * nq + i, h)),
        out_shape=jax.ShapeDtypeStruct((cfg.batch * cfg.t_q, MIX_WIDTH), BF16),
        scratch_shapes=[pltpu.VMEM((tq, cfg.tk), F32), pltpu.VMEM((C_STEP_HEADS, tq, LANES), F32)]
        + _flash_scratch(C_STEP_HEADS, tq, cfg.tk, 2 * HEAD_DIM, with_l=False),
        compiler_params=_params(("parallel", "parallel", "arbitrary")))(q, k, v, cq, ck_rows)


def _cumsum_kernel(x_ref, o_ref, carry_ref, *, tb):
    @pl.when(pl.program_id(1) == 0)
    def _():
        carry_ref[...] = jnp.zeros(carry_ref.shape, F32)

    x = x_ref[...]
    hi = x.astype(BF16)
    r1 = x - hi.astype(F32)
    mid = r1.astype(BF16)
    low = (r1 - mid.astype(F32)).astype(BF16)
    tri = jnp.where(lax.broadcasted_iota(I32, (tb, tb), 0) >= lax.broadcasted_iota(I32, (tb, tb), 1),
                    1.0, 0.0).astype(BF16)
    c = (jnp.dot(tri, hi, preferred_element_type=F32) + jnp.dot(tri, mid, preferred_element_type=F32)
         + jnp.dot(tri, low, preferred_element_type=F32)) + carry_ref[...]
    o_ref[...] = c * LOG2E
    carry_ref[...] = c[tb - 1:tb, :]


def _cumsum_rows(x, batch, t, tb):
    nt = t // tb
    return pl.pallas_call(
        functools.partial(_cumsum_kernel, tb=tb), grid=(batch, nt), name="cumsum_rows",
        in_specs=[pl.BlockSpec((tb, LANES), lambda b, j: (b * nt + j, 0))],
        out_specs=pl.BlockSpec((tb, LANES), lambda b, j: (b * nt + j, 0)),
        out_shape=jax.ShapeDtypeStruct((batch * t, LANES), F32),
        scratch_shapes=[pltpu.VMEM((1, LANES), F32)],
        compiler_params=_params(("parallel", "arbitrary")))(x)


def _split_w(w_in, sizes):
    offs = np.cumsum((0,) + tuple(sizes))
    return [w_in[:, int(offs[i]):int(offs[i + 1])] for i in range(len(sizes))]


def _pad_rows(a, rows):
    return jnp.pad(a, ((0, 0), (0, rows - a.shape[1])) + ((0, 0),) * (a.ndim - 2))


def _sample_keys(cache, new, s_pad):
    b = cache.shape[0]
    full = jnp.concatenate([cache.reshape(b, cache.shape[1], -1), new.reshape(b, new.shape[1], -1)], axis=1)
    return _pad_rows(full, s_pad).astype(BF16).reshape(b * s_pad, -1)


def _key_transposed_pairs(ki, batch, s):
    kt = jnp.swapaxes(ki.reshape(batch, s, IDX_DIM), 1, 2).astype(BF16)
    z = jnp.zeros_like(kt)
    return jnp.stack([jnp.concatenate([kt, z], axis=1), jnp.concatenate([z, kt], axis=1)], axis=1)


def _layer_a(pr, sr, cfg_p, cfg_s, xp, xs, ck, cv, cki, norm, w_in, w_out, qn, kn, ikn):
    w_all = w_in.astype(BF16)
    wq, wk, wv, wqi = _Cols(w_all, 2048, 0), _Cols(w_all, 512, 4), _Cols(w_all, 512, 5), _Cols(w_all, 1024, 3)
    wsm = _Cols(w_all, LANES, sum(A_SIZES[:4]) // LANES)
    wg = _Cols(_split_w(w_all, A_SIZES)[6], MIX_WIDTH)
    wo = w_out

    def project(rows, x):
        (q,), (qi,), (sm,) = _project(rows, x, norm, [
            _seg_qk(rows, wq, qn, scale=HEAD_DIM ** -0.5 * LOG2E, use_rope=True, head_major=True, emit_f32=False),
            _seg_rope64(rows, wqi), _seg_idx_small(rows, wsm, ikn)], "proj_a_query")
        (k_f, k_b), (v_f, v_b), (sg,) = _project(rows, x, norm, [
            _seg_qk(rows, wk, kn, scale=1.0, use_rope=True, head_major=False, emit_f32=True),
            _seg_plain(rows, wv), _seg_gate(rows, wg)], "proj_a_kv_gate")
        return q, k_f, k_b, v_f, v_b, qi, sm, sg

    q, k_f, k_b, v_f, v_b, qi, sm, sg = project(pr, xp)
    ki_f = sm[:, :IDX_DIM]
    o = _dsa_attention(cfg_p, q, 0, qi, sm, k_b, v_b,
                       _key_transposed_pairs(ki_f, pr.batch, pr.t_q), min(TOPK_MAX, pr.t_q // 4))
    yp = _out_proj(pr, xp, o, sg, wo)

    q2, k2_f, _, v2_f, _, qi2, sm2, sg2 = project(sr, xs)
    ki2_f = sm2[:, :IDX_DIM]
    b, s = sr.batch, sr.t_q
    k_all = _sample_keys(ck, k2_f.reshape(b, s, -1), cfg_s.s_pad)
    v_all = _sample_keys(cv, v2_f.reshape(b, s, -1), cfg_s.s_pad)
    ki_all = _pad_rows(jnp.concatenate([cki, ki2_f.reshape(b, s, IDX_DIM)], axis=1), cfg_s.s_pad)
    o2 = _dsa_attention(cfg_s, q2, 0, qi2, sm2, k_all, v_all,
                        _key_transposed_pairs(ki_all.reshape(-1, IDX_DIM), b, cfg_s.s_pad),
                        min(TOPK_MAX, cfg_s.s_valid // 4))
    ys = _out_proj(sr, xs, o2, sg2, wo)
    kv = (A_KV_HEADS, HEAD_DIM)
    state = (k_f.reshape(pr.batch, pr.t_q, *kv), v_f.reshape(pr.batch, pr.t_q, *kv),
             ki_f.reshape(pr.batch, pr.t_q, IDX_DIM),
             k2_f.reshape(b, s, *kv), v2_f.reshape(b, s, *kv), ki2_f.reshape(b, s, IDX_DIM))
    return yp, ys, state


def _layer_b(layer, pr, sr, cfg_p, cfg_s, xp, xs, ck, cv, norm, w_in, w_out, qn, kn,
             lq1, lk1, lq2, lk2, subln):
    lam_init = 0.8 - 0.6 * float(np.exp(-0.3 * layer))
    w_all = w_in.astype(BF16)
    wq, wk, wv, wg = [_Cols(w_all, 2048, j) for j in range(4)]
    wo = w_out

    def project(rows, x):
        (q,), (sg,) = _project(rows, x, norm, [
            _seg_qk(rows, wq, qn, scale=HEAD_DIM ** -0.5 * LOG2E, use_rope=True, head_major=False, emit_f32=False),
            _seg_gate(rows, wg)], "proj_b_query_gate")
        (k_f, k_b), (v_f, v_b) = _project(rows, x, norm, [
            _seg_qk(rows, wk, kn, scale=1.0, use_rope=True, head_major=False, emit_f32=True),
            _seg_plain(rows, wv)], "proj_b_kv")
        return q, k_f, k_b, v_f, v_b, sg

    lam_vecs = (lq1, lk1, lq2, lk2)
    q, k_f, k_b, v_f, v_b, sg = project(pr, xp)
    o = _diff_attention(cfg_p, q, 0, k_b, v_b, lam_vecs, subln, lam_init)
    yp = _out_proj(pr, xp, o, sg, wo)

    q2, k2_f, _, v2_f, _, sg2 = project(sr, xs)
    b, s = sr.batch, sr.t_q
    k_all = _sample_keys(ck, k2_f.reshape(b, s, -1), cfg_s.s_pad)
    v_all = _sample_keys(cv, v2_f.reshape(b, s, -1), cfg_s.s_pad)
    o2 = _diff_attention(cfg_s, q2, 0, k_all, v_all, lam_vecs, subln, lam_init)
    ys = _out_proj(sr, xs, o2, sg2, wo)
    state = (k_f.reshape(pr.batch, pr.t_q, 2 * B_HEADS, HEAD_DIM), v_f.reshape(pr.batch, pr.t_q, B_HEADS, B_V_DIM),
             k2_f.reshape(b, s, 2 * B_HEADS, HEAD_DIM), v2_f.reshape(b, s, B_HEADS, B_V_DIM))
    return yp, ys, state


def _head_rows(c, batch, s):
    rows = jnp.swapaxes(c.reshape(batch, s, LANES)[:, :, :C_HEADS], 1, 2)
    return rows.reshape(batch * C_HEADS // C_STEP_HEADS, C_STEP_HEADS, s)


def _layer_c(pr, sr, cfg_p, cfg_s, xp, xs, ck, cv, clogf, norm, w_in, w_out, qn, kn, fb):
    w_all = w_in.astype(BF16)
    wq, wk, wv = [_Cols(w_all, 2048, j) for j in range(3)]
    wf = _Cols(w_all, LANES, sum(C_SIZES[:3]) // LANES)
    wg = _Cols(_split_w(w_all, C_SIZES)[4], MIX_WIDTH)
    wo = w_out

    def project(rows, x):
        (q,), (logf,), (sg,) = _project(rows, x, norm, [
            _seg_qk(rows, wq, qn, scale=HEAD_DIM ** -0.5 * LOG2E, use_rope=False, head_major=False, emit_f32=False),
            _seg_logf(rows, wf, fb), _seg_gate(rows, wg)], "proj_c_query_gate")
        (k_f, k_b), (v_f, v_b) = _project(rows, x, norm, [
            _seg_qk(rows, wk, kn, scale=1.0, use_rope=False, head_major=False, emit_f32=True),
            _seg_plain(rows, wv)], "proj_c_kv")
        return q, k_f, k_b, v_f, v_b, logf, sg

    q, k_f, k_b, v_f, v_b, logf, sg = project(pr, xp)
    c = _cumsum_rows(logf, pr.batch, pr.t_q, 512)
    o = _fox_attention(cfg_p, q, 0, k_b, v_b, c, _head_rows(c, pr.batch, pr.t_q))
    yp = _out_proj(pr, xp, o, sg, wo)

    q2, k2_f, _, v2_f, _, logf2, sg2 = project(sr, xs)
    b, s = sr.batch, sr.t_q
    k_all = _sample_keys(ck, k2_f.reshape(b, s, -1), cfg_s.s_pad)
    v_all = _sample_keys(cv, v2_f.reshape(b, s, -1), cfg_s.s_pad)
    logf_all = jnp.concatenate([jnp.pad(clogf.astype(F32), ((0, 0), (0, 0), (0, LANES - C_HEADS))),
                                logf2.reshape(b, s, LANES)], axis=1)
    c2 = _cumsum_rows(_pad_rows(logf_all, cfg_s.s_pad).reshape(b * cfg_s.s_pad, LANES), b, cfg_s.s_pad, LANES)
    cq2 = c2.reshape(b, cfg_s.s_pad, LANES)[:, PAST_LEN:PAST_LEN + s].reshape(b * s, LANES)
    o2 = _fox_attention(cfg_s, q2, 0, k_all, v_all, cq2, _head_rows(c2, b, cfg_s.s_pad))
    ys = _out_proj(sr, xs, o2, sg2, wo)
    hd = (C_HEADS, HEAD_DIM)
    state = (k_f.reshape(pr.batch, pr.t_q, *hd), v_f.reshape(pr.batch, pr.t_q, *hd),
             logf[:, :C_HEADS].reshape(pr.batch, pr.t_q, C_HEADS),
             k2_f.reshape(b, s, *hd), v2_f.reshape(b, s, *hd), logf2[:, :C_HEADS].reshape(b, s, C_HEADS))
    return yp, ys, state


def kernel(x_prompt, x_sample, cache_l0_k, cache_l0_v, cache_l0_kidx, cache_l1_k, cache_l1_v, cache_l2_k, cache_l2_v, cache_l2_logf, cache_l3_k, cache_l3_v, cache_l3_kidx, l0_norm, l0_w_in, l0_w_out, l0_q_norm, l0_k_norm, l0_idx_k_norm, l1_norm, l1_w_in, l1_w_out, l1_q_norm, l1_k_norm, l1_lambda_q1, l1_lambda_k1, l1_lambda_q2, l1_lambda_k2, l1_subln, l2_norm, l2_w_in, l2_w_out, l2_q_norm, l2_k_norm, l2_forget_bias, l3_norm, l3_w_in, l3_w_out, l3_q_norm, l3_k_norm, l3_idx_k_norm):
    bp, tp, d = x_prompt.shape
    bs, ts, _ = x_sample.shape
    s_valid = PAST_LEN + ts
    s_pad = -(-s_valid // LANES) * LANES
    pr = _Rows(bp, tp, 0, 512)
    sr = _Rows(bs, ts, PAST_LEN, bs * ts)
    a_p = _Attn(bp, tp, tp, tp, 0, 256, 512)
    a_s = _Attn(bs, ts, s_pad, s_valid, PAST_LEN, ts, s_pad)
    bc_p = _Attn(bp, tp, tp, tp, 0, 512, 512)
    bc_s = a_s

    xp = x_prompt.reshape(bp * tp, d)
    xs = x_sample.reshape(bs * ts, d)
    xp, xs, st0 = _layer_a(pr, sr, a_p, a_s, xp, xs, cache_l0_k, cache_l0_v, cache_l0_kidx,
                           l0_norm, l0_w_in, l0_w_out, l0_q_norm, l0_k_norm, l0_idx_k_norm)
    xp, xs, st1 = _layer_b(1, pr, sr, bc_p, bc_s, xp, xs, cache_l1_k, cache_l1_v,
                           l1_norm, l1_w_in, l1_w_out, l1_q_norm, l1_k_norm,
                           l1_lambda_q1, l1_lambda_k1, l1_lambda_q2, l1_lambda_k2, l1_subln)
    xp, xs, st2 = _layer_c(pr, sr, bc_p, bc_s, xp, xs, cache_l2_k, cache_l2_v, cache_l2_logf,
                           l2_norm, l2_w_in, l2_w_out, l2_q_norm, l2_k_norm, l2_forget_bias)
    xp, xs, st3 = _layer_a(pr, sr, a_p, a_s, xp, xs, cache_l3_k, cache_l3_v, cache_l3_kidx,
                           l3_norm, l3_w_in, l3_w_out, l3_q_norm, l3_k_norm, l3_idx_k_norm)
    return (xp.reshape(bp, tp, d), xs.reshape(bs, ts, d)) + st0 + st1 + st2 + st3
```

```python
import functools

import numpy as np
import jax
import jax.numpy as jnp
from jax import lax
from jax.experimental import pallas as pl
from jax.experimental.pallas import tpu as pltpu

F32 = jnp.float32
BF16 = jnp.bfloat16
I32 = jnp.int32

D_MODEL = 2048
PAST_LEN = 1024
CHUNK_SHIFT = 6
ROPE_THETA = 10000.0
EPS = 1e-6
HEAD_DIM = 128
A_HEADS = 16
A_KV_HEADS = 4
A_REP = A_HEADS // A_KV_HEADS
IDX_HEADS = 16
IDX_DIM = 64
TOPK_MAX = 256
B_HEADS = 8
B_V_DIM = 256
C_HEADS = 16
MIX_WIDTH = D_MODEL

A_SIZES = (2048, 512, 512, 1024, 64, 16, 2048)
B_SIZES = (2048, 2048, 2048, 2048)
C_SIZES = (2048, 2048, 2048, 16, 2048)

LANES = 128
MXU_N = 256
VMEM_LIMIT = 56 * 1024 * 1024
MASKED = -1e30
LOG2E = 1.4426950408889634
STRIP_ROWS = 32
BISECT_UNROLL = 3
SEARCH_ROWS = 32
INT_MIN = -2 ** 31
SCORE_ROWS = 128
NEG_INF_KEY = -2139095041


def _params(sem):
    return pltpu.CompilerParams(dimension_semantics=sem, vmem_limit_bytes=VMEM_LIMIT)


def _norm_rows(x_ref, g_ref):
    x = x_ref[...]
    ms = jnp.mean(x * x, axis=-1, keepdims=True)
    return (x * lax.rsqrt(ms + EPS) * g_ref[...]).astype(BF16)


def _rope128(y, cos, sin):
    return y * cos + pltpu.roll(y, 64, 1) * sin


def _rope64(y, cos, sin):
    lane = lax.broadcasted_iota(I32, y.shape, 1)
    rot = jnp.where((lane & 63) < 32, pltpu.roll(y, 96, 1), pltpu.roll(y, 32, 1))
    return y * cos + rot * sin


class _Cols:
    def __init__(self, array, n, block=0):
        assert array.shape[1] >= (block + 1) * n
        self.array, self.n, self.block = array, n, block


class _Rows:
    def __init__(self, batch, t_q, pos0, tm):
        self.batch, self.t_q, self.pos0, self.tm = batch, t_q, pos0, tm
        self.m = batch * t_q
        self.tab_blocks = max(t_q // tm, 1)

    def tab_spec(self):
        nb = self.tab_blocks
        return pl.BlockSpec((self.tm, LANES), lambda i: (i % nb, 0))

    def tables(self, d):
        period = max(self.t_q, self.tm)
        pos = self.pos0 + (jnp.arange(period) % self.t_q)
        half = d // 2
        inv = ROPE_THETA ** (-2.0 * jnp.arange(half, dtype=F32) / d)
        ang = pos.astype(F32)[:, None] * inv[None, :]
        cos, sin = jnp.cos(ang), jnp.sin(ang)
        reps = LANES // d
        cos_f = jnp.tile(jnp.concatenate([cos, cos], axis=-1), (1, reps))
        sin_f = jnp.tile(jnp.concatenate([-sin, sin], axis=-1), (1, reps))
        return cos_f, sin_f


def _row_spec(tm, n):
    return pl.BlockSpec((tm, n), lambda i: (i, 0))


def _vec_spec(n):
    return pl.BlockSpec((1, n), lambda i: (0, 0))


class _Seg:
    def __init__(self, emit, w, aux, aux_specs, out_shape, out_specs):
        self.emit, self.w = emit, w
        self.aux, self.aux_specs = list(aux), list(aux_specs)
        self.out_shape, self.out_specs = list(out_shape), list(out_specs)


def _emit_qk(h, w_ref, aux, outs, *, n_heads, scale, use_rope, head_major, emit_f32):
    cos_ref, sin_ref, hg_ref = aux
    hg = hg_ref[...]
    of_ref = outs[0] if emit_f32 else None
    ob_ref = outs[-1]
    for c in range(n_heads // 2):
        y = jnp.dot(h, w_ref[:, c * MXU_N:(c + 1) * MXU_N], preferred_element_type=F32)
        for j in range(2):
            hd = 2 * c + j
            yh = y[:, j * LANES:(j + 1) * LANES]
            ms = jnp.mean(yh * yh, axis=-1, keepdims=True)
            yr = yh * lax.rsqrt(ms + EPS) * hg
            if use_rope:
                yr = _rope128(yr, cos_ref[...], sin_ref[...])
            if emit_f32 and len(of_ref.shape) == 3:
                of_ref[:, hd, :] = yr
            elif emit_f32:
                of_ref[:, hd * LANES:(hd + 1) * LANES] = yr
            yb = (yr * scale).astype(BF16)
            if head_major:
                ob_ref[hd] = yb
            else:
                ob_ref[:, hd * LANES:(hd + 1) * LANES] = yb


def _emit_plain(h, w_ref, aux, outs, *, n_cols, head_dim):
    of_ref, ob_ref = outs
    per_chunk = MXU_N // head_dim
    for c in range(n_cols // MXU_N):
        sl = slice(c * MXU_N, (c + 1) * MXU_N)
        y = jnp.dot(h, w_ref[:, sl], preferred_element_type=F32)
        if len(of_ref.shape) == 3:
            for j in range(per_chunk):
                of_ref[:, c * per_chunk + j, :] = y[:, j * head_dim:(j + 1) * head_dim]
        else:
            of_ref[:, sl] = y
        ob_ref[:, sl] = y.astype(BF16)


def _emit_gate(h, w_ref, aux, outs, *, n_cols):
    for c in range(n_cols // MXU_N):
        sl = slice(c * MXU_N, (c + 1) * MXU_N)
        y = jnp.dot(h, w_ref[:, sl], preferred_element_type=F32)
        outs[0][:, sl] = (y * (1.0 / (1.0 + jnp.exp(-y)))).astype(BF16)


def _emit_rope64(h, w_ref, aux, outs, *, n_cols):
    cos, sin = aux[0][...], aux[1][...]
    for c in range(n_cols // MXU_N):
        y = jnp.dot(h, w_ref[:, c * MXU_N:(c + 1) * MXU_N], preferred_element_type=F32)
        for j in range(2):
            col = c * MXU_N + j * LANES
            outs[0][:, col:col + LANES] = _rope64(y[:, j * LANES:(j + 1) * LANES], cos, sin).astype(BF16)


def _emit_idx_small(h, w_ref, aux, outs):
    cos_ref, sin_ref, hg_ref = aux
    y = jnp.dot(h, w_ref[...], preferred_element_type=F32)
    is_key = lax.broadcasted_iota(I32, y.shape, 1) < IDX_DIM
    ms = jnp.sum(jnp.where(is_key, y * y, 0.0), axis=-1, keepdims=True) * (1.0 / IDX_DIM)
    kn = _rope64(y * lax.rsqrt(ms + EPS) * hg_ref[...], cos_ref[...], sin_ref[...])
    outs[0][...] = jnp.where(is_key, kn, y)


def _emit_logf(h, w_ref, aux, outs):
    z = jnp.dot(h, w_ref[...], preferred_element_type=F32) + aux[0][...]
    outs[0][...] = jnp.minimum(z, 0.0) - jnp.log1p(jnp.exp(-jnp.abs(z)))


def _proj_kernel(x_ref, g_ref, *refs, segs):
    h = _norm_rows(x_ref, g_ref)
    n_in = sum(1 + len(s.aux) for s in segs)
    ins, outs = refs[:n_in], refs[n_in:]
    i = o = 0
    for s in segs:
        s.emit(h, ins[i], ins[i + 1:i + 1 + len(s.aux)], outs[o:o + len(s.out_shape)])
        i += 1 + len(s.aux)
        o += len(s.out_shape)


def _project(rows, x, g, segs, name):
    m, d = x.shape
    tm = rows.tm
    in_specs = [pl.BlockSpec((tm, d), lambda i: (i, 0)), pl.BlockSpec((1, d), lambda i: (0, 0))]
    args = [x, g.reshape(1, d)]
    for s in segs:
        block = s.w.block
        in_specs.append(pl.BlockSpec((d, s.w.n), lambda i, block=block: (0, block), pipeline_mode=pl.Buffered(1)))
        in_specs += s.aux_specs
        args += [s.w.array] + s.aux
    out = pl.pallas_call(
        functools.partial(_proj_kernel, segs=segs), grid=(m // tm,), in_specs=in_specs, name=name,
        out_specs=[sp for s in segs for sp in s.out_specs],
        out_shape=[sh for s in segs for sh in s.out_shape],
        compiler_params=_params(("parallel",)))(*args)
    res, o = [], 0
    for s in segs:
        res.append(out[o:o + len(s.out_shape)])
        o += len(s.out_shape)
    return res


def _state_out(rows, n, head_dim, cache_layout):
    if cache_layout:
        return (jax.ShapeDtypeStruct((rows.m, n // head_dim, head_dim), F32),
                pl.BlockSpec((rows.tm, n // head_dim, head_dim), lambda i: (i, 0, 0)))
    return jax.ShapeDtypeStruct((rows.m, n), F32), _row_spec(rows.tm, n)


def _seg_qk(rows, w, head_gain, *, scale, use_rope, head_major, emit_f32, cache_layout=False):
    n = w.n
    n_heads = n // HEAD_DIM
    cos, sin = rows.tables(HEAD_DIM)
    out_shape, out_specs = [], []
    if emit_f32:
        shape, spec = _state_out(rows, n, HEAD_DIM, cache_layout)
        out_shape.append(shape)
        out_specs.append(spec)
    if head_major:
        out_shape.append(jax.ShapeDtypeStruct((n_heads, rows.m, HEAD_DIM), BF16))
        out_specs.append(pl.BlockSpec((n_heads, rows.tm, HEAD_DIM), lambda i: (0, i, 0)))
    else:
        out_shape.append(jax.ShapeDtypeStruct((rows.m, n), BF16))
        out_specs.append(_row_spec(rows.tm, n))
    emit = functools.partial(_emit_qk, n_heads=n_heads, scale=scale, use_rope=use_rope,
                             head_major=head_major, emit_f32=emit_f32)
    return _Seg(emit, w, (cos, sin, head_gain.reshape(1, HEAD_DIM)),
                (rows.tab_spec(), rows.tab_spec(), _vec_spec(HEAD_DIM)), out_shape, out_specs)


def _seg_plain(rows, w, head_dim, cache_layout):
    n = w.n
    shape, spec = _state_out(rows, n, head_dim, cache_layout)
    return _Seg(functools.partial(_emit_plain, n_cols=n, head_dim=head_dim), w, (), (),
                [shape, jax.ShapeDtypeStruct((rows.m, n), BF16)], [spec, _row_spec(rows.tm, n)])


def _seg_gate(rows, w):
    n = w.n
    return _Seg(functools.partial(_emit_gate, n_cols=n), w, (), (),
                [jax.ShapeDtypeStruct((rows.m, n), BF16)], [_row_spec(rows.tm, n)])


def _seg_rope64(rows, w):
    n = w.n
    cos, sin = rows.tables(IDX_DIM)
    return _Seg(functools.partial(_emit_rope64, n_cols=n), w, (cos, sin), (rows.tab_spec(), rows.tab_spec()),
                [jax.ShapeDtypeStruct((rows.m, n), BF16)], [_row_spec(rows.tm, n)])


def _seg_idx_small(rows, w, key_gain):
    cos, sin = rows.tables(IDX_DIM)
    gain = jnp.concatenate([key_gain, jnp.zeros((LANES - IDX_DIM,), F32)]).reshape(1, LANES)
    return _Seg(_emit_idx_small, w, (cos, sin, gain), (rows.tab_spec(), rows.tab_spec(), _vec_spec(LANES)),
                [jax.ShapeDtypeStruct((rows.m, LANES), F32)], [_row_spec(rows.tm, LANES)])


def _seg_logf(rows, w, fb):
    fbp = jnp.concatenate([fb, jnp.zeros((LANES - C_HEADS,), F32)]).reshape(1, LANES)
    return _Seg(_emit_logf, w, (fbp,), (_vec_spec(LANES),),
                [jax.ShapeDtypeStruct((rows.m, LANES), F32)], [_row_spec(rows.tm, LANES)])


def _out_kernel(x_ref, o_ref, sg_ref, w_ref, y_ref, *, n_cols):
    a = o_ref[...] * sg_ref[...]
    for c in range(n_cols // MXU_N):
        sl = slice(c * MXU_N, (c + 1) * MXU_N)
        y_ref[:, sl] = x_ref[:, sl] + jnp.dot(a, w_ref[:, sl].astype(BF16), preferred_element_type=F32)


def _out_proj(rows, x, o, sg, w):
    m, d = x.shape
    tm = rows.tm
    kern = functools.partial(_out_kernel, n_cols=d)
    return pl.pallas_call(
        kern, grid=(m // tm,), name="out_proj",
        in_specs=[_row_spec(tm, d), _row_spec(tm, d), _row_spec(tm, d),
                  pl.BlockSpec((d, d), lambda i: (0, 0), pipeline_mode=pl.Buffered(1))],
        out_specs=_row_spec(tm, d), out_shape=jax.ShapeDtypeStruct((m, d), F32),
        compiler_params=_params(("parallel",)))(x, o, sg, w)


class _Attn:
    def __init__(self, batch, t_q, s_pad, s_valid, q_off, tq, tk):
        self.batch, self.t_q, self.s_pad, self.s_valid = batch, t_q, s_pad, s_valid
        self.q_off, self.tq, self.tk = q_off, tq, tk
        self.nq = t_q // tq


def _chunk_limits(i, cfg):
    q0 = cfg.q_off + i * cfg.tq
    qpos = q0 + lax.broadcasted_iota(I32, (cfg.tq, 1), 0)
    qend = jnp.minimum(((qpos >> CHUNK_SHIFT) + 1) << CHUNK_SHIFT, cfg.s_valid)
    kmin = jnp.minimum(((q0 >> CHUNK_SHIFT) + 1) << CHUNK_SHIFT, cfg.s_valid)
    kend = jnp.minimum((((q0 + cfg.tq - 1) >> CHUNK_SHIFT) + 1) << CHUNK_SHIFT, cfg.s_valid)
    return qend, kmin // cfg.tk, (kend + cfg.tk - 1) // cfg.tk


def _causal_limits(i, cfg):
    q0 = cfg.q_off + i * cfg.tq
    qpos = q0 + lax.broadcasted_iota(I32, (cfg.tq, 1), 0)
    return qpos + 1, (q0 + 1) // cfg.tk, (q0 + cfg.tq + cfg.tk - 1) // cfg.tk


def _qk(q, k):
    return lax.dot_general(q, k, (((1,), (1,)), ((), ())), preferred_element_type=F32)


def _visibility_bias(ks, qend, shape):
    kpos = ks + lax.broadcasted_iota(I32, shape, 1)
    return jnp.where(kpos < qend, 0.0, MASKED)


def _flash_step(qs, ks, vs, bias_fn, state, *, rows, strip):
    s_ref, p_ref, m_ref, l_ref, alpha_ref, acc_ref = state
    n = len(qs)
    strips = [slice(r * strip, (r + 1) * strip) for r in range(rows // strip)]
    chunks = [slice(a, a + LANES) for a in range(0, s_ref.shape[-1], LANES)]
    for c in range(n):
        s_ref[c] = _qk(qs[c], ks[c])
    for c in range(n):
        for r, rs in enumerate(strips):
            mx = None
            for ch in chunks:
                s = s_ref[c, rs, ch]
                b = bias_fn(c, r, ch)
                if b is not None:
                    s = s + b
                    s_ref[c, rs, ch] = s
                mx = s if mx is None else jnp.maximum(mx, s)
            m_old = m_ref[c, rs, :]
            m_new = jnp.maximum(m_old, jnp.broadcast_to(jnp.max(mx, axis=-1, keepdims=True), mx.shape))
            alpha_ref[c, rs, :] = jnp.exp2(m_old - m_new)
            m_ref[c, rs, :] = m_new
        for rs in strips:
            m = m_ref[c, rs, :]
            psum = None
            for ch in chunks:
                p = jnp.exp2(s_ref[c, rs, ch] - m)
                if l_ref is not None:
                    psum = p if psum is None else psum + p
                p_ref[c, rs, ch] = p.astype(BF16)
            if l_ref is not None:
                l_ref[c, rs, :] = alpha_ref[c, rs, :] * l_ref[c, rs, :] + psum
    for c in range(n):
        alpha = jnp.concatenate([alpha_ref[c]] * (acc_ref.shape[-1] // LANES), axis=1)
        acc_ref[c] = alpha * acc_ref[c] + jnp.dot(p_ref[c], vs[c], preferred_element_type=F32)


def _init_state(state):
    _, _, m_ref, l_ref, _, acc_ref = state
    m_ref[...] = jnp.full(m_ref.shape, MASKED, F32)
    acc_ref[...] = jnp.zeros(acc_ref.shape, F32)
    if l_ref is not None:
        l_ref[...] = jnp.zeros(l_ref.shape, F32)


def _flash_scratch(chains, rows, tk, dv, with_l):
    stat = pltpu.VMEM((chains, rows, LANES), F32)
    return ([pltpu.VMEM((chains, rows, tk), F32), pltpu.VMEM((chains, rows, tk), BF16), stat]
            + ([stat] if with_l else []) + [stat, pltpu.VMEM((chains, rows, dv), F32)])


def _with_ones(v):
    return jnp.concatenate([v, jnp.ones(v.shape, v.dtype)], axis=1)


def _dsa_kernel(q_ref, qi_ref, sm_ref, k_ref, v_ref, kit_ref, o_ref,
                key_ref, bias_ref, stat_ref, look_ref, s_ref, p_ref, m_ref, alpha_ref, acc_ref, *, cfg, topk):
    tq, tk = cfg.tq, cfg.tk
    i = pl.program_id(1)
    qend, _, n_kv = _chunk_limits(i, cfg)
    sub = min(tq, SCORE_ROWS)
    stat_ref[0] = jnp.full((tq, LANES), INT_MIN, I32)
    stat_ref[1] = jnp.zeros((tq, LANES), I32)

    def score_block(j, carry):
        ks = pl.multiple_of(j * tk, tk)
        for r0 in range(0, tq, sub):
            rs = slice(r0, r0 + sub)
            w = sm_ref[rs, IDX_DIM:IDX_DIM + IDX_HEADS] * (IDX_DIM ** -0.5 * IDX_HEADS ** -0.5)
            acc = jnp.zeros((sub, tk), F32)
            for pair in range(IDX_HEADS // 2):
                qp = qi_ref[rs, pair * LANES:(pair + 1) * LANES]
                for e in range(2):
                    hd = 2 * pair + e
                    sc = jnp.dot(qp, kit_ref[e, :, pl.ds(ks, tk)], preferred_element_type=F32)
                    acc = acc + w[:, hd:hd + 1] * jnp.maximum(sc, 0.0)
            bits = lax.bitcast_convert_type(acc, I32)
            key = bits ^ ((bits >> 31) & 0x7FFFFFFF)
            key = jnp.where(acc == 0.0, 0, key)
            kpos = ks + lax.broadcasted_iota(I32, (sub, tk), 1)
            key = jnp.where(kpos < qend[rs], key, NEG_INF_KEY)
            key_ref[rs, pl.ds(ks, tk)] = key
            kmax, nfin = stat_ref[0, rs, :], stat_ref[1, rs, :]
            for c in range(tk // LANES):
                kc = key[:, c * LANES:(c + 1) * LANES]
                kmax = jnp.maximum(kmax, kc)
                nfin = nfin + jnp.where(kc > NEG_INF_KEY, 1, 0)
            stat_ref[0, rs, :], stat_ref[1, rs, :] = kmax, nfin
        return carry

    lax.fori_loop(0, n_kv, score_block, 0)

    def key_chunk(j, c):
        return key_ref[:, pl.ds(pl.multiple_of(j * tk + c * LANES, LANES), LANES)]

    def lane_total(x):
        return jnp.broadcast_to(jnp.sum(x, axis=-1, keepdims=True), x.shape)

    def count_ge(cand):
        def body(j, part):
            ks = pl.multiple_of(j * tk, tk)
            ge = jnp.where(key_ref[:, pl.ds(ks, tk)] >= cand, 1, 0)
            for c in range(tk // LANES):
                part = part + ge[:, c * LANES:(c + 1) * LANES]
            return part
        part = lax.fori_loop(0, n_kv, body, jnp.zeros((tq, LANES), I32))
        return jnp.sum(part, axis=-1, keepdims=True)

    n_fin = lane_total(stat_ref[1])
    small = n_fin <= topk
    look_ref[0] = jnp.full((tq, LANES), NEG_INF_KEY + 1, I32)
    look_ref[1] = jnp.broadcast_to(jnp.max(stat_ref[0], axis=-1, keepdims=True), (tq, LANES)) + 1
    look_ref[2] = n_fin

    def pending():
        lo, hi, cnt = look_ref[0], look_ref[1], look_ref[2]
        open_row = (cnt != topk) & (hi != lo + 1) & jnp.logical_not(small)
        return jnp.max(jnp.where(open_row, 1, 0))

    groups = [slice(r, r + min(tq, SEARCH_ROWS)) for r in range(0, tq, min(tq, SEARCH_ROWS))]

    def halve(n_blocks):
        state = [[look_ref[v, gs, :] for v in range(3)] for gs in groups]
        part = jnp.zeros((groups[0].stop, LANES), I32)
        for _ in range(BISECT_UNROLL):
            for g, gs in enumerate(groups):
                lo, hi, cnt = state[g]
                mid = (lo >> 1) + (hi >> 1) + (lo & hi & 1)
                part = part >> 31
                for a in range(0, n_blocks * tk, LANES):
                    part = part + jnp.where(key_ref[gs, a:a + LANES] >= mid, 1, 0)
                c = lane_total(part)
                up = c >= topk
                state[g] = [jnp.where(up, mid, lo), jnp.where(up, hi, mid), jnp.where(up, c, cnt)]
        for g, gs in enumerate(groups):
            for v in range(3):
                look_ref[v, gs, :] = state[g][v]

    def search_round(_):
        for n_blocks in range(1, cfg.s_pad // tk + 1):
            pl.when(n_kv == n_blocks)(functools.partial(halve, n_blocks))
        return pending()

    lax.while_loop(lambda flag: flag > 0, search_round, pending())
    thr, n_ge = look_ref[0], look_ref[2]
    tie_rows = (n_ge > topk) & jnp.logical_not(small)

    @pl.when(jnp.max(jnp.where(tie_rows, 1, 0)) > 0)
    def _drop_late_ties():
        need_f = jnp.broadcast_to((topk - count_ge(thr[:, :1] + 1)).astype(F32), (tq, LANES))
        upper = jnp.where(lax.broadcasted_iota(I32, (LANES, LANES), 0) <= lax.broadcasted_iota(I32, (LANES, LANES), 1),
                          1.0, 0.0).astype(BF16)

        def body(j, seen):
            for c in range(tk // LANES):
                kb = key_chunk(j, c)
                eq = kb == thr
                eq_f = jnp.where(eq, 1.0, 0.0)
                rank = seen + jnp.dot(eq_f.astype(BF16), upper, preferred_element_type=F32)
                drop = eq & (rank > need_f) & tie_rows
                key_ref[:, pl.ds(pl.multiple_of(j * tk + c * LANES, LANES), LANES)] = jnp.where(drop, thr - 1, kb)
                seen = seen + lane_total(eq_f)
            return seen

        lax.fori_loop(0, n_kv, body, jnp.zeros((tq, LANES), F32))

    thr_keep = jnp.where(small, NEG_INF_KEY + 1, jnp.maximum(thr, NEG_INF_KEY + 1))
    state = (s_ref, p_ref, m_ref, None, alpha_ref, acc_ref)
    _init_state(state)
    strip = min(tq, STRIP_ROWS)
    strips_per_head = tq // strip

    def attn_block(j, carry):
        ks = pl.multiple_of(j * tk, tk)
        for c in range(tk // LANES):
            bias_ref[:, c * LANES:(c + 1) * LANES] = jnp.where(key_chunk(j, c) >= thr_keep, 0.0, MASKED)
        cols =[slice(g * HEAD_DIM, (g + 1) * HEAD_DIM) for g in range(A_KV_HEADS)]
        _flash_step([q_ref[g * A_REP:(g + 1) * A_REP].reshape(A_REP * tq, HEAD_DIM) for g in range(A_KV_HEADS)],
                    [k_ref[pl.ds(ks, tk), c] for c in cols],
                    [_with_ones(v_ref[pl.ds(ks, tk), c]) for c in cols],
                    lambda c, r, ch: bias_ref[(r % strips_per_head) * strip:(r % strips_per_head + 1) * strip, ch],
                    state, rows=A_REP * tq, strip=strip)
        return carry

    lax.fori_loop(0, n_kv, attn_block, 0)
    for g in range(A_KV_HEADS):
        o = acc_ref[g, :, :HEAD_DIM] / acc_ref[g, :, HEAD_DIM:]
        for r in range(A_REP):
            hd = g * A_REP + r
            o_ref[:, hd * HEAD_DIM:(hd + 1) * HEAD_DIM] = o[r * tq:(r + 1) * tq].astype(BF16)


def _dsa_attention(cfg, q_hm, q_row0, qi, sm, k, v, kit, topk):
    tq, nq, s = cfg.tq, cfg.nq, cfg.s_pad
    qb0 = q_row0 // tq
    kern = functools.partial(_dsa_kernel, cfg=cfg, topk=topk)
    return pl.pallas_call(
        kern, grid=(cfg.batch, nq), name="dsa_attention",
        in_specs=[pl.BlockSpec((A_HEADS, tq, HEAD_DIM), lambda b, i: (0, qb0 + b * nq + i, 0)),
                  pl.BlockSpec((tq, IDX_HEADS * IDX_DIM), lambda b, i: (qb0 + b * nq + i, 0)),
                  pl.BlockSpec((tq, LANES), lambda b, i: (qb0 + b * nq + i, 0)),
                  pl.BlockSpec((s, A_KV_HEADS * HEAD_DIM), lambda b, i: (b, 0), pipeline_mode=pl.Buffered(1)),
                  pl.BlockSpec((s, A_KV_HEADS * HEAD_DIM), lambda b, i: (b, 0), pipeline_mode=pl.Buffered(1)),
                  pl.BlockSpec((None, 2, LANES, s), lambda b, i: (b, 0, 0, 0), pipeline_mode=pl.Buffered(1))],
        out_specs=pl.BlockSpec((tq, MIX_WIDTH), lambda b, i: (b * nq + i, 0)),
        out_shape=jax.ShapeDtypeStruct((cfg.batch * cfg.t_q, MIX_WIDTH), BF16),
        scratch_shapes=[pltpu.VMEM((tq, s), I32), pltpu.VMEM((tq, cfg.tk), F32), pltpu.VMEM((2, tq, LANES), I32),
                        pltpu.VMEM((3, tq, LANES), I32)]
        + _flash_scratch(A_KV_HEADS, A_REP * tq, cfg.tk, 2 * HEAD_DIM, with_l=False),
        compiler_params=_params(("parallel", "arbitrary")))(q_hm, qi, sm, k, v, kit)


B_STEP_HEADS = 2


def _run_blocks(n_full, n_kv, step):
    def plain(j, carry):
        step(j, False)
        return carry

    def masked(j, carry):
        step(j, True)
        return carry

    lax.fori_loop(0, n_full, plain, 0)
    lax.fori_loop(n_full, n_kv, masked, 0)


def _diff_kernel(q_ref, k_ref, v_ref, lq1_ref, lk1_ref, lq2_ref, lk2_ref, sub_ref, o_ref,
                 bias_ref, s_ref, p_ref, m_ref, l_ref, alpha_ref, acc_ref, *, cfg, lam_init):
    tq, tk = cfg.tq, cfg.tk
    qend, n_full, n_kv = _chunk_limits(pl.program_id(2), cfg)
    state = (s_ref, p_ref, m_ref, l_ref, alpha_ref, acc_ref)
    _init_state(state)
    strip = min(tq, STRIP_ROWS)
    n_chains = 2 * B_STEP_HEADS

    def step(j, masked):
        ks = pl.multiple_of(j * tk, tk)
        if masked:
            bias_ref[...] = _visibility_bias(ks, qend, (tq, tk))
        qk_cols = [slice(c * HEAD_DIM, (c + 1) * HEAD_DIM) for c in range(n_chains)]
        v_cols = [slice((c // 2) * B_V_DIM, (c // 2 + 1) * B_V_DIM) for c in range(n_chains)]
        _flash_step([q_ref[:, c] for c in qk_cols],
                    [k_ref[pl.ds(ks, tk), c] for c in qk_cols],
                    [v_ref[pl.ds(ks, tk), c] for c in v_cols],
                    (lambda c, r, ch: bias_ref[r * strip:(r + 1) * strip, ch]) if masked
                    else (lambda c, r, ch: None),
                    state, rows=tq, strip=strip)

    _run_blocks(n_full, n_kv, step)
    lam = (jnp.exp(jnp.sum(lq1_ref[...] * lk1_ref[...], axis=-1, keepdims=True))
           - jnp.exp(jnp.sum(lq2_ref[...] * lk2_ref[...], axis=-1, keepdims=True)) + lam_init)
    def normalised(c):
        return acc_ref[c] / jnp.sum(l_ref[c], axis=-1, keepdims=True)

    for h in range(B_STEP_HEADS):
        o = normalised(2 * h) - lam * normalised(2 * h + 1)
        ms = jnp.mean(o * o, axis=-1, keepdims=True)
        o_ref[:, h * B_V_DIM:(h + 1) * B_V_DIM] = (
            o * lax.rsqrt(ms + EPS) * sub_ref[...] * (1.0 - lam_init)).astype(BF16)


def _diff_attention(cfg, q, q_row0, k, v, lam_vecs, subln, lam_init):
    tq, nq, s = cfg.tq, cfg.nq, cfg.s_pad
    qb0 = q_row0 // tq
    width = B_STEP_HEADS * B_V_DIM
    kern = functools.partial(_diff_kernel, cfg=cfg, lam_init=lam_init)
    vec = pl.BlockSpec((1, HEAD_DIM), lambda b, h, i: (0, 0))
    return pl.pallas_call(
        kern, grid=(cfg.batch, B_HEADS // B_STEP_HEADS, nq), name="diff_attention",
        in_specs=[pl.BlockSpec((tq, width), lambda b, h, i: (qb0 + b * nq + i, h)),
                  pl.BlockSpec((s, width), lambda b, h, i: (b, h)),
                  pl.BlockSpec((s, width), lambda b, h, i: (b, h)),
                  vec, vec, vec, vec,
                  pl.BlockSpec((1, B_V_DIM), lambda b, h, i: (0, 0))],
        out_specs=pl.BlockSpec((tq, width), lambda b, h, i: (b * nq + i, h)),
        out_shape=jax.ShapeDtypeStruct((cfg.batch * cfg.t_q, MIX_WIDTH), BF16),
        scratch_shapes=[pltpu.VMEM((tq, cfg.tk), F32)]
        + _flash_scratch(2 * B_STEP_HEADS, tq, cfg.tk, B_V_DIM, with_l=True),
        compiler_params=_params(("parallel", "parallel", "arbitrary")))(
            q, k, v, *[a.reshape(1, HEAD_DIM) for a in lam_vecs], subln.reshape(1, B_V_DIM))


C_STEP_HEADS = 4


def _fox_kernel(q_ref, k_ref, v_ref, cq_ref, ck_ref, o_ref,
                bias_ref, cqh_ref, s_ref, p_ref, m_ref, alpha_ref, acc_ref, *, cfg):
    tq, tk = cfg.tq, cfg.tk
    hg = pl.program_id(1)
    qend, n_full, n_kv = _causal_limits(pl.program_id(2), cfg)
    state = (s_ref, p_ref, m_ref, None, alpha_ref, acc_ref)
    _init_state(state)
    strip = min(tq, STRIP_ROWS)
    lane = lax.broadcasted_iota(I32, (tq, LANES), 1)
    for c in range(C_STEP_HEADS):
        head_lane = lane == hg * C_STEP_HEADS + c
        cqh_ref[c] = jnp.broadcast_to(
            jnp.sum(jnp.where(head_lane, cq_ref[...], 0.0), axis=-1, keepdims=True), (tq, LANES))

    def step(j, masked):
        ks = pl.multiple_of(j * tk, tk)
        if masked:
            bias_ref[...] = _visibility_bias(ks, qend, (tq, tk))

        def bias(c, r, ch):
            rs = slice(r * strip, (r + 1) * strip)
            b = cqh_ref[c, rs, :] - ck_ref[c:c + 1, pl.ds(pl.multiple_of(ks + ch.start, LANES), LANES)]
            return b + bias_ref[rs, ch] if masked else b

        cols = [slice(c * HEAD_DIM, (c + 1) * HEAD_DIM) for c in range(C_STEP_HEADS)]
        _flash_step([q_ref[:, c] for c in cols],
                    [k_ref[pl.ds(ks, tk), c] for c in cols],
                    [_with_ones(v_ref[pl.ds(ks, tk), c]) for c in cols],
                    bias, state, rows=tq, strip=strip)

    _run_blocks(n_full, n_kv, step)
    for c in range(C_STEP_HEADS):
        o_ref[:, c * HEAD_DIM:(c + 1) * HEAD_DIM] = (
            acc_ref[c, :, :HEAD_DIM] / acc_ref[c, :, HEAD_DIM:]).astype(BF16)


def _fox_attention(cfg, q, q_row0, k, v, cq, ck_rows):
    tq, nq, s = cfg.tq, cfg.nq, cfg.s_pad
    qb0 = q_row0 // tq
    width = C_STEP_HEADS * HEAD_DIM
    groups = C_HEADS // C_STEP_HEADS
    kern = functools.partial(_fox_kernel, cfg=cfg)
    return pl.pallas_call(
        kern, grid=(cfg.batch, groups, nq), name="fox_attention",
        in_specs=[pl.BlockSpec((tq, width), lambda b, h, i: (qb0 + b * nq + i, h)),
                  pl.BlockSpec((s, width), lambda b, h, i: (b, h)),
                  pl.BlockSpec((s, width), lambda b, h, i: (b, h)),
                  pl.BlockSpec((tq, LANES), lambda b, h, i: (qb0 + b * nq + i, 0)),
                  pl.BlockSpec((None, C_STEP_HEADS, s), lambda b, h, i: (b * groups + h, 0, 0))],
        out_specs=pl.BlockSpec((tq, width), lambda b, h, i: (b * nq + i, h)),
        out_shape=jax.ShapeDtypeStruct((cfg.batch * cfg.t_q, MIX_WIDTH), BF16),
        scratch_shapes=[pltpu.VMEM((tq, cfg.tk), F32), pltpu.VMEM((C_STEP_HEADS, tq, LANES), F32)]
        + _flash_scratch(C_STEP_HEADS, tq, cfg.tk, 2 * HEAD_DIM, with_l=False),
        compiler_params=_params(("parallel", "parallel", "arbitrary")))(q, k, v, cq, ck_rows)


def _cumsum_kernel(x_ref, o_ref, carry_ref, *, tb):
    @pl.when(pl.program_id(1) == 0)
    def _():
        carry_ref[...] = jnp.zeros(carry_ref.shape, F32)

    x = x_ref[...]
    hi = x.astype(BF16)
    r1 = x - hi.astype(F32)
    mid = r1.astype(BF16)
    low = (r1 - mid.astype(F32)).astype(BF16)
    tri = jnp.where(lax.broadcasted_iota(I32, (tb, tb), 0) >= lax.broadcasted_iota(I32, (tb, tb), 1),
                    1.0, 0.0).astype(BF16)
    c = (jnp.dot(tri, hi, preferred_element_type=F32) + jnp.dot(tri, mid, preferred_element_type=F32)
         + jnp.dot(tri, low, preferred_element_type=F32)) + carry_ref[...]
    o_ref[...] = c * LOG2E
    carry_ref[...] = c[tb - 1:tb, :]


def _cumsum_rows(x, batch, t, tb):
    nt = t // tb
    return pl.pallas_call(
        functools.partial(_cumsum_kernel, tb=tb), grid=(batch, nt), name="cumsum_rows",
        in_specs=[pl.BlockSpec((tb, LANES), lambda b, j: (b * nt + j, 0))],
        out_specs=pl.BlockSpec((tb, LANES), lambda b, j: (b * nt + j, 0)),
        out_shape=jax.ShapeDtypeStruct((batch * t, LANES), F32),
        scratch_shapes=[pltpu.VMEM((1, LANES), F32)],
        compiler_params=_params(("parallel", "arbitrary")))(x)


def _split_w(w_in, sizes):
    offs = np.cumsum((0,) + tuple(sizes))
    return [w_in[:, int(offs[i]):int(offs[i + 1])] for i in range(len(sizes))]


def _pad_rows(a, rows):
    return jnp.pad(a, ((0, 0), (0, rows - a.shape[1])) + ((0, 0),) * (a.ndim - 2))


def _sample_keys(cache, new, s_pad):
    b = cache.shape[0]
    full = jnp.concatenate([cache.reshape(b, cache.shape[1], -1), new.reshape(b, new.shape[1], -1)], axis=1)
    return _pad_rows(full, s_pad).astype(BF16).reshape(b * s_pad, -1)


def _key_transposed_pairs(ki, batch, s):
    kt = jnp.swapaxes(ki.reshape(batch, s, IDX_DIM), 1, 2).astype(BF16)
    z = jnp.zeros_like(kt)
    return jnp.stack([jnp.concatenate([kt, z], axis=1), jnp.concatenate([z, kt], axis=1)], axis=1)


def _layer_a(pr, sr, cfg_p, cfg_s, xp, xs, ck, cv, cki, norm, w_in, w_out, qn, kn, ikn):
    w_all = w_in.astype(BF16)
    wq, wk, wv, wqi = _Cols(w_all, 2048, 0), _Cols(w_all, 512, 4), _Cols(w_all, 512, 5), _Cols(w_all, 1024, 3)
    wsm = _Cols(w_all, LANES, sum(A_SIZES[:4]) // LANES)
    wg = _Cols(_split_w(w_all, A_SIZES)[6], MIX_WIDTH)
    wo = w_out

    def project(rows, x):
        (q,), (qi,), (sm,) = _project(rows, x, norm, [
            _seg_qk(rows, wq, qn, scale=HEAD_DIM ** -0.5 * LOG2E, use_rope=True, head_major=True, emit_f32=False),
            _seg_rope64(rows, wqi), _seg_idx_small(rows, wsm, ikn)], "proj_a_query")
        (k_f, k_b), (v_f, v_b), (sg,) = _project(rows, x, norm, [
            _seg_qk(rows, wk, kn, scale=1.0, use_rope=True, head_major=False, emit_f32=True, cache_layout=True),
            _seg_plain(rows, wv, HEAD_DIM, True), _seg_gate(rows, wg)], "proj_a_kv_gate")
        return q, k_f, k_b, v_f, v_b, qi, sm, sg

    q, k_f, k_b, v_f, v_b, qi, sm, sg = project(pr, xp)
    ki_f = sm[:, :IDX_DIM]
    o = _dsa_attention(cfg_p, q, 0, qi, sm, k_b, v_b,
                       _key_transposed_pairs(ki_f, pr.batch, pr.t_q), min(TOPK_MAX, pr.t_q // 4))
    yp = _out_proj(pr, xp, o, sg, wo)

    q2, k2_f, _, v2_f, _, qi2, sm2, sg2 = project(sr, xs)
    ki2_f = sm2[:, :IDX_DIM]
    b, s = sr.batch, sr.t_q
    k_all = _sample_keys(ck, k2_f.reshape(b, s, -1), cfg_s.s_pad)
    v_all = _sample_keys(cv, v2_f.reshape(b, s, -1), cfg_s.s_pad)
    ki_all = _pad_rows(jnp.concatenate([cki, ki2_f.reshape(b, s, IDX_DIM)], axis=1), cfg_s.s_pad)
    o2 = _dsa_attention(cfg_s, q2, 0, qi2, sm2, k_all, v_all,
                        _key_transposed_pairs(ki_all.reshape(-1, IDX_DIM), b, cfg_s.s_pad),
                        min(TOPK_MAX, cfg_s.s_valid // 4))
    ys = _out_proj(sr, xs, o2, sg2, wo)
    kv = (A_KV_HEADS, HEAD_DIM)
    state = (k_f.reshape(pr.batch, pr.t_q, *kv), v_f.reshape(pr.batch, pr.t_q, *kv),
             ki_f.reshape(pr.batch, pr.t_q, IDX_DIM),
             k2_f.reshape(b, s, *kv), v2_f.reshape(b, s, *kv), ki2_f.reshape(b, s, IDX_DIM))
    return yp, ys, state


def _layer_b(layer, pr, sr, cfg_p, cfg_s, xp, xs, ck, cv, norm, w_in, w_out, qn, kn,
             lq1, lk1, lq2, lk2, subln):
    lam_init = 0.8 - 0.6 * float(np.exp(-0.3 * layer))
    w_all = w_in.astype(BF16)
    wq, wk, wv, wg = [_Cols(w_all, 2048, j) for j in range(4)]
    wo = w_out

    def project(rows, x):
        (q,), (sg,) = _project(rows, x, norm, [
            _seg_qk(rows, wq, qn, scale=HEAD_DIM ** -0.5 * LOG2E, use_rope=True, head_major=False, emit_f32=False),
            _seg_gate(rows, wg)], "proj_b_query_gate")
        (k_f, k_b), (v_f, v_b) = _project(rows, x, norm, [
            _seg_qk(rows, wk, kn, scale=1.0, use_rope=True, head_major=False, emit_f32=True, cache_layout=True),
            _seg_plain(rows, wv, B_V_DIM, True)], "proj_b_kv")
        return q, k_f, k_b, v_f, v_b, sg

    lam_vecs = (lq1, lk1, lq2, lk2)
    q, k_f, k_b, v_f, v_b, sg = project(pr, xp)
    o = _diff_attention(cfg_p, q, 0, k_b, v_b, lam_vecs, subln, lam_init)
    yp = _out_proj(pr, xp, o, sg, wo)

    q2, k2_f, _, v2_f, _, sg2 = project(sr, xs)
    b, s = sr.batch, sr.t_q
    k_all = _sample_keys(ck, k2_f.reshape(b, s, -1), cfg_s.s_pad)
    v_all = _sample_keys(cv, v2_f.reshape(b, s, -1), cfg_s.s_pad)
    o2 = _diff_attention(cfg_s, q2, 0, k_all, v_all, lam_vecs, subln, lam_init)
    ys = _out_proj(sr, xs, o2, sg2, wo)
    state = (k_f.reshape(pr.batch, pr.t_q, 2 * B_HEADS, HEAD_DIM), v_f.reshape(pr.batch, pr.t_q, B_HEADS, B_V_DIM),
             k2_f.reshape(b, s, 2 * B_HEADS, HEAD_DIM), v2_f.reshape(b, s, B_HEADS, B_V_DIM))
    return yp, ys, state


def _head_rows(c, batch, s):
    rows = jnp.swapaxes(c.reshape(batch, s, LANES)[:, :, :C_HEADS], 1, 2)
    return rows.reshape(batch * C_HEADS // C_STEP_HEADS, C_STEP_HEADS, s)


def _layer_c(pr, sr, cfg_p, cfg_s, xp, xs, ck, cv, clogf, norm, w_in, w_out, qn, kn, fb):
    w_all = w_in.astype(BF16)
    wq, wk, wv = [_Cols(w_all, 2048, j) for j in range(3)]
    wf = _Cols(w_all, LANES, sum(C_SIZES[:3]) // LANES)
    wg = _Cols(_split_w(w_all, C_SIZES)[4], MIX_WIDTH)
    wo = w_out

    def project(rows, x):
        (q,), (logf,), (sg,) = _project(rows, x, norm, [
            _seg_qk(rows, wq, qn, scale=HEAD_DIM ** -0.5 * LOG2E, use_rope=False, head_major=False, emit_f32=False),
            _seg_logf(rows, wf, fb), _seg_gate(rows, wg)], "proj_c_query_gate")
        (k_f, k_b), (v_f, v_b) = _project(rows, x, norm, [
            _seg_qk(rows, wk, kn, scale=1.0, use_rope=False, head_major=False, emit_f32=True),
            _seg_plain(rows, wv, HEAD_DIM, False)], "proj_c_kv")
        return q, k_f, k_b, v_f, v_b, logf, sg

    q, k_f, k_b, v_f, v_b, logf, sg = project(pr, xp)
    c = _cumsum_rows(logf, pr.batch, pr.t_q, 512)
    o = _fox_attention(cfg_p, q, 0, k_b, v_b, c, _head_rows(c, pr.batch, pr.t_q))
    yp = _out_proj(pr, xp, o, sg, wo)

    q2, k2_f, _, v2_f, _, logf2, sg2 = project(sr, xs)
    b, s = sr.batch, sr.t_q
    k_all = _sample_keys(ck, k2_f.reshape(b, s, -1), cfg_s.s_pad)
    v_all = _sample_keys(cv, v2_f.reshape(b, s, -1), cfg_s.s_pad)
    logf_all = jnp.concatenate([jnp.pad(clogf.astype(F32), ((0, 0), (0, 0), (0, LANES - C_HEADS))),
                                logf2.reshape(b, s, LANES)], axis=1)
    c2 = _cumsum_rows(_pad_rows(logf_all, cfg_s.s_pad).reshape(b * cfg_s.s_pad, LANES), b, cfg_s.s_pad, LANES)
    cq2 = c2.reshape(b, cfg_s.s_pad, LANES)[:, PAST_LEN:PAST_LEN + s].reshape(b * s, LANES)
    o2 = _fox_attention(cfg_s, q2, 0, k_all, v_all, cq2, _head_rows(c2, b, cfg_s.s_pad))
    ys = _out_proj(sr, xs, o2, sg2, wo)
    hd = (C_HEADS, HEAD_DIM)
    state = (k_f.reshape(pr.batch, pr.t_q, *hd), v_f.reshape(pr.batch, pr.t_q, *hd),
             logf[:, :C_HEADS].reshape(pr.batch, pr.t_q, C_HEADS),
             k2_f.reshape(b, s, *hd), v2_f.reshape(b, s, *hd), logf2[:, :C_HEADS].reshape(b, s, C_HEADS))
    return yp, ys, state


def kernel(x_prompt, x_sample, cache_l0_k, cache_l0_v, cache_l0_kidx, cache_l1_k, cache_l1_v, cache_l2_k, cache_l2_v, cache_l2_logf, cache_l3_k, cache_l3_v, cache_l3_kidx, l0_norm, l0_w_in, l0_w_out, l0_q_norm, l0_k_norm, l0_idx_k_norm, l1_norm, l1_w_in, l1_w_out, l1_q_norm, l1_k_norm, l1_lambda_q1, l1_lambda_k1, l1_lambda_q2, l1_lambda_k2, l1_subln, l2_norm, l2_w_in, l2_w_out, l2_q_norm, l2_k_norm, l2_forget_bias, l3_norm, l3_w_in, l3_w_out, l3_q_norm, l3_k_norm, l3_idx_k_norm):
    bp, tp, d = x_prompt.shape
    bs, ts, _ = x_sample.shape
    s_valid = PAST_LEN + ts
    s_pad = -(-s_valid // LANES) * LANES
    pr = _Rows(bp, tp, 0, 512)
    sr = _Rows(bs, ts, PAST_LEN, bs * ts)
    a_p = _Attn(bp, tp, tp, tp, 0, 256, 512)
    a_s = _Attn(bs, ts, s_pad, s_valid, PAST_LEN, ts, s_pad)
    bc_p = _Attn(bp, tp, tp, tp, 0, 512, 512)
    bc_s = a_s

    xp = x_prompt.reshape(bp * tp, d)
    xs = x_sample.reshape(bs * ts, d)
    xp, xs, st0 = _layer_a(pr, sr, a_p, a_s, xp, xs, cache_l0_k, cache_l0_v, cache_l0_kidx,
                           l0_norm, l0_w_in, l0_w_out, l0_q_norm, l0_k_norm, l0_idx_k_norm)
    xp, xs, st1 = _layer_b(1, pr, sr, bc_p, bc_s, xp, xs, cache_l1_k, cache_l1_v,
                           l1_norm, l1_w_in, l1_w_out, l1_q_norm, l1_k_norm,
                           l1_lambda_q1, l1_lambda_k1, l1_lambda_q2, l1_lambda_k2, l1_subln)
    xp, xs, st2 = _layer_c(pr, sr, bc_p, bc_s, xp, xs, cache_l2_k, cache_l2_v, cache_l2_logf,
                           l2_norm, l2_w_in, l2_w_out, l2_q_norm, l2_k_norm, l2_forget_bias)
    xp, xs, st3 = _layer_a(pr, sr, a_p, a_s, xp, xs, cache_l3_k, cache_l3_v, cache_l3_kidx,
                           l3_norm, l3_w_in, l3_w_out, l3_q_norm, l3_k_norm, l3_idx_k_norm)
    return (xp.reshape(bp, tp, d), xs.reshape(bs, ts, d)) + st0 + st1 + st2 + st3
```

```python
import functools

import numpy as np
import jax
import jax.numpy as jnp
from jax import lax
from jax.experimental import pallas as pl
from jax.experimental.pallas import tpu as pltpu

F32 = jnp.float32
BF16 = jnp.bfloat16
I32 = jnp.int32

D_MODEL = 2048
PAST_LEN = 1024
CHUNK_SHIFT = 6
ROPE_THETA = 10000.0
EPS = 1e-6
HEAD_DIM = 128
A_HEADS = 16
A_KV_HEADS = 4
A_REP = A_HEADS // A_KV_HEADS
IDX_HEADS = 16
IDX_DIM = 64
TOPK_MAX = 256
B_HEADS = 8
B_V_DIM = 256
C_HEADS = 16
MIX_WIDTH = D_MODEL

A_SIZES = (2048, 512, 512, 1024, 64, 16, 2048)
B_SIZES = (2048, 2048, 2048, 2048)
C_SIZES = (2048, 2048, 2048, 16, 2048)

LANES = 128
MXU_N = 256
VMEM_LIMIT = 56 * 1024 * 1024
MASKED = -1e30
LOG2E = 1.4426950408889634
STRIP_ROWS = 32
BISECT_UNROLL = 3
SEARCH_ROWS = 32
INT_MIN = -2 ** 31
SCORE_ROWS = 128
NEG_INF_KEY = -2139095041


def _params(sem):
    return pltpu.CompilerParams(dimension_semantics=sem, vmem_limit_bytes=VMEM_LIMIT)


def _norm_rows(x_ref, g_ref):
    x = x_ref[...]
    ms = jnp.mean(x * x, axis=-1, keepdims=True)
    return (x * lax.rsqrt(ms + EPS) * g_ref[...]).astype(BF16)


def _rope128(y, cos, sin):
    return y * cos + pltpu.roll(y, 64, 1) * sin


def _rope64(y, cos, sin):
    lane = lax.broadcasted_iota(I32, y.shape, 1)
    rot = jnp.where((lane & 63) < 32, pltpu.roll(y, 96, 1), pltpu.roll(y, 32, 1))
    return y * cos + rot * sin


class _Cols:
    def __init__(self, array, n, block=0):
        assert array.shape[1] >= (block + 1) * n
        self.array, self.n, self.block = array, n, block


class _Rows:
    def __init__(self, batch, t_q, pos0, tm):
        self.batch, self.t_q, self.pos0, self.tm = batch, t_q, pos0, tm
        self.m = batch * t_q
        self.tab_blocks = max(t_q // tm, 1)

    def tab_spec(self):
        nb = self.tab_blocks
        return pl.BlockSpec((self.tm, LANES), lambda i: (i % nb, 0))

    def tables(self, d):
        period = max(self.t_q, self.tm)
        pos = self.pos0 + (jnp.arange(period) % self.t_q)
        half = d // 2
        inv = ROPE_THETA ** (-2.0 * jnp.arange(half, dtype=F32) / d)
        ang = pos.astype(F32)[:, None] * inv[None, :]
        cos, sin = jnp.cos(ang), jnp.sin(ang)
        reps = LANES // d
        cos_f = jnp.tile(jnp.concatenate([cos, cos], axis=-1), (1, reps))
        sin_f = jnp.tile(jnp.concatenate([-sin, sin], axis=-1), (1, reps))
        return cos_f, sin_f


def _row_spec(tm, n):
    return pl.BlockSpec((tm, n), lambda i: (i, 0))


def _vec_spec(n):
    return pl.BlockSpec((1, n), lambda i: (0, 0))


class _Seg:
    def __init__(self, emit, w, aux, aux_specs, out_shape, out_specs):
        self.emit, self.w = emit, w
        self.aux, self.aux_specs = list(aux), list(aux_specs)
        self.out_shape, self.out_specs = list(out_shape), list(out_specs)


def _emit_qk(h, w_ref, aux, outs, *, n_heads, scale, use_rope, head_major, emit_f32):
    cos_ref, sin_ref, hg_ref = aux
    hg = hg_ref[...]
    of_ref = outs[0] if emit_f32 else None
    ob_ref = outs[-1]
    for c in range(n_heads // 2):
        y = jnp.dot(h, w_ref[:, c * MXU_N:(c + 1) * MXU_N], preferred_element_type=F32)
        for j in range(2):
            hd = 2 * c + j
            yh = y[:, j * LANES:(j + 1) * LANES]
            ms = jnp.mean(yh * yh, axis=-1, keepdims=True)
            yr = yh * lax.rsqrt(ms + EPS) * hg
            if use_rope:
                yr = _rope128(yr, cos_ref[...], sin_ref[...])
            if emit_f32 and len(of_ref.shape) == 3:
                of_ref[:, hd, :] = yr
            elif emit_f32:
                of_ref[:, hd * LANES:(hd + 1) * LANES] = yr
            yb = (yr * scale).astype(BF16)
            if head_major:
                ob_ref[hd] = yb
            else:
                ob_ref[:, hd * LANES:(hd + 1) * LANES] = yb


def _emit_plain(h, w_ref, aux, outs, *, n_cols, head_dim):
    of_ref, ob_ref = outs
    per_chunk = MXU_N // head_dim
    for c in range(n_cols // MXU_N):
        sl = slice(c * MXU_N, (c + 1) * MXU_N)
        y = jnp.dot(h, w_ref[:, sl], preferred_element_type=F32)
        if len(of_ref.shape) == 3:
            for j in range(per_chunk):
                of_ref[:, c * per_chunk + j, :] = y[:, j * head_dim:(j + 1) * head_dim]
        else:
            of_ref[:, sl] = y
        ob_ref[:, sl] = y.astype(BF16)


def _emit_gate(h, w_ref, aux, outs, *, n_cols):
    for c in range(n_cols // MXU_N):
        sl = slice(c * MXU_N, (c + 1) * MXU_N)
        y = jnp.dot(h, w_ref[:, sl], preferred_element_type=F32)
        outs[0][:, sl] = (y * (1.0 / (1.0 + jnp.exp(-y)))).astype(BF16)


def _emit_rope64(h, w_ref, aux, outs, *, n_cols):
    cos, sin = aux[0][...], aux[1][...]
    for c in range(n_cols // MXU_N):
        y = jnp.dot(h, w_ref[:, c * MXU_N:(c + 1) * MXU_N], preferred_element_type=F32)
        for j in range(2):
            col = c * MXU_N + j * LANES
            outs[0][:, col:col + LANES] = _rope64(y[:, j * LANES:(j + 1) * LANES], cos, sin).astype(BF16)


def _emit_idx_small(h, w_ref, aux, outs):
    cos_ref, sin_ref, hg_ref = aux
    y = jnp.dot(h, w_ref[...], preferred_element_type=F32)
    is_key = lax.broadcasted_iota(I32, y.shape, 1) < IDX_DIM
    ms = jnp.sum(jnp.where(is_key, y * y, 0.0), axis=-1, keepdims=True) * (1.0 / IDX_DIM)
    kn = _rope64(y * lax.rsqrt(ms + EPS) * hg_ref[...], cos_ref[...], sin_ref[...])
    outs[0][...] = jnp.where(is_key, kn, y)


def _emit_logf(h, w_ref, aux, outs):
    z = jnp.dot(h, w_ref[...], preferred_element_type=F32) + aux[0][...]
    outs[0][...] = jnp.minimum(z, 0.0) - jnp.log1p(jnp.exp(-jnp.abs(z)))


def _proj_kernel(x_ref, g_ref, *refs, segs):
    h = _norm_rows(x_ref, g_ref)
    n_in = sum(1 + len(s.aux) for s in segs)
    ins, outs = refs[:n_in], refs[n_in:]
    i = o = 0
    for s in segs:
        s.emit(h, ins[i], ins[i + 1:i + 1 + len(s.aux)], outs[o:o + len(s.out_shape)])
        i += 1 + len(s.aux)
        o += len(s.out_shape)


def _project(rows, x, g, segs, name):
    m, d = x.shape
    tm = rows.tm
    in_specs = [pl.BlockSpec((tm, d), lambda i: (i, 0)), pl.BlockSpec((1, d), lambda i: (0, 0))]
    args = [x, g.reshape(1, d)]
    for s in segs:
        block = s.w.block
        in_specs.append(pl.BlockSpec((d, s.w.n), lambda i, block=block: (0, block), pipeline_mode=pl.Buffered(1)))
        in_specs += s.aux_specs
        args += [s.w.array] + s.aux
    out = pl.pallas_call(
        functools.partial(_proj_kernel, segs=segs), grid=(m // tm,), in_specs=in_specs, name=name,
        out_specs=[sp for s in segs for sp in s.out_specs],
        out_shape=[sh for s in segs for sh in s.out_shape],
        compiler_params=_params(("parallel",)))(*args)
    res, o = [], 0
    for s in segs:
        res.append(out[o:o + len(s.out_shape)])
        o += len(s.out_shape)
    return res


def _state_out(rows, n, head_dim, cache_layout):
    if cache_layout:
        return (jax.ShapeDtypeStruct((rows.m, n // head_dim, head_dim), F32),
                pl.BlockSpec((rows.tm, n // head_dim, head_dim), lambda i: (i, 0, 0)))
    return jax.ShapeDtypeStruct((rows.m, n), F32), _row_spec(rows.tm, n)


def _seg_qk(rows, w, head_gain, *, scale, use_rope, head_major, emit_f32, cache_layout=False):
    n = w.n
    n_heads = n // HEAD_DIM
    cos, sin = rows.tables(HEAD_DIM)
    out_shape, out_specs = [], []
    if emit_f32:
        shape, spec = _state_out(rows, n, HEAD_DIM, cache_layout)
        out_shape.append(shape)
        out_specs.append(spec)
    if head_major:
        out_shape.append(jax.ShapeDtypeStruct((n_heads, rows.m, HEAD_DIM), BF16))
        out_specs.append(pl.BlockSpec((n_heads, rows.tm, HEAD_DIM), lambda i: (0, i, 0)))
    else:
        out_shape.append(jax.ShapeDtypeStruct((rows.m, n), BF16))
        out_specs.append(_row_spec(rows.tm, n))
    emit = functools.partial(_emit_qk, n_heads=n_heads, scale=scale, use_rope=use_rope,
                             head_major=head_major, emit_f32=emit_f32)
    return _Seg(emit, w, (cos, sin, head_gain.reshape(1, HEAD_DIM)),
                (rows.tab_spec(), rows.tab_spec(), _vec_spec(HEAD_DIM)), out_shape, out_specs)


def _seg_plain(rows, w, head_dim, cache_layout):
    n = w.n
    shape, spec = _state_out(rows, n, head_dim, cache_layout)
    return _Seg(functools.partial(_emit_plain, n_cols=n, head_dim=head_dim), w, (), (),
                [shape, jax.ShapeDtypeStruct((rows.m, n), BF16)], [spec, _row_spec(rows.tm, n)])


def _seg_gate(rows, w):
    n = w.n
    return _Seg(functools.partial(_emit_gate, n_cols=n), w, (), (),
                [jax.ShapeDtypeStruct((rows.m, n), BF16)], [_row_spec(rows.tm, n)])


def _seg_rope64(rows, w):
    n = w.n
    cos, sin = rows.tables(IDX_DIM)
    return _Seg(functools.partial(_emit_rope64, n_cols=n), w, (cos, sin), (rows.tab_spec(), rows.tab_spec()),
                [jax.ShapeDtypeStruct((rows.m, n), BF16)], [_row_spec(rows.tm, n)])


def _seg_idx_small(rows, w, key_gain):
    cos, sin = rows.tables(IDX_DIM)
    gain = jnp.concatenate([key_gain, jnp.zeros((LANES - IDX_DIM,), F32)]).reshape(1, LANES)
    return _Seg(_emit_idx_small, w, (cos, sin, gain), (rows.tab_spec(), rows.tab_spec(), _vec_spec(LANES)),
                [jax.ShapeDtypeStruct((rows.m, LANES), F32)], [_row_spec(rows.tm, LANES)])


def _seg_logf(rows, w, fb):
    fbp = jnp.concatenate([fb, jnp.zeros((LANES - C_HEADS,), F32)]).reshape(1, LANES)
    return _Seg(_emit_logf, w, (fbp,), (_vec_spec(LANES),),
                [jax.ShapeDtypeStruct((rows.m, LANES), F32)], [_row_spec(rows.tm, LANES)])


def _out_kernel(x_ref, o_ref, sg_ref, w_ref, y_ref, *, n_cols):
    a = o_ref[...] * sg_ref[...]
    for c in range(n_cols // MXU_N):
        sl = slice(c * MXU_N, (c + 1) * MXU_N)
        y_ref[:, sl] = x_ref[:, sl] + jnp.dot(a, w_ref[:, sl].astype(BF16), preferred_element_type=F32)


def _out_proj(rows, x, o, sg, w):
    m, d = x.shape
    tm = rows.tm
    kern = functools.partial(_out_kernel, n_cols=d)
    return pl.pallas_call(
        kern, grid=(m // tm,), name="out_proj",
        in_specs=[_row_spec(tm, d), _row_spec(tm, d), _row_spec(tm, d),
                  pl.BlockSpec((d, d), lambda i: (0, 0), pipeline_mode=pl.Buffered(1))],
        out_specs=_row_spec(tm, d), out_shape=jax.ShapeDtypeStruct((m, d), F32),
        compiler_params=_params(("parallel",)))(x, o, sg, w)


class _Attn:
    def __init__(self, batch, t_q, s_pad, s_valid, q_off, tq, tk):
        self.batch, self.t_q, self.s_pad, self.s_valid = batch, t_q, s_pad, s_valid
        self.q_off, self.tq, self.tk = q_off, tq, tk
        self.nq = t_q // tq


def _chunk_limits(i, cfg):
    q0 = cfg.q_off + i * cfg.tq
    qpos = q0 + lax.broadcasted_iota(I32, (cfg.tq, 1), 0)
    qend = jnp.minimum(((qpos >> CHUNK_SHIFT) + 1) << CHUNK_SHIFT, cfg.s_valid)
    kmin = jnp.minimum(((q0 >> CHUNK_SHIFT) + 1) << CHUNK_SHIFT, cfg.s_valid)
    kend = jnp.minimum((((q0 + cfg.tq - 1) >> CHUNK_SHIFT) + 1) << CHUNK_SHIFT, cfg.s_valid)
    return qend, kmin // cfg.tk, (kend + cfg.tk - 1) // cfg.tk


def _causal_limits(i, cfg):
    q0 = cfg.q_off + i * cfg.tq
    qpos = q0 + lax.broadcasted_iota(I32, (cfg.tq, 1), 0)
    return qpos + 1, (q0 + 1) // cfg.tk, (q0 + cfg.tq + cfg.tk - 1) // cfg.tk


def _qk(q, k):
    return lax.dot_general(q, k, (((1,), (1,)), ((), ())), preferred_element_type=F32)


def _visibility_bias(ks, qend, shape):
    kpos = ks + lax.broadcasted_iota(I32, shape, 1)
    return jnp.where(kpos < qend, 0.0, MASKED)


def _flash_step(qs, ks, vs, bias_fn, state, *, rows, strip):
    s_ref, p_ref, m_ref, l_ref, alpha_ref, acc_ref = state
    n = len(qs)
    strips = [slice(r * strip, (r + 1) * strip) for r in range(rows // strip)]
    chunks = [slice(a, a + LANES) for a in range(0, s_ref.shape[-1], LANES)]
    for c in range(n):
        s_ref[c] = _qk(qs[c], ks[c])
    for c in range(n):
        for r, rs in enumerate(strips):
            mx = None
            for ch in chunks:
                s = s_ref[c, rs, ch]
                b = bias_fn(c, r, ch)
                if b is not None:
                    s = s + b
                    s_ref[c, rs, ch] = s
                mx = s if mx is None else jnp.maximum(mx, s)
            m_old = m_ref[c, rs, :]
            m_new = jnp.maximum(m_old, jnp.broadcast_to(jnp.max(mx, axis=-1, keepdims=True), mx.shape))
            alpha_ref[c, rs, :] = jnp.exp2(m_old - m_new)
            m_ref[c, rs, :] = m_new
        for rs in strips:
            m = m_ref[c, rs, :]
            psum = None
            for ch in chunks:
                p = jnp.exp2(s_ref[c, rs, ch] - m)
                if l_ref is not None:
                    psum = p if psum is None else psum + p
                p_ref[c, rs, ch] = p.astype(BF16)
            if l_ref is not None:
                l_ref[c, rs, :] = alpha_ref[c, rs, :] * l_ref[c, rs, :] + psum
    for c in range(n):
        alpha = jnp.concatenate([alpha_ref[c]] * (acc_ref.shape[-1] // LANES), axis=1)
        acc_ref[c] = alpha * acc_ref[c] + jnp.dot(p_ref[c], vs[c], preferred_element_type=F32)


def _init_state(state):
    _, _, m_ref, l_ref, _, acc_ref = state
    m_ref[...] = jnp.full(m_ref.shape, MASKED, F32)
    acc_ref[...] = jnp.zeros(acc_ref.shape, F32)
    if l_ref is not None:
        l_ref[...] = jnp.zeros(l_ref.shape, F32)


def _flash_scratch(chains, rows, tk, dv, with_l):
    stat = pltpu.VMEM((chains, rows, LANES), F32)
    return ([pltpu.VMEM((chains, rows, tk), F32), pltpu.VMEM((chains, rows, tk), BF16), stat]
            + ([stat] if with_l else []) + [stat, pltpu.VMEM((chains, rows, dv), F32)])


def _with_ones(v):
    return jnp.concatenate([v, jnp.ones(v.shape, v.dtype)], axis=1)


def _dsa_kernel(q_ref, qi_ref, sm_ref, k_ref, v_ref, kit_ref, o_ref,
                key_ref, bias_ref, stat_ref, look_ref, s_ref, p_ref, m_ref, alpha_ref, acc_ref, *, cfg, topk):
    tq, tk = cfg.tq, cfg.tk
    i = pl.program_id(1)
    qend, _, n_kv = _chunk_limits(i, cfg)
    sub = min(tq, SCORE_ROWS)
    stat_ref[0] = jnp.full((tq, LANES), INT_MIN, I32)
    stat_ref[1] = jnp.zeros((tq, LANES), I32)

    def score_block(j, carry):
        ks = pl.multiple_of(j * tk, tk)
        for r0 in range(0, tq, sub):
            rs = slice(r0, r0 + sub)
            w = sm_ref[rs, IDX_DIM:IDX_DIM + IDX_HEADS] * (IDX_DIM ** -0.5 * IDX_HEADS ** -0.5)
            acc = jnp.zeros((sub, tk), F32)
            for pair in range(IDX_HEADS // 2):
                qp = qi_ref[rs, pair * LANES:(pair + 1) * LANES]
                for e in range(2):
                    hd = 2 * pair + e
                    sc = jnp.dot(qp, kit_ref[e, :, pl.ds(ks, tk)], preferred_element_type=F32)
                    acc = acc + w[:, hd:hd + 1] * jnp.maximum(sc, 0.0)
            bits = lax.bitcast_convert_type(acc, I32)
            key = bits ^ ((bits >> 31) & 0x7FFFFFFF)
            key = jnp.where(acc == 0.0, 0, key)
            kpos = ks + lax.broadcasted_iota(I32, (sub, tk), 1)
            key = jnp.where(kpos < qend[rs], key, NEG_INF_KEY)
            key_ref[rs, pl.ds(ks, tk)] = key
            kmax, nfin = stat_ref[0, rs, :], stat_ref[1, rs, :]
            for c in range(tk // LANES):
                kc = key[:, c * LANES:(c + 1) * LANES]
                kmax = jnp.maximum(kmax, kc)
                nfin = nfin + jnp.where(kc > NEG_INF_KEY, 1, 0)
            stat_ref[0, rs, :], stat_ref[1, rs, :] = kmax, nfin
        return carry

    lax.fori_loop(0, n_kv, score_block, 0)

    def key_chunk(j, c):
        return key_ref[:, pl.ds(pl.multiple_of(j * tk + c * LANES, LANES), LANES)]

    def lane_total(x):
        return jnp.broadcast_to(jnp.sum(x, axis=-1, keepdims=True), x.shape)

    def count_ge(cand):
        def body(j, part):
            ks = pl.multiple_of(j * tk, tk)
            ge = jnp.where(key_ref[:, pl.ds(ks, tk)] >= cand, 1, 0)
            for c in range(tk // LANES):
                part = part + ge[:, c * LANES:(c + 1) * LANES]
            return part
        part = lax.fori_loop(0, n_kv, body, jnp.zeros((tq, LANES), I32))
        return jnp.sum(part, axis=-1, keepdims=True)

    n_fin = lane_total(stat_ref[1])
    small = n_fin <= topk
    look_ref[0] = jnp.full((tq, LANES), NEG_INF_KEY + 1, I32)
    look_ref[1] = jnp.broadcast_to(jnp.max(stat_ref[0], axis=-1, keepdims=True), (tq, LANES)) + 1
    look_ref[2] = n_fin

    def pending():
        lo, hi, cnt = look_ref[0], look_ref[1], look_ref[2]
        open_row = (cnt != topk) & (hi != lo + 1) & jnp.logical_not(small)
        return jnp.max(jnp.where(open_row, 1, 0))

    groups = [slice(r, r + min(tq, SEARCH_ROWS)) for r in range(0, tq, min(tq, SEARCH_ROWS))]

    def halve(n_blocks):
        state = [[look_ref[v, gs, :] for v in range(3)] for gs in groups]
        part = jnp.zeros((groups[0].stop, LANES), I32)
        for _ in range(BISECT_UNROLL):
            for g, gs in enumerate(groups):
                lo, hi, cnt = state[g]
                mid = (lo >> 1) + (hi >> 1) + (lo & hi & 1)
                part = part >> 31
                for a in range(0, n_blocks * tk, LANES):
                    part = part + jnp.where(key_ref[gs, a:a + LANES] >= mid, 1, 0)
                c = lane_total(part)
                up = c >= topk
                state[g] = [jnp.where(up, mid, lo), jnp.where(up, hi, mid), jnp.where(up, c, cnt)]
        for g, gs in enumerate(groups):
            for v in range(3):
                look_ref[v, gs, :] = state[g][v]

    def search_round(_):
        for n_blocks in range(1, cfg.s_pad // tk + 1):
            pl.when(n_kv == n_blocks)(functools.partial(halve, n_blocks))
        return pending()

    lax.while_loop(lambda flag: flag > 0, search_round, pending())
    thr, n_ge = look_ref[0], look_ref[2]
    tie_rows = (n_ge > topk) & jnp.logical_not(small)

    @pl.when(jnp.max(jnp.where(tie_rows, 1, 0)) > 0)
    def _drop_late_ties():
        need_f = jnp.broadcast_to((topk - count_ge(thr[:, :1] + 1)).astype(F32), (tq, LANES))
        upper = jnp.where(lax.broadcasted_iota(I32, (LANES, LANES), 0) <= lax.broadcasted_iota(I32, (LANES, LANES), 1),
                          1.0, 0.0).astype(BF16)

        def body(j, seen):
            for c in range(tk // LANES):
                kb = key_chunk(j, c)
                eq = kb == thr
                eq_f = jnp.where(eq, 1.0, 0.0)
                rank = seen + jnp.dot(eq_f.astype(BF16), upper, preferred_element_type=F32)
                drop = eq & (rank > need_f) & tie_rows
                key_ref[:, pl.ds(pl.multiple_of(j * tk + c * LANES, LANES), LANES)] = jnp.where(drop, thr - 1, kb)
                seen = seen + lane_total(eq_f)
            return seen

        lax.fori_loop(0, n_kv, body, jnp.zeros((tq, LANES), F32))

    thr_keep = jnp.where(small, NEG_INF_KEY + 1, jnp.maximum(thr, NEG_INF_KEY + 1))
    state = (s_ref, p_ref, m_ref, None, alpha_ref, acc_ref)
    _init_state(state)
    strip = min(tq, STRIP_ROWS)
    strips_per_head = tq // strip

    def attn_block(j, carry):
        ks = pl.multiple_of(j * tk, tk)
        for c in range(tk // LANES):
            bias_ref[:, c * LANES:(c + 1) * LANES] = jnp.where(key_chunk(j, c) >= thr_keep, 0.0, MASKED)
        cols =[slice(g * HEAD_DIM, (g + 1) * HEAD_DIM) for g in range(A_KV_HEADS)]
        _flash_step([q_ref[g * A_REP:(g + 1) * A_REP].reshape(A_REP * tq, HEAD_DIM) for g in range(A_KV_HEADS)],
                    [k_ref[pl.ds(ks, tk), c] for c in cols],
                    [_with_ones(v_ref[pl.ds(ks, tk), c]) for c in cols],
                    lambda c, r, ch: bias_ref[(r % strips_per_head) * strip:(r % strips_per_head + 1) * strip, ch],
                    state, rows=A_REP * tq, strip=strip)
        return carry

    lax.fori_loop(0, n_kv, attn_block, 0)
    for g in range(A_KV_HEADS):
        o = acc_ref[g, :, :HEAD_DIM] / acc_ref[g, :, HEAD_DIM:]
        for r in range(A_REP):
            hd = g * A_REP + r
            o_ref[:, hd * HEAD_DIM:(hd + 1) * HEAD_DIM] = o[r * tq:(r + 1) * tq].astype(BF16)


def _dsa_attention(cfg, q_hm, q_row0, qi, sm, k, v, kit, topk):
    tq, nq, s = cfg.tq, cfg.nq, cfg.s_pad
    qb0 = q_row0 // tq
    kern = functools.partial(_dsa_kernel, cfg=cfg, topk=topk)
    return pl.pallas_call(
        kern, grid=(cfg.batch, nq), name="dsa_attention",
        in_specs=[pl.BlockSpec((A_HEADS, tq, HEAD_DIM), lambda b, i: (0, qb0 + b * nq + i, 0)),
                  pl.BlockSpec((tq, IDX_HEADS * IDX_DIM), lambda b, i: (qb0 + b * nq + i, 0)),
                  pl.BlockSpec((tq, LANES), lambda b, i: (qb0 + b * nq + i, 0)),
                  pl.BlockSpec((s, A_KV_HEADS * HEAD_DIM), lambda b, i: (b, 0), pipeline_mode=pl.Buffered(1)),
                  pl.BlockSpec((s, A_KV_HEADS * HEAD_DIM), lambda b, i: (b, 0), pipeline_mode=pl.Buffered(1)),
                  pl.BlockSpec((None, 2, LANES, s), lambda b, i: (b, 0, 0, 0), pipeline_mode=pl.Buffered(1))],
        out_specs=pl.BlockSpec((tq, MIX_WIDTH), lambda b, i: (b * nq + i, 0)),
        out_shape=jax.ShapeDtypeStruct((cfg.batch * cfg.t_q, MIX_WIDTH), BF16),
        scratch_shapes=[pltpu.VMEM((tq, s), I32), pltpu.VMEM((tq, cfg.tk), F32), pltpu.VMEM((2, tq, LANES), I32),
                        pltpu.VMEM((3, tq, LANES), I32)]
        + _flash_scratch(A_KV_HEADS, A_REP * tq, cfg.tk, 2 * HEAD_DIM, with_l=False),
        compiler_params=_params(("parallel", "arbitrary")))(q_hm, qi, sm, k, v, kit)


B_STEP_HEADS = 2


def _run_blocks(n_full, n_kv, step):
    def plain(j, carry):
        step(j, False)
        return carry

    def masked(j, carry):
        step(j, True)
        return carry

    lax.fori_loop(0, n_full, plain, 0)
    lax.fori_loop(n_full, n_kv, masked, 0)


def _diff_kernel(q_ref, k_ref, v_ref, lq1_ref, lk1_ref, lq2_ref, lk2_ref, sub_ref, o_ref,
                 bias_ref, s_ref, p_ref, m_ref, l_ref, alpha_ref, acc_ref, *, cfg, lam_init):
    tq, tk = cfg.tq, cfg.tk
    qend, n_full, n_kv = _chunk_limits(pl.program_id(2), cfg)
    state = (s_ref, p_ref, m_ref, l_ref, alpha_ref, acc_ref)
    _init_state(state)
    strip = min(tq, STRIP_ROWS)
    n_chains = 2 * B_STEP_HEADS

    def step(j, masked):
        ks = pl.multiple_of(j * tk, tk)
        if masked:
            bias_ref[...] = _visibility_bias(ks, qend, (tq, tk))
        qk_cols = [slice(c * HEAD_DIM, (c + 1) * HEAD_DIM) for c in range(n_chains)]
        v_cols = [slice((c // 2) * B_V_DIM, (c // 2 + 1) * B_V_DIM) for c in range(n_chains)]
        _flash_step([q_ref[:, c] for c in qk_cols],
                    [k_ref[pl.ds(ks, tk), c] for c in qk_cols],
                    [v_ref[pl.ds(ks, tk), c] for c in v_cols],
                    (lambda c, r, ch: bias_ref[r * strip:(r + 1) * strip, ch]) if masked
                    else (lambda c, r, ch: None),
                    state, rows=tq, strip=strip)

    _run_blocks(n_full, n_kv, step)
    lam = (jnp.exp(jnp.sum(lq1_ref[...] * lk1_ref[...], axis=-1, keepdims=True))
           - jnp.exp(jnp.sum(lq2_ref[...] * lk2_ref[...], axis=-1, keepdims=True)) + lam_init)
    def normalised(c):
        return acc_ref[c] / jnp.sum(l_ref[c], axis=-1, keepdims=True)

    for h in range(B_STEP_HEADS):
        o = normalised(2 * h) - lam * normalised(2 * h + 1)
        ms = jnp.mean(o * o, axis=-1, keepdims=True)
        o_ref[:, h * B_V_DIM:(h + 1) * B_V_DIM] = (
            o * lax.rsqrt(ms + EPS) * sub_ref[...] * (1.0 - lam_init)).astype(BF16)


def _diff_attention(cfg, q, q_row0, k, v, lam_vecs, subln, lam_init):
    tq, nq, s = cfg.tq, cfg.nq, cfg.s_pad
    qb0 = q_row0 // tq
    width = B_STEP_HEADS * B_V_DIM
    kern = functools.partial(_diff_kernel, cfg=cfg, lam_init=lam_init)
    vec = pl.BlockSpec((1, HEAD_DIM), lambda b, h, i: (0, 0))
    return pl.pallas_call(
        kern, grid=(cfg.batch, B_HEADS // B_STEP_HEADS, nq), name="diff_attention",
        in_specs=[pl.BlockSpec((tq, width), lambda b, h, i: (qb0 + b * nq + i, h)),
                  pl.BlockSpec((s, width), lambda b, h, i: (b, h)),
                  pl.BlockSpec((s, width), lambda b, h, i: (b, h)),
                  vec, vec, vec, vec,
                  pl.BlockSpec((1, B_V_DIM), lambda b, h, i: (0, 0))],
        out_specs=pl.BlockSpec((tq, width), lambda b, h, i: (b * nq + i, h)),
        out_shape=jax.ShapeDtypeStruct((cfg.batch * cfg.t_q, MIX_WIDTH), BF16),
        scratch_shapes=[pltpu.VMEM((tq, cfg.tk), F32)]
        + _flash_scratch(2 * B_STEP_HEADS, tq, cfg.tk, B_V_DIM, with_l=True),
        compiler_params=_params(("parallel", "parallel", "arbitrary")))(
            q, k, v, *[a.reshape(1, HEAD_DIM) for a in lam_vecs], subln.reshape(1, B_V_DIM))


C_STEP_HEADS = 4


def _fox_kernel(q_ref, k_ref, v_ref, cq_ref, ck_ref, o_ref,
                bias_ref, cqh_ref, s_ref, p_ref, m_ref, alpha_ref, acc_ref, *, cfg):
    tq, tk = cfg.tq, cfg.tk
    hg = pl.program_id(1)
    qend, n_full, n_kv = _causal_limits(pl.program_id(2), cfg)
    state = (s_ref, p_ref, m_ref, None, alpha_ref, acc_ref)
    _init_state(state)
    strip = min(tq, STRIP_ROWS)
    lane = lax.broadcasted_iota(I32, (tq, LANES), 1)
    for c in range(C_STEP_HEADS):
        head_lane = lane == hg * C_STEP_HEADS + c
        cqh_ref[c] = jnp.broadcast_to(
            jnp.sum(jnp.where(head_lane, cq_ref[...], 0.0), axis=-1, keepdims=True), (tq, LANES))

    def step(j, masked):
        ks = pl.multiple_of(j * tk, tk)
        if masked:
            bias_ref[...] = _visibility_bias(ks, qend, (tq, tk))

        def bias(c, r, ch):
            rs = slice(r * strip, (r + 1) * strip)
            b = cqh_ref[c, rs, :] - ck_ref[c:c + 1, pl.ds(pl.multiple_of(ks + ch.start, LANES), LANES)]
            return b + bias_ref[rs, ch] if masked else b

        cols = [slice(c * HEAD_DIM, (c + 1) * HEAD_DIM) for c in range(C_STEP_HEADS)]
        _flash_step([q_ref[:, c] for c in cols],
                    [k_ref[pl.ds(ks, tk), c] for c in cols],
                    [_with_ones(v_ref[pl.ds(ks, tk), c]) for c in cols],
                    bias, state, rows=tq, strip=strip)

    _run_blocks(n_full, n_kv, step)
    for c in range(C_STEP_HEADS):
        o_ref[:, c * HEAD_DIM:(c + 1) * HEAD_DIM] = (
            acc_ref[c, :, :HEAD_DIM] / acc_ref[c, :, HEAD_DIM:]).astype(BF16)


def _fox_attention(cfg, q, q_row0, k, v, cq, ck_rows):
    tq, nq, s = cfg.tq, cfg.nq, cfg.s_pad
    qb0 = q_row0 // tq
    width = C_STEP_HEADS * HEAD_DIM
    groups = C_HEADS // C_STEP_HEADS
    kern = functools.partial(_fox_kernel, cfg=cfg)
    return pl.pallas_call(
        kern, grid=(cfg.batch, groups, nq), name="fox_attention",
        in_specs=[pl.BlockSpec((tq, width), lambda b, h, i: (qb0 + b * nq + i, h)),
                  pl.BlockSpec((s, width), lambda b, h, i: (b, h)),
                  pl.BlockSpec((s, width), lambda b, h, i: (b, h)),
                  pl.BlockSpec((tq, LANES), lambda b, h, i: (qb0 + b * nq + i, 0)),
                  pl.BlockSpec((None, C_STEP_HEADS, s), lambda b, h, i: (b * groups + h, 0, 0))],
        out_specs=pl.BlockSpec((tq, width), lambda b, h, i: (b * nq + i, h)),
        out_shape=jax.ShapeDtypeStruct((cfg.batch * cfg.t_q, MIX_WIDTH), BF16),
        scratch_shapes=[pltpu.VMEM((tq, cfg.tk), F32), pltpu.VMEM((C_STEP_HEADS, tq, LANES), F32)]
        + _flash_scratch(C_STEP_HEADS, tq, cfg.tk, 2 * HEAD_DIM, with_l=False),
        compiler_params=_params(("parallel", "parallel", "arbitrary")))(q, k, v, cq, ck_rows)


def _cumsum_kernel(x_ref, o_ref, carry_ref, *, tb):
    @pl.when(pl.program_id(1) == 0)
    def _():
        carry_ref[...] = jnp.zeros(carry_ref.shape, F32)

    x = x_ref[...]
    hi = x.astype(BF16)
    r1 = x - hi.astype(F32)
    mid = r1.astype(BF16)
    low = (r1 - mid.astype(F32)).astype(BF16)
    tri = jnp.where(lax.broadcasted_iota(I32, (tb, tb), 0) >= lax.broadcasted_iota(I32, (tb, tb), 1),
                    1.0, 0.0).astype(BF16)
    c = (jnp.dot(tri, hi, preferred_element_type=F32) + jnp.dot(tri, mid, preferred_element_type=F32)
         + jnp.dot(tri, low, preferred_element_type=F32)) + carry_ref[...]
    o_ref[...] = c * LOG2E
    carry_ref[...] = c[tb - 1:tb, :]


def _cumsum_rows(x, batch, t, tb):
    nt = t // tb
    return pl.pallas_call(
        functools.partial(_cumsum_kernel, tb=tb), grid=(batch, nt), name="cumsum_rows",
        in_specs=[pl.BlockSpec((tb, LANES), lambda b, j: (b * nt + j, 0))],
        out_specs=pl.BlockSpec((tb, LANES), lambda b, j: (b * nt + j, 0)),
        out_shape=jax.ShapeDtypeStruct((batch * t, LANES), F32),
        scratch_shapes=[pltpu.VMEM((1, LANES), F32)],
        compiler_params=_params(("parallel", "arbitrary")))(x)


def _split_w(w_in, sizes):
    offs = np.cumsum((0,) + tuple(sizes))
    return [w_in[:, int(offs[i]):int(offs[i + 1])] for i in range(len(sizes))]


def _pad_rows(a, rows):
    return jnp.pad(a, ((0, 0), (0, rows - a.shape[1])) + ((0, 0),) * (a.ndim - 2))


def _sample_keys_kernel(cache_ref, new_ref, o_ref, *, heads, hd, past, s_new):
    for h in range(heads):
        o_ref[0:past, h * hd:(h + 1) * hd] = cache_ref[pl.ds(h, past, stride=heads), :].astype(BF16)
    o_ref[past:past + s_new, :] = new_ref[...].astype(BF16)
    o_ref[past + s_new:, :] = jnp.zeros((o_ref.shape[0] - past - s_new, o_ref.shape[1]), BF16)


def _sample_keys(cache, new, s_pad):
    b, past, heads, hd = cache.shape
    n = heads * hd
    s_new = new.shape[1]
    if heads % 8 or hd != LANES:
        full = jnp.concatenate([cache.reshape(b, past, n), new.reshape(b, s_new, n)], axis=1)
        return _pad_rows(full, s_pad).astype(BF16).reshape(b * s_pad, n)
    kern = functools.partial(_sample_keys_kernel, heads=heads, hd=hd, past=past, s_new=s_new)
    return pl.pallas_call(
        kern, grid=(b,), name="sample_keys",
        in_specs=[pl.BlockSpec((None, past * heads, hd), lambda i: (i, 0, 0)),
                  pl.BlockSpec((s_new, n), lambda i: (i, 0))],
        out_specs=pl.BlockSpec((s_pad, n), lambda i: (i, 0)),
        out_shape=jax.ShapeDtypeStruct((b * s_pad, n), BF16),
        compiler_params=_params(("parallel",)))(cache.reshape(b, past * heads, hd), new.reshape(b * s_new, n))


def _key_transposed_pairs(ki, batch, s):
    kt = jnp.swapaxes(ki.reshape(batch, s, IDX_DIM), 1, 2).astype(BF16)
    z = jnp.zeros_like(kt)
    return jnp.stack([jnp.concatenate([kt, z], axis=1), jnp.concatenate([z, kt], axis=1)], axis=1)


def _layer_a(pr, sr, cfg_p, cfg_s, xp, xs, ck, cv, cki, norm, w_in, w_out, qn, kn, ikn):
    w_all = w_in.astype(BF16)
    wq, wk, wv, wqi = _Cols(w_all, 2048, 0), _Cols(w_all, 512, 4), _Cols(w_all, 512, 5), _Cols(w_all, 1024, 3)
    wsm = _Cols(w_all, LANES, sum(A_SIZES[:4]) // LANES)
    wg = _Cols(_split_w(w_all, A_SIZES)[6], MIX_WIDTH)
    wo = w_out

    def project(rows, x):
        (q,), (qi,), (sm,) = _project(rows, x, norm, [
            _seg_qk(rows, wq, qn, scale=HEAD_DIM ** -0.5 * LOG2E, use_rope=True, head_major=True, emit_f32=False),
            _seg_rope64(rows, wqi), _seg_idx_small(rows, wsm, ikn)], "proj_a_query")
        (k_f, k_b), (v_f, v_b), (sg,) = _project(rows, x, norm, [
            _seg_qk(rows, wk, kn, scale=1.0, use_rope=True, head_major=False, emit_f32=True, cache_layout=True),
            _seg_plain(rows, wv, HEAD_DIM, True), _seg_gate(rows, wg)], "proj_a_kv_gate")
        return q, k_f, k_b, v_f, v_b, qi, sm, sg

    q, k_f, k_b, v_f, v_b, qi, sm, sg = project(pr, xp)
    ki_f = sm[:, :IDX_DIM]
    o = _dsa_attention(cfg_p, q, 0, qi, sm, k_b, v_b,
                       _key_transposed_pairs(ki_f, pr.batch, pr.t_q), min(TOPK_MAX, pr.t_q // 4))
    yp = _out_proj(pr, xp, o, sg, wo)

    q2, k2_f, _, v2_f, _, qi2, sm2, sg2 = project(sr, xs)
    ki2_f = sm2[:, :IDX_DIM]
    b, s = sr.batch, sr.t_q
    k_all = _sample_keys(ck, k2_f.reshape(b, s, -1), cfg_s.s_pad)
    v_all = _sample_keys(cv, v2_f.reshape(b, s, -1), cfg_s.s_pad)
    ki_all = _pad_rows(jnp.concatenate([cki, ki2_f.reshape(b, s, IDX_DIM)], axis=1), cfg_s.s_pad)
    o2 = _dsa_attention(cfg_s, q2, 0, qi2, sm2, k_all, v_all,
                        _key_transposed_pairs(ki_all.reshape(-1, IDX_DIM), b, cfg_s.s_pad),
                        min(TOPK_MAX, cfg_s.s_valid // 4))
    ys = _out_proj(sr, xs, o2, sg2, wo)
    kv = (A_KV_HEADS, HEAD_DIM)
    state = (k_f.reshape(pr.batch, pr.t_q, *kv), v_f.reshape(pr.batch, pr.t_q, *kv),
             ki_f.reshape(pr.batch, pr.t_q, IDX_DIM),
             k2_f.reshape(b, s, *kv), v2_f.reshape(b, s, *kv), ki2_f.reshape(b, s, IDX_DIM))
    return yp, ys, state


def _layer_b(layer, pr, sr, cfg_p, cfg_s, xp, xs, ck, cv, norm, w_in, w_out, qn, kn,
             lq1, lk1, lq2, lk2, subln):
    lam_init = 0.8 - 0.6 * float(np.exp(-0.3 * layer))
    w_all = w_in.astype(BF16)
    wq, wk, wv, wg = [_Cols(w_all, 2048, j) for j in range(4)]
    wo = w_out

    def project(rows, x):
        (q,), (sg,) = _project(rows, x, norm, [
            _seg_qk(rows, wq, qn, scale=HEAD_DIM ** -0.5 * LOG2E, use_rope=True, head_major=False, emit_f32=False),
            _seg_gate(rows, wg)], "proj_b_query_gate")
        (k_f, k_b), (v_f, v_b) = _project(rows, x, norm, [
            _seg_qk(rows, wk, kn, scale=1.0, use_rope=True, head_major=False, emit_f32=True, cache_layout=True),
            _seg_plain(rows, wv, B_V_DIM, True)], "proj_b_kv")
        return q, k_f, k_b, v_f, v_b, sg

    lam_vecs = (lq1, lk1, lq2, lk2)
    q, k_f, k_b, v_f, v_b, sg = project(pr, xp)
    o = _diff_attention(cfg_p, q, 0, k_b, v_b, lam_vecs, subln, lam_init)
    yp = _out_proj(pr, xp, o, sg, wo)

    q2, k2_f, _, v2_f, _, sg2 = project(sr, xs)
    b, s = sr.batch, sr.t_q
    k_all = _sample_keys(ck, k2_f.reshape(b, s, -1), cfg_s.s_pad)
    v_all = _sample_keys(cv, v2_f.reshape(b, s, -1), cfg_s.s_pad)
    o2 = _diff_attention(cfg_s, q2, 0, k_all, v_all, lam_vecs, subln, lam_init)
    ys = _out_proj(sr, xs, o2, sg2, wo)
    state = (k_f.reshape(pr.batch, pr.t_q, 2 * B_HEADS, HEAD_DIM), v_f.reshape(pr.batch, pr.t_q, B_HEADS, B_V_DIM),
             k2_f.reshape(b, s, 2 * B_HEADS, HEAD_DIM), v2_f.reshape(b, s, B_HEADS, B_V_DIM))
    return yp, ys, state


def _head_rows(c, batch, s):
    rows = jnp.swapaxes(c.reshape(batch, s, LANES)[:, :, :C_HEADS], 1, 2)
    return rows.reshape(batch * C_HEADS // C_STEP_HEADS, C_STEP_HEADS, s)


def _layer_c(pr, sr, cfg_p, cfg_s, xp, xs, ck, cv, clogf, norm, w_in, w_out, qn, kn, fb):
    w_all = w_in.astype(BF16)
    wq, wk, wv = [_Cols(w_all, 2048, j) for j in range(3)]
    wf = _Cols(w_all, LANES, sum(C_SIZES[:3]) // LANES)
    wg = _Cols(_split_w(w_all, C_SIZES)[4], MIX_WIDTH)
    wo = w_out

    def project(rows, x):
        (q,), (logf,), (sg,) = _project(rows, x, norm, [
            _seg_qk(rows, wq, qn, scale=HEAD_DIM ** -0.5 * LOG2E, use_rope=False, head_major=False, emit_f32=False),
            _seg_logf(rows, wf, fb), _seg_gate(rows, wg)], "proj_c_query_gate")
        (k_f, k_b), (v_f, v_b) = _project(rows, x, norm, [
            _seg_qk(rows, wk, kn, scale=1.0, use_rope=False, head_major=False, emit_f32=True),
            _seg_plain(rows, wv, HEAD_DIM, False)], "proj_c_kv")
        return q, k_f, k_b, v_f, v_b, logf, sg

    q, k_f, k_b, v_f, v_b, logf, sg = project(pr, xp)
    c = _cumsum_rows(logf, pr.batch, pr.t_q, 512)
    o = _fox_attention(cfg_p, q, 0, k_b, v_b, c, _head_rows(c, pr.batch, pr.t_q))
    yp = _out_proj(pr, xp, o, sg, wo)

    q2, k2_f, _, v2_f, _, logf2, sg2 = project(sr, xs)
    b, s = sr.batch, sr.t_q
    k_all = _sample_keys(ck, k2_f.reshape(b, s, -1), cfg_s.s_pad)
    v_all = _sample_keys(cv, v2_f.reshape(b, s, -1), cfg_s.s_pad)
    logf_all = jnp.concatenate([jnp.pad(clogf.astype(F32), ((0, 0), (0, 0), (0, LANES - C_HEADS))),
                                logf2.reshape(b, s, LANES)], axis=1)
    c2 = _cumsum_rows(_pad_rows(logf_all, cfg_s.s_pad).reshape(b * cfg_s.s_pad, LANES), b, cfg_s.s_pad, LANES)
    cq2 = c2.reshape(b, cfg_s.s_pad, LANES)[:, PAST_LEN:PAST_LEN + s].reshape(b * s, LANES)
    o2 = _fox_attention(cfg_s, q2, 0, k_all, v_all, cq2, _head_rows(c2, b, cfg_s.s_pad))
    ys = _out_proj(sr, xs, o2, sg2, wo)
    hd = (C_HEADS, HEAD_DIM)
    state = (k_f.reshape(pr.batch, pr.t_q, *hd), v_f.reshape(pr.batch, pr.t_q, *hd),
             logf[:, :C_HEADS].reshape(pr.batch, pr.t_q, C_HEADS),
             k2_f.reshape(b, s, *hd), v2_f.reshape(b, s, *hd), logf2[:, :C_HEADS].reshape(b, s, C_HEADS))
    return yp, ys, state


def kernel(x_prompt, x_sample, cache_l0_k, cache_l0_v, cache_l0_kidx, cache_l1_k, cache_l1_v, cache_l2_k, cache_l2_v, cache_l2_logf, cache_l3_k, cache_l3_v, cache_l3_kidx, l0_norm, l0_w_in, l0_w_out, l0_q_norm, l0_k_norm, l0_idx_k_norm, l1_norm, l1_w_in, l1_w_out, l1_q_norm, l1_k_norm, l1_lambda_q1, l1_lambda_k1, l1_lambda_q2, l1_lambda_k2, l1_subln, l2_norm, l2_w_in, l2_w_out, l2_q_norm, l2_k_norm, l2_forget_bias, l3_norm, l3_w_in, l3_w_out, l3_q_norm, l3_k_norm, l3_idx_k_norm):
    bp, tp, d = x_prompt.shape
    bs, ts, _ = x_sample.shape
    s_valid = PAST_LEN + ts
    s_pad = -(-s_valid // LANES) * LANES
    pr = _Rows(bp, tp, 0, 512)
    sr = _Rows(bs, ts, PAST_LEN, bs * ts)
    a_p = _Attn(bp, tp, tp, tp, 0, 256, 512)
    a_s = _Attn(bs, ts, s_pad, s_valid, PAST_LEN, ts, s_pad)
    bc_p = _Attn(bp, tp, tp, tp, 0, 512, 512)
    bc_s = a_s

    xp = x_prompt.reshape(bp * tp, d)
    xs = x_sample.reshape(bs * ts, d)
    xp, xs, st0 = _layer_a(pr, sr, a_p, a_s, xp, xs, cache_l0_k, cache_l0_v, cache_l0_kidx,
                           l0_norm, l0_w_in, l0_w_out, l0_q_norm, l0_k_norm, l0_idx_k_norm)
    xp, xs, st1 = _layer_b(1, pr, sr, bc_p, bc_s, xp, xs, cache_l1_k, cache_l1_v,
                           l1_norm, l1_w_in, l1_w_out, l1_q_norm, l1_k_norm,
                           l1_lambda_q1, l1_lambda_k1, l1_lambda_q2, l1_lambda_k2, l1_subln)
    xp, xs, st2 = _layer_c(pr, sr, bc_p, bc_s, xp, xs, cache_l2_k, cache_l2_v, cache_l2_logf,
                           l2_norm, l2_w_in, l2_w_out, l2_q_norm, l2_k_norm, l2_forget_bias)
    xp, xs, st3 = _layer_a(pr, sr, a_p, a_s, xp, xs, cache_l3_k, cache_l3_v, cache_l3_kidx,
                           l3_norm, l3_w_in, l3_w_out, l3_q_norm, l3_k_norm, l3_idx_k_norm)
    return (xp.reshape(bp, tp, d), xs.reshape(bs, ts, d)) + st0 + st1 + st2 + st3
```

```python
import functools

import numpy as np
import jax
import jax.numpy as jnp
from jax import lax
from jax.experimental import pallas as pl
from jax.experimental.pallas import tpu as pltpu

F32 = jnp.float32
BF16 = jnp.bfloat16
I32 = jnp.int32

D_MODEL = 2048
PAST_LEN = 1024
CHUNK_SHIFT = 6
ROPE_THETA = 10000.0
EPS = 1e-6
HEAD_DIM = 128
A_HEADS = 16
A_KV_HEADS = 4
A_REP = A_HEADS // A_KV_HEADS
IDX_HEADS = 16
IDX_DIM = 64
TOPK_MAX = 256
B_HEADS = 8
B_V_DIM = 256
C_HEADS = 16
MIX_WIDTH = D_MODEL

A_SIZES = (2048, 512, 512, 1024, 64, 16, 2048)
B_SIZES = (2048, 2048, 2048, 2048)
C_SIZES = (2048, 2048, 2048, 16, 2048)

LANES = 128
MXU_N = 256
VMEM_LIMIT = 56 * 1024 * 1024
MASKED = -1e30
NARROW_RANGE = 256.0
LOG2E = 1.4426950408889634
STRIP_ROWS = 32
BISECT_UNROLL = 3
SEARCH_ROWS = 32
INT_MIN = -2 ** 31
SCORE_ROWS = 128
NEG_INF_KEY = -2139095041


def _params(sem):
    return pltpu.CompilerParams(dimension_semantics=sem, vmem_limit_bytes=VMEM_LIMIT)


def _norm_rows(x_ref, g_ref):
    x = x_ref[...]
    ms = jnp.mean(x * x, axis=-1, keepdims=True)
    return (x * lax.rsqrt(ms + EPS) * g_ref[...]).astype(BF16)


def _rope128(y, cos, sin):
    return y * cos + pltpu.roll(y, 64, 1) * sin


def _rope64(y, cos, sin):
    lane = lax.broadcasted_iota(I32, y.shape, 1)
    rot = jnp.where((lane & 63) < 32, pltpu.roll(y, 96, 1), pltpu.roll(y, 32, 1))
    return y * cos + rot * sin


class _Cols:
    def __init__(self, array, n, block=0):
        assert array.shape[1] >= (block + 1) * n
        self.array, self.n, self.block = array, n, block


class _Rows:
    def __init__(self, batch, t_q, pos0, tm):
        self.batch, self.t_q, self.pos0, self.tm = batch, t_q, pos0, tm
        self.m = batch * t_q
        self.tab_blocks = max(t_q // tm, 1)

    def tab_spec(self):
        nb = self.tab_blocks
        return pl.BlockSpec((self.tm, LANES), lambda i: (i % nb, 0))

    def tables(self, d):
        period = max(self.t_q, self.tm)
        pos = self.pos0 + (jnp.arange(period) % self.t_q)
        half = d // 2
        inv = ROPE_THETA ** (-2.0 * jnp.arange(half, dtype=F32) / d)
        ang = pos.astype(F32)[:, None] * inv[None, :]
        cos, sin = jnp.cos(ang), jnp.sin(ang)
        reps = LANES // d
        cos_f = jnp.tile(jnp.concatenate([cos, cos], axis=-1), (1, reps))
        sin_f = jnp.tile(jnp.concatenate([-sin, sin], axis=-1), (1, reps))
        return cos_f, sin_f


def _row_spec(tm, n):
    return pl.BlockSpec((tm, n), lambda i: (i, 0))


def _vec_spec(n):
    return pl.BlockSpec((1, n), lambda i: (0, 0))


class _Seg:
    def __init__(self, emit, w, aux, aux_specs, out_shape, out_specs):
        self.emit, self.w = emit, w
        self.aux, self.aux_specs = list(aux), list(aux_specs)
        self.out_shape, self.out_specs = list(out_shape), list(out_specs)


def _emit_qk(h, w_ref, aux, outs, *, n_heads, scale, use_rope, head_major, emit_f32):
    cos_ref, sin_ref, hg_ref = aux
    hg = hg_ref[...]
    of_ref = outs[0] if emit_f32 else None
    ob_ref = outs[-1]
    for c in range(n_heads // 2):
        y = jnp.dot(h, w_ref[:, c * MXU_N:(c + 1) * MXU_N], preferred_element_type=F32)
        for j in range(2):
            hd = 2 * c + j
            yh = y[:, j * LANES:(j + 1) * LANES]
            ms = jnp.mean(yh * yh, axis=-1, keepdims=True)
            yr = yh * lax.rsqrt(ms + EPS) * hg
            if use_rope:
                yr = _rope128(yr, cos_ref[...], sin_ref[...])
            if emit_f32 and len(of_ref.shape) == 3:
                of_ref[:, hd, :] = yr
            elif emit_f32:
                of_ref[:, hd * LANES:(hd + 1) * LANES] = yr
            yb = (yr * scale).astype(BF16)
            if head_major:
                ob_ref[hd] = yb
            else:
                ob_ref[:, hd * LANES:(hd + 1) * LANES] = yb


def _emit_plain(h, w_ref, aux, outs, *, n_cols, head_dim):
    of_ref, ob_ref = outs
    per_chunk = MXU_N // head_dim
    for c in range(n_cols // MXU_N):
        sl = slice(c * MXU_N, (c + 1) * MXU_N)
        y = jnp.dot(h, w_ref[:, sl], preferred_element_type=F32)
        if len(of_ref.shape) == 3:
            for j in range(per_chunk):
                of_ref[:, c * per_chunk + j, :] = y[:, j * head_dim:(j + 1) * head_dim]
        else:
            of_ref[:, sl] = y
        ob_ref[:, sl] = y.astype(BF16)


def _emit_gate(h, w_ref, aux, outs, *, n_cols):
    for c in range(n_cols // MXU_N):
        sl = slice(c * MXU_N, (c + 1) * MXU_N)
        y = jnp.dot(h, w_ref[:, sl], preferred_element_type=F32)
        outs[0][:, sl] = (y * (1.0 / (1.0 + jnp.exp(-y)))).astype(BF16)


def _emit_rope64(h, w_ref, aux, outs, *, n_cols):
    cos, sin = aux[0][...], aux[1][...]
    for c in range(n_cols // MXU_N):
        y = jnp.dot(h, w_ref[:, c * MXU_N:(c + 1) * MXU_N], preferred_element_type=F32)
        for j in range(2):
            col = c * MXU_N + j * LANES
            outs[0][:, col:col + LANES] = _rope64(y[:, j * LANES:(j + 1) * LANES], cos, sin).astype(BF16)


def _emit_idx_small(h, w_ref, aux, outs):
    cos_ref, sin_ref, hg_ref = aux
    y = jnp.dot(h, w_ref[...], preferred_element_type=F32)
    is_key = lax.broadcasted_iota(I32, y.shape, 1) < IDX_DIM
    ms = jnp.sum(jnp.where(is_key, y * y, 0.0), axis=-1, keepdims=True) * (1.0 / IDX_DIM)
    kn = _rope64(y * lax.rsqrt(ms + EPS) * hg_ref[...], cos_ref[...], sin_ref[...])
    outs[0][...] = jnp.where(is_key, kn, y)


def _emit_logf(h, w_ref, aux, outs):
    z = jnp.dot(h, w_ref[...], preferred_element_type=F32) + aux[0][...]
    outs[0][...] = jnp.minimum(z, 0.0) - jnp.log1p(jnp.exp(-jnp.abs(z)))


def _proj_kernel(x_ref, g_ref, *refs, segs):
    h = _norm_rows(x_ref, g_ref)
    n_in = sum(1 + len(s.aux) for s in segs)
    ins, outs = refs[:n_in], refs[n_in:]
    i = o = 0
    for s in segs:
        s.emit(h, ins[i], ins[i + 1:i + 1 + len(s.aux)], outs[o:o + len(s.out_shape)])
        i += 1 + len(s.aux)
        o += len(s.out_shape)


def _project(rows, x, g, segs, name):
    m, d = x.shape
    tm = rows.tm
    in_specs = [pl.BlockSpec((tm, d), lambda i: (i, 0)), pl.BlockSpec((1, d), lambda i: (0, 0))]
    args = [x, g.reshape(1, d)]
    for s in segs:
        block = s.w.block
        in_specs.append(pl.BlockSpec((d, s.w.n), lambda i, block=block: (0, block), pipeline_mode=pl.Buffered(1)))
        in_specs += s.aux_specs
        args += [s.w.array] + s.aux
    out = pl.pallas_call(
        functools.partial(_proj_kernel, segs=segs), grid=(m // tm,), in_specs=in_specs, name=name,
        out_specs=[sp for s in segs for sp in s.out_specs],
        out_shape=[sh for s in segs for sh in s.out_shape],
        compiler_params=_params(("parallel",)))(*args)
    res, o = [], 0
    for s in segs:
        res.append(out[o:o + len(s.out_shape)])
        o += len(s.out_shape)
    return res


def _state_out(rows, n, head_dim, cache_layout):
    if cache_layout:
        return (jax.ShapeDtypeStruct((rows.m, n // head_dim, head_dim), F32),
                pl.BlockSpec((rows.tm, n // head_dim, head_dim), lambda i: (i, 0, 0)))
    return jax.ShapeDtypeStruct((rows.m, n), F32), _row_spec(rows.tm, n)


def _seg_qk(rows, w, head_gain, *, scale, use_rope, head_major, emit_f32, cache_layout=False):
    n = w.n
    n_heads = n // HEAD_DIM
    cos, sin = rows.tables(HEAD_DIM)
    out_shape, out_specs = [], []
    if emit_f32:
        shape, spec = _state_out(rows, n, HEAD_DIM, cache_layout)
        out_shape.append(shape)
        out_specs.append(spec)
    if head_major:
        out_shape.append(jax.ShapeDtypeStruct((n_heads, rows.m, HEAD_DIM), BF16))
        out_specs.append(pl.BlockSpec((n_heads, rows.tm, HEAD_DIM), lambda i: (0, i, 0)))
    else:
        out_shape.append(jax.ShapeDtypeStruct((rows.m, n), BF16))
        out_specs.append(_row_spec(rows.tm, n))
    emit = functools.partial(_emit_qk, n_heads=n_heads, scale=scale, use_rope=use_rope,
                             head_major=head_major, emit_f32=emit_f32)
    return _Seg(emit, w, (cos, sin, head_gain.reshape(1, HEAD_DIM)),
                (rows.tab_spec(), rows.tab_spec(), _vec_spec(HEAD_DIM)), out_shape, out_specs)


def _seg_plain(rows, w, head_dim, cache_layout):
    n = w.n
    shape, spec = _state_out(rows, n, head_dim, cache_layout)
    return _Seg(functools.partial(_emit_plain, n_cols=n, head_dim=head_dim), w, (), (),
                [shape, jax.ShapeDtypeStruct((rows.m, n), BF16)], [spec, _row_spec(rows.tm, n)])


def _seg_gate(rows, w):
    n = w.n
    return _Seg(functools.partial(_emit_gate, n_cols=n), w, (), (),
                [jax.ShapeDtypeStruct((rows.m, n), BF16)], [_row_spec(rows.tm, n)])


def _seg_rope64(rows, w):
    n = w.n
    cos, sin = rows.tables(IDX_DIM)
    return _Seg(functools.partial(_emit_rope64, n_cols=n), w, (cos, sin), (rows.tab_spec(), rows.tab_spec()),
                [jax.ShapeDtypeStruct((rows.m, n), BF16)], [_row_spec(rows.tm, n)])


def _seg_idx_small(rows, w, key_gain):
    cos, sin = rows.tables(IDX_DIM)
    gain = jnp.concatenate([key_gain, jnp.zeros((LANES - IDX_DIM,), F32)]).reshape(1, LANES)
    return _Seg(_emit_idx_small, w, (cos, sin, gain), (rows.tab_spec(), rows.tab_spec(), _vec_spec(LANES)),
                [jax.ShapeDtypeStruct((rows.m, LANES), F32)], [_row_spec(rows.tm, LANES)])


def _seg_logf(rows, w, fb):
    fbp = jnp.concatenate([fb, jnp.zeros((LANES - C_HEADS,), F32)]).reshape(1, LANES)
    return _Seg(_emit_logf, w, (fbp,), (_vec_spec(LANES),),
                [jax.ShapeDtypeStruct((rows.m, LANES), F32)], [_row_spec(rows.tm, LANES)])


def _out_kernel(x_ref, o_ref, sg_ref, w_ref, y_ref, *, n_cols):
    a = o_ref[...] * sg_ref[...]
    for c in range(n_cols // MXU_N):
        sl = slice(c * MXU_N, (c + 1) * MXU_N)
        y_ref[:, sl] = x_ref[:, sl] + jnp.dot(a, w_ref[:, sl].astype(BF16), preferred_element_type=F32)


def _out_proj(rows, x, o, sg, w):
    m, d = x.shape
    tm = rows.tm
    kern = functools.partial(_out_kernel, n_cols=d)
    return pl.pallas_call(
        kern, grid=(m // tm,), name="out_proj",
        in_specs=[_row_spec(tm, d), _row_spec(tm, d), _row_spec(tm, d),
                  pl.BlockSpec((d, d), lambda i: (0, 0), pipeline_mode=pl.Buffered(1))],
        out_specs=_row_spec(tm, d), out_shape=jax.ShapeDtypeStruct((m, d), F32),
        compiler_params=_params(("parallel",)))(x, o, sg, w)


class _Attn:
    def __init__(self, batch, t_q, s_pad, s_valid, q_off, tq, tk, pv_dtype):
        self.batch, self.t_q, self.s_pad, self.s_valid = batch, t_q, s_pad, s_valid
        self.q_off, self.tq, self.tk = q_off, tq, tk
        self.nq = t_q // tq
        self.pv_dtype = pv_dtype


def _chunk_limits(i, cfg):
    q0 = cfg.q_off + i * cfg.tq
    qpos = q0 + lax.broadcasted_iota(I32, (cfg.tq, 1), 0)
    qend = jnp.minimum(((qpos >> CHUNK_SHIFT) + 1) << CHUNK_SHIFT, cfg.s_valid)
    kmin = jnp.minimum(((q0 >> CHUNK_SHIFT) + 1) << CHUNK_SHIFT, cfg.s_valid)
    kend = jnp.minimum((((q0 + cfg.tq - 1) >> CHUNK_SHIFT) + 1) << CHUNK_SHIFT, cfg.s_valid)
    return qend, kmin // cfg.tk, (kend + cfg.tk - 1) // cfg.tk


def _causal_limits(i, cfg):
    q0 = cfg.q_off + i * cfg.tq
    qpos = q0 + lax.broadcasted_iota(I32, (cfg.tq, 1), 0)
    return qpos + 1, (q0 + 1) // cfg.tk, (q0 + cfg.tq + cfg.tk - 1) // cfg.tk


def _qk(q, k):
    return lax.dot_general(q, k, (((1,), (1,)), ((), ())), preferred_element_type=F32)


def _visibility_bias(ks, qend, shape):
    kpos = ks + lax.broadcasted_iota(I32, shape, 1)
    return jnp.where(kpos < qend, 0.0, MASKED)


def _flash_step(qs, ks, vs, bias_fn, state, *, rows, strip):
    s_ref, p_ref, m_ref, l_ref, alpha_ref, acc_ref = state
    n = len(qs)
    strips = [slice(r * strip, (r + 1) * strip) for r in range(rows // strip)]
    chunks = [slice(a, a + LANES) for a in range(0, s_ref.shape[-1], LANES)]
    for c in range(n):
        s_ref[c] = _qk(qs[c], ks[c])
    for c in range(n):
        for r, rs in enumerate(strips):
            mx = None
            for ch in chunks:
                s = s_ref[c, rs, ch]
                b = bias_fn(c, r, ch)
                if b is not None:
                    s = s + b
                    s_ref[c, rs, ch] = s
                mx = s if mx is None else jnp.maximum(mx, s)
            m_old = m_ref[c, rs, :]
            m_new = jnp.maximum(m_old, jnp.broadcast_to(jnp.max(mx, axis=-1, keepdims=True), mx.shape))
            alpha_ref[c, rs, :] = jnp.exp2(m_old - m_new)
            m_ref[c, rs, :] = m_new
        for rs in strips:
            m = m_ref[c, rs, :]
            psum = None
            for ch in chunks:
                p = jnp.exp2(s_ref[c, rs, ch] - m)
                if l_ref is not None:
                    psum = p if psum is None else psum + p
                p_ref[c, rs, ch] = p.astype(p_ref.dtype)
            if l_ref is not None:
                l_ref[c, rs, :] = alpha_ref[c, rs, :] * l_ref[c, rs, :] + psum
    for c in range(n):
        v, dv = vs[c], vs[c].shape[1]
        scale = None
        if p_ref.dtype != BF16:
            top = jnp.max(jnp.max(jnp.abs(v), axis=0, keepdims=True).astype(F32), axis=1, keepdims=True)
            power = jnp.ceil(jnp.log2(jnp.maximum(top * (1.0 / NARROW_RANGE), 1.0)))
            scale = jnp.exp2(power)
            v = v * jnp.exp2(-power).astype(v.dtype)
        v = v.astype(p_ref.dtype)
        if l_ref is None:
            v = jnp.concatenate([v, jnp.ones(v.shape, v.dtype)], axis=1)
        pv = jnp.dot(p_ref[c], v, preferred_element_type=F32)
        if scale is not None:
            pv = jnp.concatenate([pv[:, :dv] * scale, pv[:, dv:]], axis=1) if l_ref is None else pv * scale
        alpha = jnp.concatenate([alpha_ref[c]] * (acc_ref.shape[-1] // LANES), axis=1)
        acc_ref[c] = alpha * acc_ref[c] + pv


def _init_state(state):
    _, _, m_ref, l_ref, _, acc_ref = state
    m_ref[...] = jnp.full(m_ref.shape, MASKED, F32)
    acc_ref[...] = jnp.zeros(acc_ref.shape, F32)
    if l_ref is not None:
        l_ref[...] = jnp.zeros(l_ref.shape, F32)


def _flash_scratch(cfg, chains, rows, dv, with_l):
    tk = cfg.tk
    stat = pltpu.VMEM((chains, rows, LANES), F32)
    return ([pltpu.VMEM((chains, rows, tk), F32), pltpu.VMEM((chains, rows, tk), cfg.pv_dtype), stat]
            + ([stat] if with_l else []) + [stat, pltpu.VMEM((chains, rows, dv), F32)])


def _dsa_kernel(q_ref, qi_ref, sm_ref, k_ref, v_ref, kit_ref, o_ref,
                key_ref, bias_ref, stat_ref, look_ref, s_ref, p_ref, m_ref, alpha_ref, acc_ref, *, cfg, topk):
    tq, tk = cfg.tq, cfg.tk
    i = pl.program_id(1)
    qend, _, n_kv = _chunk_limits(i, cfg)
    sub = min(tq, SCORE_ROWS)
    stat_ref[0] = jnp.full((tq, LANES), INT_MIN, I32)
    stat_ref[1] = jnp.zeros((tq, LANES), I32)

    def score_block(j, carry):
        ks = pl.multiple_of(j * tk, tk)
        for r0 in range(0, tq, sub):
            rs = slice(r0, r0 + sub)
            w = sm_ref[rs, IDX_DIM:IDX_DIM + IDX_HEADS] * (IDX_DIM ** -0.5 * IDX_HEADS ** -0.5)
            acc = jnp.zeros((sub, tk), F32)
            for pair in range(IDX_HEADS // 2):
                qp = qi_ref[rs, pair * LANES:(pair + 1) * LANES]
                for e in range(2):
                    hd = 2 * pair + e
                    sc = jnp.dot(qp, kit_ref[e, :, pl.ds(ks, tk)], preferred_element_type=F32)
                    acc = acc + w[:, hd:hd + 1] * jnp.maximum(sc, 0.0)
            bits = lax.bitcast_convert_type(acc, I32)
            key = bits ^ ((bits >> 31) & 0x7FFFFFFF)
            key = jnp.where(acc == 0.0, 0, key)
            kpos = ks + lax.broadcasted_iota(I32, (sub, tk), 1)
            key = jnp.where(kpos < qend[rs], key, NEG_INF_KEY)
            key_ref[rs, pl.ds(ks, tk)] = key
            kmax, nfin = stat_ref[0, rs, :], stat_ref[1, rs, :]
            for c in range(tk // LANES):
                kc = key[:, c * LANES:(c + 1) * LANES]
                kmax = jnp.maximum(kmax, kc)
                nfin = nfin + jnp.where(kc > NEG_INF_KEY, 1, 0)
            stat_ref[0, rs, :], stat_ref[1, rs, :] = kmax, nfin
        return carry

    lax.fori_loop(0, n_kv, score_block, 0)

    def key_chunk(j, c):
        return key_ref[:, pl.ds(pl.multiple_of(j * tk + c * LANES, LANES), LANES)]

    def lane_total(x):
        return jnp.broadcast_to(jnp.sum(x, axis=-1, keepdims=True), x.shape)

    def count_ge(cand):
        def body(j, part):
            ks = pl.multiple_of(j * tk, tk)
            ge = jnp.where(key_ref[:, pl.ds(ks, tk)] >= cand, 1, 0)
            for c in range(tk // LANES):
                part = part + ge[:, c * LANES:(c + 1) * LANES]
            return part
        part = lax.fori_loop(0, n_kv, body, jnp.zeros((tq, LANES), I32))
        return jnp.sum(part, axis=-1, keepdims=True)

    n_fin = lane_total(stat_ref[1])
    small = n_fin <= topk
    look_ref[0] = jnp.full((tq, LANES), NEG_INF_KEY + 1, I32)
    look_ref[1] = jnp.broadcast_to(jnp.max(stat_ref[0], axis=-1, keepdims=True), (tq, LANES)) + 1
    look_ref[2] = n_fin

    def pending():
        lo, hi, cnt = look_ref[0], look_ref[1], look_ref[2]
        open_row = (cnt != topk) & (hi != lo + 1) & jnp.logical_not(small)
        return jnp.max(jnp.where(open_row, 1, 0))

    groups = [slice(r, r + min(tq, SEARCH_ROWS)) for r in range(0, tq, min(tq, SEARCH_ROWS))]

    def halve(n_blocks):
        state = [[look_ref[v, gs, :] for v in range(3)] for gs in groups]
        part = jnp.zeros((groups[0].stop, LANES), I32)
        for _ in range(BISECT_UNROLL):
            for g, gs in enumerate(groups):
                lo, hi, cnt = state[g]
                mid = (lo >> 1) + (hi >> 1) + (lo & hi & 1)
                part = part >> 31
                for a in range(0, n_blocks * tk, LANES):
                    part = part + jnp.where(key_ref[gs, a:a + LANES] >= mid, 1, 0)
                c = lane_total(part)
                up = c >= topk
                state[g] = [jnp.where(up, mid, lo), jnp.where(up, hi, mid), jnp.where(up, c, cnt)]
        for g, gs in enumerate(groups):
            for v in range(3):
                look_ref[v, gs, :] = state[g][v]

    def search_round(_):
        for n_blocks in range(1, cfg.s_pad // tk + 1):
            pl.when(n_kv == n_blocks)(functools.partial(halve, n_blocks))
        return pending()

    lax.while_loop(lambda flag: flag > 0, search_round, pending())
    thr, n_ge = look_ref[0], look_ref[2]
    tie_rows = (n_ge > topk) & jnp.logical_not(small)

    @pl.when(jnp.max(jnp.where(tie_rows, 1, 0)) > 0)
    def _drop_late_ties():
        need_f = jnp.broadcast_to((topk - count_ge(thr[:, :1] + 1)).astype(F32), (tq, LANES))
        upper = jnp.where(lax.broadcasted_iota(I32, (LANES, LANES), 0) <= lax.broadcasted_iota(I32, (LANES, LANES), 1),
                          1.0, 0.0).astype(BF16)

        def body(j, seen):
            for c in range(tk // LANES):
                kb = key_chunk(j, c)
                eq = kb == thr
                eq_f = jnp.where(eq, 1.0, 0.0)
                rank = seen + jnp.dot(eq_f.astype(BF16), upper, preferred_element_type=F32)
                drop = eq & (rank > need_f) & tie_rows
                key_ref[:, pl.ds(pl.multiple_of(j * tk + c * LANES, LANES), LANES)] = jnp.where(drop, thr - 1, kb)
                seen = seen + lane_total(eq_f)
            return seen

        lax.fori_loop(0, n_kv, body, jnp.zeros((tq, LANES), F32))

    thr_keep = jnp.where(small, NEG_INF_KEY + 1, jnp.maximum(thr, NEG_INF_KEY + 1))
    state = (s_ref, p_ref, m_ref, None, alpha_ref, acc_ref)
    _init_state(state)
    strip = min(tq, STRIP_ROWS)
    strips_per_head = tq // strip

    def attn_block(j, carry):
        ks = pl.multiple_of(j * tk, tk)
        for c in range(tk // LANES):
            bias_ref[:, c * LANES:(c + 1) * LANES] = jnp.where(key_chunk(j, c) >= thr_keep, 0.0, MASKED)
        cols =[slice(g * HEAD_DIM, (g + 1) * HEAD_DIM) for g in range(A_KV_HEADS)]
        _flash_step([q_ref[g * A_REP:(g + 1) * A_REP].reshape(A_REP * tq, HEAD_DIM) for g in range(A_KV_HEADS)],
                    [k_ref[pl.ds(ks, tk), c] for c in cols],
                    [v_ref[pl.ds(ks, tk), c] for c in cols],
                    lambda c, r, ch: bias_ref[(r % strips_per_head) * strip:(r % strips_per_head + 1) * strip, ch],
                    state, rows=A_REP * tq, strip=strip)
        return carry

    lax.fori_loop(0, n_kv, attn_block, 0)
    for g in range(A_KV_HEADS):
        o = acc_ref[g, :, :HEAD_DIM] / acc_ref[g, :, HEAD_DIM:]
        for r in range(A_REP):
            hd = g * A_REP + r
            o_ref[:, hd * HEAD_DIM:(hd + 1) * HEAD_DIM] = o[r * tq:(r + 1) * tq].astype(BF16)


def _dsa_attention(cfg, q_hm, q_row0, qi, sm, k, v, kit, topk):
    tq, nq, s = cfg.tq, cfg.nq, cfg.s_pad
    qb0 = q_row0 // tq
    kern = functools.partial(_dsa_kernel, cfg=cfg, topk=topk)
    return pl.pallas_call(
        kern, grid=(cfg.batch, nq), name="dsa_attention",
        in_specs=[pl.BlockSpec((A_HEADS, tq, HEAD_DIM), lambda b, i: (0, qb0 + b * nq + i, 0)),
                  pl.BlockSpec((tq, IDX_HEADS * IDX_DIM), lambda b, i: (qb0 + b * nq + i, 0)),
                  pl.BlockSpec((tq, LANES), lambda b, i: (qb0 + b * nq + i, 0)),
                  pl.BlockSpec((s, A_KV_HEADS * HEAD_DIM), lambda b, i: (b, 0), pipeline_mode=pl.Buffered(1)),
                  pl.BlockSpec((s, A_KV_HEADS * HEAD_DIM), lambda b, i: (b, 0), pipeline_mode=pl.Buffered(1)),
                  pl.BlockSpec((None, 2, LANES, s), lambda b, i: (b, 0, 0, 0), pipeline_mode=pl.Buffered(1))],
        out_specs=pl.BlockSpec((tq, MIX_WIDTH), lambda b, i: (b * nq + i, 0)),
        out_shape=jax.ShapeDtypeStruct((cfg.batch * cfg.t_q, MIX_WIDTH), BF16),
        scratch_shapes=[pltpu.VMEM((tq, s), I32), pltpu.VMEM((tq, cfg.tk), F32), pltpu.VMEM((2, tq, LANES), I32),
                        pltpu.VMEM((3, tq, LANES), I32)]
        + _flash_scratch(cfg, A_KV_HEADS, A_REP * tq, 2 * HEAD_DIM, with_l=False),
        compiler_params=_params(("parallel", "arbitrary")))(q_hm, qi, sm, k, v, kit)


B_STEP_HEADS = 2


def _run_blocks(n_full, n_kv, step):
    def plain(j, carry):
        step(j, False)
        return carry

    def masked(j, carry):
        step(j, True)
        return carry

    lax.fori_loop(0, n_full, plain, 0)
    lax.fori_loop(n_full, n_kv, masked, 0)


def _diff_kernel(q_ref, k_ref, v_ref, lq1_ref, lk1_ref, lq2_ref, lk2_ref, sub_ref, o_ref,
                 bias_ref, s_ref, p_ref, m_ref, l_ref, alpha_ref, acc_ref, *, cfg, lam_init):
    tq, tk = cfg.tq, cfg.tk
    qend, n_full, n_kv = _chunk_limits(pl.program_id(2), cfg)
    state = (s_ref, p_ref, m_ref, l_ref, alpha_ref, acc_ref)
    _init_state(state)
    strip = min(tq, STRIP_ROWS)
    n_chains = 2 * B_STEP_HEADS

    def step(j, masked):
        ks = pl.multiple_of(j * tk, tk)
        if masked:
            bias_ref[...] = _visibility_bias(ks, qend, (tq, tk))
        qk_cols = [slice(c * HEAD_DIM, (c + 1) * HEAD_DIM) for c in range(n_chains)]
        v_cols = [slice((c // 2) * B_V_DIM, (c // 2 + 1) * B_V_DIM) for c in range(n_chains)]
        _flash_step([q_ref[:, c] for c in qk_cols],
                    [k_ref[pl.ds(ks, tk), c] for c in qk_cols],
                    [v_ref[pl.ds(ks, tk), c] for c in v_cols],
                    (lambda c, r, ch: bias_ref[r * strip:(r + 1) * strip, ch]) if masked
                    else (lambda c, r, ch: None),
                    state, rows=tq, strip=strip)

    _run_blocks(n_full, n_kv, step)
    lam = (jnp.exp(jnp.sum(lq1_ref[...] * lk1_ref[...], axis=-1, keepdims=True))
           - jnp.exp(jnp.sum(lq2_ref[...] * lk2_ref[...], axis=-1, keepdims=True)) + lam_init)
    def normalised(c):
        return acc_ref[c] / jnp.sum(l_ref[c], axis=-1, keepdims=True)

    for h in range(B_STEP_HEADS):
        o = normalised(2 * h) - lam * normalised(2 * h + 1)
        ms = jnp.mean(o * o, axis=-1, keepdims=True)
        o_ref[:, h * B_V_DIM:(h + 1) * B_V_DIM] = (
            o * lax.rsqrt(ms + EPS) * sub_ref[...] * (1.0 - lam_init)).astype(BF16)


def _diff_attention(cfg, q, q_row0, k, v, lam_vecs, subln, lam_init):
    tq, nq, s = cfg.tq, cfg.nq, cfg.s_pad
    qb0 = q_row0 // tq
    width = B_STEP_HEADS * B_V_DIM
    kern = functools.partial(_diff_kernel, cfg=cfg, lam_init=lam_init)
    vec = pl.BlockSpec((1, HEAD_DIM), lambda b, h, i: (0, 0))
    return pl.pallas_call(
        kern, grid=(cfg.batch, B_HEADS // B_STEP_HEADS, nq), name="diff_attention",
        in_specs=[pl.BlockSpec((tq, width), lambda b, h, i: (qb0 + b * nq + i, h)),
                  pl.BlockSpec((s, width), lambda b, h, i: (b, h)),
                  pl.BlockSpec((s, width), lambda b, h, i: (b, h)),
                  vec, vec, vec, vec,
                  pl.BlockSpec((1, B_V_DIM), lambda b, h, i: (0, 0))],
        out_specs=pl.BlockSpec((tq, width), lambda b, h, i: (b * nq + i, h)),
        out_shape=jax.ShapeDtypeStruct((cfg.batch * cfg.t_q, MIX_WIDTH), BF16),
        scratch_shapes=[pltpu.VMEM((tq, cfg.tk), F32)]
        + _flash_scratch(cfg, 2 * B_STEP_HEADS, tq, B_V_DIM, with_l=True),
        compiler_params=_params(("parallel", "parallel", "arbitrary")))(
            q, k, v, *[a.reshape(1, HEAD_DIM) for a in lam_vecs], subln.reshape(1, B_V_DIM))


C_STEP_HEADS = 4


def _fox_kernel(q_ref, k_ref, v_ref, cq_ref, ck_ref, o_ref,
                bias_ref, cqh_ref, s_ref, p_ref, m_ref, alpha_ref, acc_ref, *, cfg):
    tq, tk = cfg.tq, cfg.tk
    hg = pl.program_id(1)
    qend, n_full, n_kv = _causal_limits(pl.program_id(2), cfg)
    state = (s_ref, p_ref, m_ref, None, alpha_ref, acc_ref)
    _init_state(state)
    strip = min(tq, STRIP_ROWS)
    lane = lax.broadcasted_iota(I32, (tq, LANES), 1)
    for c in range(C_STEP_HEADS):
        head_lane = lane == hg * C_STEP_HEADS + c
        cqh_ref[c] = jnp.broadcast_to(
            jnp.sum(jnp.where(head_lane, cq_ref[...], 0.0), axis=-1, keepdims=True), (tq, LANES))

    def step(j, masked):
        ks = pl.multiple_of(j * tk, tk)
        if masked:
            bias_ref[...] = _visibility_bias(ks, qend, (tq, tk))

        def bias(c, r, ch):
            rs = slice(r * strip, (r + 1) * strip)
            b = cqh_ref[c, rs, :] - ck_ref[c:c + 1, pl.ds(pl.multiple_of(ks + ch.start, LANES), LANES)]
            return b + bias_ref[rs, ch] if masked else b

        cols = [slice(c * HEAD_DIM, (c + 1) * HEAD_DIM) for c in range(C_STEP_HEADS)]
        _flash_step([q_ref[:, c] for c in cols],
                    [k_ref[pl.ds(ks, tk), c] for c in cols],
                    [v_ref[pl.ds(ks, tk), c] for c in cols],
                    bias, state, rows=tq, strip=strip)

    _run_blocks(n_full, n_kv, step)
    for c in range(C_STEP_HEADS):
        o_ref[:, c * HEAD_DIM:(c + 1) * HEAD_DIM] = (
            acc_ref[c, :, :HEAD_DIM] / acc_ref[c, :, HEAD_DIM:]).astype(BF16)


def _fox_attention(cfg, q, q_row0, k, v, cq, ck_rows):
    tq, nq, s = cfg.tq, cfg.nq, cfg.s_pad
    qb0 = q_row0 // tq
    width = C_STEP_HEADS * HEAD_DIM
    groups = C_HEADS // C_STEP_HEADS
    kern = functools.partial(_fox_kernel, cfg=cfg)
    return pl.pallas_call(
        kern, grid=(cfg.batch, groups, nq), name="fox_attention",
        in_specs=[pl.BlockSpec((tq, width), lambda b, h, i: (qb0 + b * nq + i, h)),
                  pl.BlockSpec((s, width), lambda b, h, i: (b, h)),
                  pl.BlockSpec((s, width), lambda b, h, i: (b, h)),
                  pl.BlockSpec((tq, LANES), lambda b, h, i: (qb0 + b * nq + i, 0)),
                  pl.BlockSpec((None, C_STEP_HEADS, s), lambda b, h, i: (b * groups + h, 0, 0))],
        out_specs=pl.BlockSpec((tq, width), lambda b, h, i: (b * nq + i, h)),
        out_shape=jax.ShapeDtypeStruct((cfg.batch * cfg.t_q, MIX_WIDTH), BF16),
        scratch_shapes=[pltpu.VMEM((tq, cfg.tk), F32), pltpu.VMEM((C_STEP_HEADS, tq, LANES), F32)]
        + _flash_scratch(cfg, C_STEP_HEADS, tq, 2 * HEAD_DIM, with_l=False),
        compiler_params=_params(("parallel", "parallel", "arbitrary")))(q, k, v, cq, ck_rows)


def _cumsum_kernel(x_ref, o_ref, carry_ref, *, tb):
    @pl.when(pl.program_id(1) == 0)
    def _():
        carry_ref[...] = jnp.zeros(carry_ref.shape, F32)

    x = x_ref[...]
    hi = x.astype(BF16)
    r1 = x - hi.astype(F32)
    mid = r1.astype(BF16)
    low = (r1 - mid.astype(F32)).astype(BF16)
    tri = jnp.where(lax.broadcasted_iota(I32, (tb, tb), 0) >= lax.broadcasted_iota(I32, (tb, tb), 1),
                    1.0, 0.0).astype(BF16)
    c = (jnp.dot(tri, hi, preferred_element_type=F32) + jnp.dot(tri, mid, preferred_element_type=F32)
         + jnp.dot(tri, low, preferred_element_type=F32)) + carry_ref[...]
    o_ref[...] = c * LOG2E
    carry_ref[...] = c[tb - 1:tb, :]


def _cumsum_rows(x, batch, t, tb):
    nt = t // tb
    return pl.pallas_call(
        functools.partial(_cumsum_kernel, tb=tb), grid=(batch, nt), name="cumsum_rows",
        in_specs=[pl.BlockSpec((tb, LANES), lambda b, j: (b * nt + j, 0))],
        out_specs=pl.BlockSpec((tb, LANES), lambda b, j: (b * nt + j, 0)),
        out_shape=jax.ShapeDtypeStruct((batch * t, LANES), F32),
        scratch_shapes=[pltpu.VMEM((1, LANES), F32)],
        compiler_params=_params(("parallel", "arbitrary")))(x)


def _split_w(w_in, sizes):
    offs = np.cumsum((0,) + tuple(sizes))
    return [w_in[:, int(offs[i]):int(offs[i + 1])] for i in range(len(sizes))]


def _pad_rows(a, rows):
    return jnp.pad(a, ((0, 0), (0, rows - a.shape[1])) + ((0, 0),) * (a.ndim - 2))


def _sample_keys_kernel(*refs, heads, hd, past, s_new):
    slabs, new_ref, o_ref = refs[:-2], refs[-2], refs[-1]
    for h in range(heads):
        for j, slab_ref in enumerate(slabs):
            col = h * hd + j * LANES
            o_ref[0:past, col:col + LANES] = slab_ref[pl.ds(h, past, stride=heads), :].astype(BF16)
    o_ref[past:past + s_new, :] = new_ref[...].astype(BF16)
    o_ref[past + s_new:, :] = jnp.zeros((o_ref.shape[0] - past - s_new, o_ref.shape[1]), BF16)


def _sample_keys(cache, new, s_pad):
    b, past, heads, hd = cache.shape
    n = heads * hd
    s_new = new.shape[1]
    if heads % 8 or hd % LANES:
        full = jnp.concatenate([cache.reshape(b, past, n), new.reshape(b, s_new, n)], axis=1)
        return _pad_rows(full, s_pad).astype(BF16).reshape(b * s_pad, n)
    kern = functools.partial(_sample_keys_kernel, heads=heads, hd=hd, past=past, s_new=s_new)
    pairs = cache.reshape(b, past * heads, hd)
    slabs = hd // LANES
    return pl.pallas_call(
        kern, grid=(b,), name="sample_keys",
        in_specs=[pl.BlockSpec((None, past * heads, LANES), lambda i, j=j: (i, 0, j)) for j in range(slabs)]
        + [pl.BlockSpec((s_new, n), lambda i: (i, 0))],
        out_specs=pl.BlockSpec((s_pad, n), lambda i: (i, 0)),
        out_shape=jax.ShapeDtypeStruct((b * s_pad, n), BF16),
        compiler_params=_params(("parallel",)))(*([pairs] * slabs), new.reshape(b * s_new, n))


def _key_transposed_pairs(ki, batch, s):
    kt = jnp.swapaxes(ki.reshape(batch, s, IDX_DIM), 1, 2).astype(BF16)
    z = jnp.zeros_like(kt)
    return jnp.stack([jnp.concatenate([kt, z], axis=1), jnp.concatenate([z, kt], axis=1)], axis=1)


def _layer_a(pr, sr, cfg_p, cfg_s, xp, xs, ck, cv, cki, norm, w_in, w_out, qn, kn, ikn):
    w_all = w_in.astype(BF16)
    wq, wk, wv, wqi = _Cols(w_all, 2048, 0), _Cols(w_all, 512, 4), _Cols(w_all, 512, 5), _Cols(w_all, 1024, 3)
    wsm = _Cols(w_all, LANES, sum(A_SIZES[:4]) // LANES)
    wg = _Cols(_split_w(w_all, A_SIZES)[6], MIX_WIDTH)
    wo = w_out

    def project(rows, x):
        (q,), (sg,) = _project(rows, x, norm, [
            _seg_qk(rows, wq, qn, scale=HEAD_DIM ** -0.5 * LOG2E, use_rope=True, head_major=True, emit_f32=False),
            _seg_gate(rows, wg)], "proj_a_query_gate")
        (qi,), (k_f, k_b), (sm,), (v_f, v_b) = _project(rows, x, norm, [
            _seg_rope64(rows, wqi),
            _seg_qk(rows, wk, kn, scale=1.0, use_rope=True, head_major=False, emit_f32=True, cache_layout=True),
            _seg_idx_small(rows, wsm, ikn), _seg_plain(rows, wv, HEAD_DIM, True)], "proj_a_index_kv")
        return q, k_f, k_b, v_f, v_b, qi, sm, sg

    q, k_f, k_b, v_f, v_b, qi, sm, sg = project(pr, xp)
    ki_f = sm[:, :IDX_DIM]
    o = _dsa_attention(cfg_p, q, 0, qi, sm, k_b, v_b,
                       _key_transposed_pairs(ki_f, pr.batch, pr.t_q), min(TOPK_MAX, pr.t_q // 4))
    yp = _out_proj(pr, xp, o, sg, wo)

    q2, k2_f, _, v2_f, _, qi2, sm2, sg2 = project(sr, xs)
    ki2_f = sm2[:, :IDX_DIM]
    b, s = sr.batch, sr.t_q
    k_all = _sample_keys(ck, k2_f.reshape(b, s, -1), cfg_s.s_pad)
    v_all = _sample_keys(cv, v2_f.reshape(b, s, -1), cfg_s.s_pad)
    ki_all = _pad_rows(jnp.concatenate([cki, ki2_f.reshape(b, s, IDX_DIM)], axis=1), cfg_s.s_pad)
    o2 = _dsa_attention(cfg_s, q2, 0, qi2, sm2, k_all, v_all,
                        _key_transposed_pairs(ki_all.reshape(-1, IDX_DIM), b, cfg_s.s_pad),
                        min(TOPK_MAX, cfg_s.s_valid // 4))
    ys = _out_proj(sr, xs, o2, sg2, wo)
    kv = (A_KV_HEADS, HEAD_DIM)
    state = (k_f.reshape(pr.batch, pr.t_q, *kv), v_f.reshape(pr.batch, pr.t_q, *kv),
             ki_f.reshape(pr.batch, pr.t_q, IDX_DIM),
             k2_f.reshape(b, s, *kv), v2_f.reshape(b, s, *kv), ki2_f.reshape(b, s, IDX_DIM))
    return yp, ys, state


def _layer_b(layer, pr, sr, cfg_p, cfg_s, xp, xs, ck, cv, norm, w_in, w_out, qn, kn,
             lq1, lk1, lq2, lk2, subln):
    lam_init = 0.8 - 0.6 * float(np.exp(-0.3 * layer))
    w_all = w_in.astype(BF16)
    wq, wk, wv, wg = [_Cols(w_all, 2048, j) for j in range(4)]
    wo = w_out

    def project(rows, x):
        (q,), (sg,) = _project(rows, x, norm, [
            _seg_qk(rows, wq, qn, scale=HEAD_DIM ** -0.5 * LOG2E, use_rope=True, head_major=False, emit_f32=False),
            _seg_gate(rows, wg)], "proj_b_query_gate")
        (k_f, k_b), (v_f, v_b) = _project(rows, x, norm, [
            _seg_qk(rows, wk, kn, scale=1.0, use_rope=True, head_major=False, emit_f32=True, cache_layout=True),
            _seg_plain(rows, wv, B_V_DIM, True)], "proj_b_kv")
        return q, k_f, k_b, v_f, v_b, sg

    lam_vecs = (lq1, lk1, lq2, lk2)
    q, k_f, k_b, v_f, v_b, sg = project(pr, xp)
    o = _diff_attention(cfg_p, q, 0, k_b, v_b, lam_vecs, subln, lam_init)
    yp = _out_proj(pr, xp, o, sg, wo)

    q2, k2_f, _, v2_f, _, sg2 = project(sr, xs)
    b, s = sr.batch, sr.t_q
    k_all = _sample_keys(ck, k2_f.reshape(b, s, -1), cfg_s.s_pad)
    v_all = _sample_keys(cv, v2_f.reshape(b, s, -1), cfg_s.s_pad)
    o2 = _diff_attention(cfg_s, q2, 0, k_all, v_all, lam_vecs, subln, lam_init)
    ys = _out_proj(sr, xs, o2, sg2, wo)
    state = (k_f.reshape(pr.batch, pr.t_q, 2 * B_HEADS, HEAD_DIM), v_f.reshape(pr.batch, pr.t_q, B_HEADS, B_V_DIM),
             k2_f.reshape(b, s, 2 * B_HEADS, HEAD_DIM), v2_f.reshape(b, s, B_HEADS, B_V_DIM))
    return yp, ys, state


def _head_rows(c, batch, s):
    rows = jnp.swapaxes(c.reshape(batch, s, LANES)[:, :, :C_HEADS], 1, 2)
    return rows.reshape(batch * C_HEADS // C_STEP_HEADS, C_STEP_HEADS, s)


def _layer_c(pr, sr, cfg_p, cfg_s, xp, xs, ck, cv, clogf, norm, w_in, w_out, qn, kn, fb):
    w_all = w_in.astype(BF16)
    wq, wk, wv = [_Cols(w_all, 2048, j) for j in range(3)]
    wf = _Cols(w_all, LANES, sum(C_SIZES[:3]) // LANES)
    wg = _Cols(_split_w(w_all, C_SIZES)[4], MIX_WIDTH)
    wo = w_out

    def project(rows, x):
        (q,), (logf,), (sg,) = _project(rows, x, norm, [
            _seg_qk(rows, wq, qn, scale=HEAD_DIM ** -0.5 * LOG2E, use_rope=False, head_major=False, emit_f32=False),
            _seg_logf(rows, wf, fb), _seg_gate(rows, wg)], "proj_c_query_gate")
        (k_f, k_b), (v_f, v_b) = _project(rows, x, norm, [
            _seg_qk(rows, wk, kn, scale=1.0, use_rope=False, head_major=False, emit_f32=True),
            _seg_plain(rows, wv, HEAD_DIM, False)], "proj_c_kv")
        return q, k_f, k_b, v_f, v_b, logf, sg

    q, k_f, k_b, v_f, v_b, logf, sg = project(pr, xp)
    c = _cumsum_rows(logf, pr.batch, pr.t_q, 512)
    o = _fox_attention(cfg_p, q, 0, k_b, v_b, c, _head_rows(c, pr.batch, pr.t_q))
    yp = _out_proj(pr, xp, o, sg, wo)

    q2, k2_f, _, v2_f, _, logf2, sg2 = project(sr, xs)
    b, s = sr.batch, sr.t_q
    k_all = _sample_keys(ck, k2_f.reshape(b, s, -1), cfg_s.s_pad)
    v_all = _sample_keys(cv, v2_f.reshape(b, s, -1), cfg_s.s_pad)
    logf_all = jnp.concatenate([jnp.pad(clogf.astype(F32), ((0, 0), (0, 0), (0, LANES - C_HEADS))),
                                logf2.reshape(b, s, LANES)], axis=1)
    c2 = _cumsum_rows(_pad_rows(logf_all, cfg_s.s_pad).reshape(b * cfg_s.s_pad, LANES), b, cfg_s.s_pad, LANES)
    cq2 = c2.reshape(b, cfg_s.s_pad, LANES)[:, PAST_LEN:PAST_LEN + s].reshape(b * s, LANES)
    o2 = _fox_attention(cfg_s, q2, 0, k_all, v_all, cq2, _head_rows(c2, b, cfg_s.s_pad))
    ys = _out_proj(sr, xs, o2, sg2, wo)
    hd = (C_HEADS, HEAD_DIM)
    state = (k_f.reshape(pr.batch, pr.t_q, *hd), v_f.reshape(pr.batch, pr.t_q, *hd),
             logf[:, :C_HEADS].reshape(pr.batch, pr.t_q, C_HEADS),
             k2_f.reshape(b, s, *hd), v2_f.reshape(b, s, *hd), logf2[:, :C_HEADS].reshape(b, s, C_HEADS))
    return yp, ys, state


def kernel(x_prompt, x_sample, cache_l0_k, cache_l0_v, cache_l0_kidx, cache_l1_k, cache_l1_v, cache_l2_k, cache_l2_v, cache_l2_logf, cache_l3_k, cache_l3_v, cache_l3_kidx, l0_norm, l0_w_in, l0_w_out, l0_q_norm, l0_k_norm, l0_idx_k_norm, l1_norm, l1_w_in, l1_w_out, l1_q_norm, l1_k_norm, l1_lambda_q1, l1_lambda_k1, l1_lambda_q2, l1_lambda_k2, l1_subln, l2_norm, l2_w_in, l2_w_out, l2_q_norm, l2_k_norm, l2_forget_bias, l3_norm, l3_w_in, l3_w_out, l3_q_norm, l3_k_norm, l3_idx_k_norm):
    bp, tp, d = x_prompt.shape
    bs, ts, _ = x_sample.shape
    s_valid = PAST_LEN + ts
    s_pad = -(-s_valid // LANES) * LANES
    pr = _Rows(bp, tp, 0, 512)
    sr = _Rows(bs, ts, PAST_LEN, bs * ts)
    a_p = _Attn(bp, tp, tp, tp, 0, 256, 512, jnp.float8_e4m3fn)
    a_s = _Attn(bs, ts, s_pad, s_valid, PAST_LEN, ts, s_pad, BF16)
    bc_p = _Attn(bp, tp, tp, tp, 0, 512, 512, jnp.float8_e4m3fn)
    bc_s = a_s

    xp = x_prompt.reshape(bp * tp, d)
    xs = x_sample.reshape(bs * ts, d)
    xp, xs, st0 = _layer_a(pr, sr, a_p, a_s, xp, xs, cache_l0_k, cache_l0_v, cache_l0_kidx,
                           l0_norm, l0_w_in, l0_w_out, l0_q_norm, l0_k_norm, l0_idx_k_norm)
    xp, xs, st1 = _layer_b(1, pr, sr, bc_p, bc_s, xp, xs, cache_l1_k, cache_l1_v,
                           l1_norm, l1_w_in, l1_w_out, l1_q_norm, l1_k_norm,
                           l1_lambda_q1, l1_lambda_k1, l1_lambda_q2, l1_lambda_k2, l1_subln)
    xp, xs, st2 = _layer_c(pr, sr, bc_p, bc_s, xp, xs, cache_l2_k, cache_l2_v, cache_l2_logf,
                           l2_norm, l2_w_in, l2_w_out, l2_q_norm, l2_k_norm, l2_forget_bias)
    xp, xs, st3 = _layer_a(pr, sr, a_p, a_s, xp, xs, cache_l3_k, cache_l3_v, cache_l3_kidx,
                           l3_norm, l3_w_in, l3_w_out, l3_q_norm, l3_k_norm, l3_idx_k_norm)
    return (xp.reshape(bp, tp, d), xs.reshape(bs, ts, d)) + st0 + st1 + st2 + st3
```
